```python
import math
import jax, jax.numpy as jnp
from jax import lax
import numpy as np

D_MODEL = 1024
BATCH = 8
SEQ = 2048
DEPTH = 1
DEC_BATCH = 32
DEC_SEQ = 8
PAST_LEN = 16384
PAGE_SIZE = 128

N_META = 16
D_FF = 2816
EPS = 1e-6
D_INNER = 2 * D_MODEL
SSM_HEAD_DIM = 64
SSM_HEADS = D_INNER // SSM_HEAD_DIM
SSM_GROUPS = 4
HEADS_PER_GROUP = SSM_HEADS // SSM_GROUPS
D_STATE = 128
CONV_K = 4
CONV_DIM = D_INNER + 2 * SSM_GROUPS * D_STATE
CHUNK = 128
MLA_HEADS = 16
Q_LORA = D_MODEL // 2
KV_LORA = D_MODEL // 4
QK_NOPE = 64
QK_ROPE = 32
V_HEAD = 64
ROPE_THETA = 10000.0
ATTN_SCALE = (QK_NOPE + QK_ROPE) ** -0.5
Q_BLOCK = 128
IN_SIZES = (D_INNER, CONV_DIM, SSM_HEADS, Q_LORA, KV_LORA, QK_ROPE, D_MODEL, D_MODEL)
N_IN = D_INNER + CONV_DIM + SSM_HEADS + Q_LORA + KV_LORA + QK_ROPE + 2 * D_MODEL

kernel_name = "hybrid_ssd_mla_gated_decoder_step"

F32 = jnp.float32


def rmsnorm(x, g):
    xf = x.astype(F32)
    y = xf * lax.rsqrt(jnp.mean(xf * xf, axis=-1, keepdims=True) + EPS)
    return (y * g.astype(F32)).astype(x.dtype)


def group_rmsnorm(x, g, groups):
    shp = x.shape
    xf = x.astype(F32).reshape(*shp[:-1], groups, shp[-1] // groups)
    y = xf * lax.rsqrt(jnp.mean(xf * xf, axis=-1, keepdims=True) + EPS)
    return (y.reshape(shp) * g.astype(F32)).astype(x.dtype)


def swiglu(x, w_gate, w_up, w_down):
    return (jax.nn.silu(x @ w_gate) * (x @ w_up)) @ w_down


def rope_tables(pos):
    inv = ROPE_THETA ** (-jnp.arange(0, QK_ROPE, 2, dtype=F32) / QK_ROPE)
    ang = pos.astype(F32)[:, None] * inv[None, :]
    return jnp.cos(ang), jnp.sin(ang)


def apply_rope(t, cos, sin):
    tf = t.astype(F32)
    half = QK_ROPE // 2
    t1, t2 = tf[..., :half], tf[..., half:]
    return jnp.concatenate([t1 * cos - t2 * sin, t1 * sin + t2 * cos], axis=-1).astype(t.dtype)


def causal_dwconv(x_hist, w, b):
    out = lax.conv_general_dilated(
        x_hist, w.astype(x_hist.dtype)[:, None, :], window_strides=(1,), padding='VALID',
        dimension_numbers=('NWC', 'WIO', 'NWC'), feature_group_count=x_hist.shape[-1])
    return out + b.astype(x_hist.dtype)


def ssd_scan(x, dt, b_in, c_in, a, d_skip, h0, lead):
    bsz, L = x.shape[0], x.shape[1]
    trail = (-(lead + L)) % CHUNK
    def pad(t):
        return jnp.pad(t.astype(F32), [(0, 0), (lead, trail)] + [(0, 0)] * (t.ndim - 2))
    xp, dtp, bp, cp = pad(x), pad(dt), pad(b_in), pad(c_in)
    nc = xp.shape[1] // CHUNK
    G, E, P, N = SSM_GROUPS, HEADS_PER_GROUP, SSM_HEAD_DIM, D_STATE
    xp = xp.reshape(bsz, nc, CHUNK, G, E, P)
    dtp = dtp.reshape(bsz, nc, CHUNK, G, E)
    bp = bp.reshape(bsz, nc, CHUNK, G, N)
    cp = cp.reshape(bsz, nc, CHUNK, G, N)
    a_cs = jnp.cumsum(dtp * a.astype(F32).reshape(G, E), axis=2)
    xdt = xp * dtp[..., None]
    a_t = jnp.moveaxis(a_cs, 2, -1)
    seg = a_t[..., :, None] - a_t[..., None, :]
    causal = jnp.tril(jnp.ones((CHUNK, CHUNK), dtype=bool))
    decay_in = jnp.exp(jnp.where(causal, seg, -jnp.inf))
    cb = jnp.einsum('bclgn,bcsgn->bcgls', cp, bp)
    y_diag = jnp.einsum('bcgels,bcsgep->bclgep', cb[:, :, :, None] * decay_in, xdt)
    decay_states = jnp.exp(a_cs[:, :, -1:] - a_cs)
    states = jnp.einsum('bclgn,bclge,bclgep->bcgepn', bp, decay_states, xdt)
    chunk_decay = jnp.exp(a_cs[:, :, -1])
    def step(h, inp):
        s, dcy = inp
        return h * dcy[..., None, None] + s, h
    h_init = h0.astype(F32).reshape(bsz, G, E, P, N)
    h_final, h_prev = lax.scan(step, h_init, (jnp.moveaxis(states, 1, 0), jnp.moveaxis(chunk_decay, 1, 0)))
    h_prev = jnp.moveaxis(h_prev, 0, 1)
    y_off = jnp.einsum('bclgn,bcgepn,bclge->bclgep', cp, h_prev, jnp.exp(a_cs))
    y = y_diag + y_off + xp * d_skip.astype(F32).reshape(G, E)[..., None]
    y = y.reshape(bsz, nc * CHUNK, SSM_HEADS, P)[:, lead:lead + L]
    return y.astype(x.dtype), h_final.reshape(bsz, SSM_HEADS, P, N).astype(x.dtype)


def mla_scores(q_lat, q_pe, c_kv, k_pe):
    s = jnp.einsum('bthc,bsc->bhts', q_lat, c_kv) + jnp.einsum('bthr,bsr->bhts', q_pe, k_pe)
    return s.astype(F32) * ATTN_SCALE


def attend_prompt(q_lat, q_pe, c_kv, k_pe):
    bsz, L = q_lat.shape[0], q_lat.shape[1]
    nb = -(-L // Q_BLOCK)
    lq = nb * Q_BLOCK
    def blocks(t):
        t = jnp.pad(t, [(0, 0), (0, lq - L), (0, 0), (0, 0)])
        return jnp.swapaxes(t.reshape(bsz, nb, Q_BLOCK, *t.shape[2:]), 0, 1)
    qpos = jnp.arange(lq).reshape(nb, Q_BLOCK)
    kpos = jnp.arange(L)
    def one_block(args):
        ql, qr, qp = args
        s = mla_scores(ql, qr, c_kv, k_pe)
        s = jnp.where(kpos[None, :] <= qp[:, None], s, -jnp.inf)
        pr = jax.nn.softmax(s, axis=-1).astype(c_kv.dtype)
        return jnp.einsum('bhts,bsc->bthc', pr, c_kv)
    out = lax.map(one_block, (blocks(q_lat), blocks(q_pe), qpos))
    return jnp.swapaxes(out, 0, 1).reshape(bsz, lq, MLA_HEADS, KV_LORA)[:, :L]


def make_attend_sample(past_c, past_kr):
    def attend(q_lat, q_pe, c_kv, k_pe):
        t = q_lat.shape[1]
        n_past = past_c.shape[1]
        s_past = mla_scores(q_lat, q_pe, past_c, past_kr)
        s_new = mla_scores(q_lat, q_pe, c_kv, k_pe)
        s_new = jnp.where(jnp.tril(jnp.ones((t, t), dtype=bool)), s_new, -jnp.inf)
        pr = jax.nn.softmax(jnp.concatenate([s_past, s_new], axis=-1), axis=-1).astype(c_kv.dtype)
        return (jnp.einsum('bhts,bsc->bthc', pr[..., :n_past], past_c)
                + jnp.einsum('bhts,bsc->bthc', pr[..., n_past:], c_kv))
    return attend


def token_mixer(u, conv_hist, h0, pos, lead, attend, p):
    bsz, L = u.shape[0], u.shape[1]
    proj = u @ p['w_in']
    offs, acc = [], 0
    for s in IN_SIZES[:-1]:
        acc += s
        offs.append(acc)
    z, xbc, dt_raw, q_a, kv_a, k_pe, gate_a, gate_b = jnp.split(proj, offs, axis=-1)
    xbc_hist = jnp.concatenate([conv_hist.astype(xbc.dtype), xbc], axis=1)
    conv_new = xbc_hist[:, -(CONV_K - 1):]
    xbc_c = jax.nn.silu(causal_dwconv(xbc_hist, p['conv_w'], p['conv_b']))
    xs, b_in, c_in = jnp.split(xbc_c, [D_INNER, D_INNER + SSM_GROUPS * D_STATE], axis=-1)
    dt = jax.nn.softplus(dt_raw.astype(F32) + p['dt_bias'].astype(F32))
    a = -jnp.exp(p['a_log'].astype(F32))
    y, h_new = ssd_scan(xs.reshape(bsz, L, SSM_HEADS, SSM_HEAD_DIM), dt,
                        b_in.reshape(bsz, L, SSM_GROUPS, D_STATE), c_in.reshape(bsz, L, SSM_GROUPS, D_STATE),
                        a, p['d_skip'], h0, lead)
    y = y.reshape(bsz, L, D_INNER) * jax.nn.silu(z)
    branch_a = group_rmsnorm(y, p['ssm_norm_g'], SSM_GROUPS) @ p['w_a_out']
    q = (rmsnorm(q_a, p['q_a_norm_g']) @ p['w_q_b']).reshape(bsz, L, MLA_HEADS, QK_NOPE + QK_ROPE)
    q_nope, q_pe = q[..., :QK_NOPE], q[..., QK_NOPE:]
    cos, sin = rope_tables(pos)
    q_pe = apply_rope(q_pe, cos[:, None, :], sin[:, None, :])
    k_pe = apply_rope(k_pe, cos, sin)
    c_kv = rmsnorm(kv_a, p['kv_a_norm_g'])
    q_lat = jnp.einsum('bthn,chn->bthc', q_nope, p['w_uk'])
    o_lat = attend(q_lat, q_pe, c_kv, k_pe)
    o = jnp.einsum('bthc,chv->bthv', o_lat, p['w_uv']).reshape(bsz, L, MLA_HEADS * V_HEAD)
    branch_b = o @ p['w_b_out']
    merged = jax.nn.sigmoid(gate_a) * branch_a + jax.nn.sigmoid(gate_b) * branch_b
    return merged @ p['w_o'], c_kv, k_pe, h_new, conv_new


def decoder_layer(x, conv_hist, h0, pos, lead, attend, p):
    h = x + 0.5 * rmsnorm(swiglu(rmsnorm(x, p['ffn1_pre_g']), p['ffn1_w_gate'], p['ffn1_w_up'], p['ffn1_w_down']),
                          p['ffn1_post_g'])
    mix, c_kv, k_pe, h_new, conv_new = token_mixer(rmsnorm(h, p['mix_pre_g']), conv_hist, h0, pos, lead, attend, p)
    h = h + rmsnorm(mix, p['mix_post_g'])
    h = h + 0.5 * rmsnorm(swiglu(rmsnorm(h, p['ffn2_pre_g']), p['ffn2_w_gate'], p['ffn2_w_up'], p['ffn2_w_down']),
                          p['ffn2_post_g'])
    return h, c_kv, k_pe, h_new, conv_new


def setup_inputs(seed: int = 0) -> dict:
    key = jax.random.key(seed)
    ks = iter(jax.random.split(key, 64))
    def nrm(shape, scale):
        return jax.random.normal(next(ks), shape, F32) * scale
    def gain(shape):
        return 1.0 + 0.05 * jax.random.normal(next(ks), shape, F32)
    n_pages = PAST_LEN // PAGE_SIZE
    n_used = DEC_BATCH * n_pages
    n_pool = n_used + n_used // 4
    perm = jax.random.permutation(next(ks), n_pool).astype(jnp.int32)
    page_table = perm[:n_used].reshape(DEC_BATCH, n_pages)
    dt0 = jnp.exp(jax.random.uniform(next(ks), (DEPTH, SSM_HEADS), F32, math.log(1e-3), math.log(1e-1)))
    dt_bias = dt0 + jnp.log(-jnp.expm1(-dt0))
    a_log = jnp.log(jax.random.uniform(next(ks), (DEPTH, SSM_HEADS), F32, 1.0, 16.0))
    return {
        'x_prompt': nrm((BATCH, SEQ, D_MODEL), 1.0),
        'x_sample': nrm((DEC_BATCH, DEC_SEQ, D_MODEL), 1.0),
        'cache_kv_latent': nrm((DEPTH, n_pool, PAGE_SIZE, KV_LORA), 1.0),
        'cache_k_rope': nrm((DEPTH, n_pool, PAGE_SIZE, QK_ROPE), 1.0),
        'state_ssm': nrm((DEPTH, DEC_BATCH, SSM_HEADS, SSM_HEAD_DIM, D_STATE), 0.3),
        'state_conv': nrm((DEPTH, DEC_BATCH, CONV_K - 1, CONV_DIM), 1.0),
        'page_table': page_table,
        'meta_tokens': nrm((N_META, D_MODEL), 1.0),
        'ffn1_pre_g': gain((DEPTH, D_MODEL)),
        'ffn1_w_gate': nrm((DEPTH, D_MODEL, D_FF), D_MODEL ** -0.5),
        'ffn1_w_up': nrm((DEPTH, D_MODEL, D_FF), D_MODEL ** -0.5),
        'ffn1_w_down': nrm((DEPTH, D_FF, D_MODEL), D_FF ** -0.5),
        'ffn1_post_g': gain((DEPTH, D_MODEL)),
        'mix_pre_g': gain((DEPTH, D_MODEL)),
        'w_in': nrm((DEPTH, D_MODEL, N_IN), D_MODEL ** -0.5),
        'conv_w': nrm((DEPTH, CONV_K, CONV_DIM), CONV_K ** -0.5),
        'conv_b': nrm((DEPTH, CONV_DIM), 0.02),
        'dt_bias': dt_bias,
        'a_log': a_log,
        'd_skip': gain((DEPTH, SSM_HEADS)),
        'ssm_norm_g': gain((DEPTH, D_INNER)),
        'q_a_norm_g': gain((DEPTH, Q_LORA)),
        'w_q_b': nrm((DEPTH, Q_LORA, MLA_HEADS * (QK_NOPE + QK_ROPE)), Q_LORA ** -0.5),
        'kv_a_norm_g': gain((DEPTH, KV_LORA)),
        'w_uk': nrm((DEPTH, KV_LORA, MLA_HEADS, QK_NOPE), KV_LORA ** -0.5),
        'w_uv': nrm((DEPTH, KV_LORA, MLA_HEADS, V_HEAD), KV_LORA ** -0.5),
        'w_a_out': nrm((DEPTH, D_INNER, D_MODEL), D_INNER ** -0.5),
        'w_b_out': nrm((DEPTH, MLA_HEADS * V_HEAD, D_MODEL), (MLA_HEADS * V_HEAD) ** -0.5),
        'w_o': nrm((DEPTH, D_MODEL, D_MODEL), D_MODEL ** -0.5),
        'mix_post_g': gain((DEPTH, D_MODEL)),
        'ffn2_pre_g': gain((DEPTH, D_MODEL)),
        'ffn2_w_gate': nrm((DEPTH, D_MODEL, D_FF), D_MODEL ** -0.5),
        'ffn2_w_up': nrm((DEPTH, D_MODEL, D_FF), D_MODEL ** -0.5),
        'ffn2_w_down': nrm((DEPTH, D_FF, D_MODEL), D_FF ** -0.5),
        'ffn2_post_g': gain((DEPTH, D_MODEL)),
    }


def reference(x_prompt, x_sample, cache_kv_latent, cache_k_rope, state_ssm, state_conv, page_table, meta_tokens,
              ffn1_pre_g, ffn1_w_gate, ffn1_w_up, ffn1_w_down, ffn1_post_g,
              mix_pre_g, w_in, conv_w, conv_b, dt_bias, a_log, d_skip, ssm_norm_g,
              q_a_norm_g, w_q_b, kv_a_norm_g, w_uk, w_uv, w_a_out, w_b_out, w_o, mix_post_g,
              ffn2_pre_g, ffn2_w_gate, ffn2_w_up, ffn2_w_down, ffn2_post_g):
    bp, seq = x_prompt.shape[0], x_prompt.shape[1]
    bs, tdec = x_sample.shape[0], x_sample.shape[1]
    past_len = page_table.shape[1] * cache_kv_latent.shape[2]
    hp = jnp.concatenate([jnp.broadcast_to(meta_tokens.astype(x_prompt.dtype)[None], (bp, N_META, D_MODEL)),
                          x_prompt], axis=1)
    pos_p = jnp.arange(N_META + seq)
    lead_p = (-N_META) % CHUNK
    conv0_p = jnp.zeros((bp, CONV_K - 1, CONV_DIM), x_prompt.dtype)
    ssm0_p = jnp.zeros((bp, SSM_HEADS, SSM_HEAD_DIM, D_STATE), x_prompt.dtype)
    hs = x_sample
    pos_s = past_len + jnp.arange(tdec)
    kvp, krp, ssp, cvp, kvs, krs, sss, cvs = [], [], [], [], [], [], [], []
    for l in range(DEPTH):
        p = dict(ffn1_pre_g=ffn1_pre_g[l], ffn1_w_gate=ffn1_w_gate[l], ffn1_w_up=ffn1_w_up[l],
                 ffn1_w_down=ffn1_w_down[l], ffn1_post_g=ffn1_post_g[l], mix_pre_g=mix_pre_g[l],
                 w_in=w_in[l], conv_w=conv_w[l], conv_b=conv_b[l], dt_bias=dt_bias[l], a_log=a_log[l],
                 d_skip=d_skip[l], ssm_norm_g=ssm_norm_g[l], q_a_norm_g=q_a_norm_g[l], w_q_b=w_q_b[l],
                 kv_a_norm_g=kv_a_norm_g[l], w_uk=w_uk[l], w_uv=w_uv[l], w_a_out=w_a_out[l],
                 w_b_out=w_b_out[l], w_o=w_o[l], mix_post_g=mix_post_g[l], ffn2_pre_g=ffn2_pre_g[l],
                 ffn2_w_gate=ffn2_w_gate[l], ffn2_w_up=ffn2_w_up[l], ffn2_w_down=ffn2_w_down[l],
                 ffn2_post_g=ffn2_post_g[l])
        hp, c_p, k_p, h_p, cv_p = decoder_layer(hp, conv0_p, ssm0_p, pos_p, lead_p, attend_prompt, p)
        past_c = cache_kv_latent[l][page_table].reshape(bs, past_len, KV_LORA).astype(x_sample.dtype)
        past_kr = cache_k_rope[l][page_table].reshape(bs, past_len, QK_ROPE).astype(x_sample.dtype)
        hs, c_s, k_s, h_s, cv_s = decoder_layer(hs, state_conv[l], state_ssm[l], pos_s, 0,
                                                make_attend_sample(past_c, past_kr), p)
        kvp.append(c_p); krp.append(k_p); ssp.append(h_p); cvp.append(cv_p)
        kvs.append(c_s); krs.append(k_s); sss.append(h_s); cvs.append(cv_s)
    y_prompt = hp[:, N_META:]
    y_sample = hs
    return (y_prompt, y_sample,
            jnp.stack(kvp), jnp.stack(krp), jnp.stack(ssp), jnp.stack(cvp),
            jnp.stack(kvs), jnp.stack(krs), jnp.stack(sss), jnp.stack(cvs))
```

```python
import functools
import math

import numpy as np
import jax
import jax.numpy as jnp
from jax import lax
from jax.experimental import pallas as pl
from jax.experimental.pallas import tpu as pltpu

F32 = jnp.float32
BF16 = jnp.bfloat16
HIGHEST = lax.Precision.HIGHEST

EPS = 1e-6
N_META = 16
CHUNK = 128
LEAD = (-N_META) % CHUNK
SSM_HEAD_DIM = 64
SSM_GROUPS = 4
D_STATE = 128
CONV_K = 4
MLA_HEADS = 16
QK_NOPE = 64
QK_ROPE = 32
V_HEAD = 64
ROPE_THETA = 10000.0
ATTN_SCALE = (QK_NOPE + QK_ROPE) ** -0.5
NEG = -1e30
SLAB = 1024
QK_DIM = 384
MIB = 1024 * 1024


def _cparams(sem, vmem_mib):
    return pltpu.CompilerParams(dimension_semantics=sem, vmem_limit_bytes=int(vmem_mib * MIB))


def _resident(shape):
    nd = len(shape)
    return pl.BlockSpec(shape, lambda *_: (0,) * nd, pipeline_mode=pl.Buffered(1))


def _rms(x, g):
    return x * lax.rsqrt(jnp.mean(x * x, axis=-1, keepdims=True) + EPS) * g


def _dot(a, b):
    return jnp.dot(a, b, preferred_element_type=F32)


def _dot_exact(a, b):
    return jnp.dot(a, b, preferred_element_type=F32, precision=HIGHEST)


def _dot_nt(a, b):
    return lax.dot_general(a, b, (((1,), (1,)), ((), ())), preferred_element_type=F32)


def _softplus(x):
    return jnp.maximum(x, 0.0) + jnp.log1p(jnp.exp(-jnp.abs(x)))


def _row_tile(m, cap):
    for t in range(min(cap, m), 15, -1):
        if m % t == 0 and t % 16 == 0:
            return t
    raise ValueError(f"no row tile for {m}")


def _ffn_kernel(x_ref, pre_ref, wg_ref, wu_ref, wd_ref, post_ref, o_ref, *, chunks):
    x = x_ref[...]
    xn = _rms(x, pre_ref[...]).astype(BF16)
    acc = jnp.zeros(x.shape, F32)
    for lo, sz in chunks:
        g = _dot(xn, wg_ref[:, lo:lo + sz])
        u = _dot(xn, wu_ref[:, lo:lo + sz])
        h = (jax.nn.silu(g) * u).astype(BF16)
        acc = acc + _dot(h, wd_ref[lo:lo + sz, :])
    o_ref[...] = x + 0.5 * _rms(acc, post_ref[...])


def _ffn_chunks(f):
    out, lo = [], 0
    while lo < f:
        sz = min(1024, f - lo)
        out.append((lo, sz))
        lo += sz
    return tuple(out)


def _ffn(x, pre_g, wg, wu, wd, post_g):
    m, d = x.shape
    f = wg.shape[1]
    tm = _row_tile(m, 512)
    return pl.pallas_call(
        functools.partial(_ffn_kernel, chunks=_ffn_chunks(f)),
        out_shape=jax.ShapeDtypeStruct((m, d), F32),
        grid=(m // tm,),
        in_specs=[pl.BlockSpec((tm, d), lambda i: (i, 0)), _resident((1, d)), _resident((d, f)),
                  _resident((d, f)), _resident((f, d)), _resident((1, d))],
        out_specs=pl.BlockSpec((tm, d), lambda i: (i, 0)),
        compiler_params=_cparams(("arbitrary",), 52),
        name="ffn_block",
    )(x, pre_g, wg, wu, wd, post_g)


def _inproj_kernel(x_ref, g_ref, wbig_ref, wsmall_ref, proj_ref, small_ref, *, nslab):
    xn = _rms(x_ref[...], g_ref[...]).astype(BF16)
    for s in range(nslab):
        proj_ref[s] = _dot(xn, wbig_ref[:, s * SLAB:(s + 1) * SLAB]).astype(proj_ref.dtype)
    small_ref[...] = _dot(xn, wsmall_ref[...])


def _inproj(h, g, wbig, wsmall, proj_dtype):
    m, d = h.shape
    nslab = wbig.shape[1] // SLAB
    ns = wsmall.shape[1]
    tm = _row_tile(m, 512)
    return pl.pallas_call(
        functools.partial(_inproj_kernel, nslab=nslab),
        out_shape=(jax.ShapeDtypeStruct((nslab, m, SLAB), proj_dtype), jax.ShapeDtypeStruct((m, ns), F32)),
        grid=(m // tm,),
        in_specs=[pl.BlockSpec((tm, d), lambda i: (i, 0)), _resident((1, d)), _resident(wbig.shape),
                  _resident(wsmall.shape)],
        out_specs=(pl.BlockSpec((nslab, tm, SLAB), lambda i: (0, i, 0)), pl.BlockSpec((tm, ns), lambda i: (i, 0))),
        compiler_params=_cparams(("arbitrary",), 52),
        name="in_proj",
    )(h, g, wbig, wsmall)


def _qkv_kernel(s7_ref, small_ref, pos_ref, qg_ref, kvg_ref, wq_ref, wuk_ref, inv_ref, sign_ref,
                ckv_ref, kpe_ref, kcat_ref, q_ref, *, tqb):
    s7 = s7_ref[0].astype(F32)
    tm = s7.shape[0]
    q_lora = qg_ref.shape[1]
    kv_lora = kvg_ref.shape[1]
    nh = MLA_HEADS
    c_kv = _rms(s7[:, q_lora:q_lora + kv_lora], kvg_ref[...])
    ckv_ref[...] = c_kv
    ang = pos_ref[...] * inv_ref[...]
    cos = jnp.cos(ang)
    sin = jnp.sin(ang) * sign_ref[...]
    small = small_ref[...]
    kpe = small[:, 128:256] * cos + small[:, 256:384] * sin
    kpe_ref[...] = kpe
    kcat_ref[:, 0:kv_lora] = c_kv.astype(kcat_ref.dtype)
    kcat_ref[:, kv_lora:kv_lora + 128] = kpe.astype(kcat_ref.dtype)

    qn = _rms(s7[:, 0:q_lora], qg_ref[...]).astype(BF16)
    nope_w = nh * QK_NOPE
    pe_w = nh * QK_ROPE
    q_nope = _dot(qn, wq_ref[:, 0:nope_w])
    q_pe = _dot(qn, wq_ref[:, nope_w:nope_w + pe_w])
    q_rot = _dot(qn, wq_ref[:, nope_w + pe_w:nope_w + 2 * pe_w])
    cos_w = jnp.concatenate([cos] * (pe_w // 128), axis=1)
    sin_w = jnp.concatenate([sin] * (pe_w // 128), axis=1)
    q_pe = (q_pe * cos_w + q_rot * sin_w) * ATTN_SCALE
    lane = lax.broadcasted_iota(jnp.int32, (1, 128), 1)
    nblk = tm // tqb
    for p in range(nh // 2):
        qn_pair = q_nope[:, p * 128:(p + 1) * 128].astype(BF16)
        q_lat = _dot(qn_pair, wuk_ref[p]) * ATTN_SCALE
        for e in range(2):
            hd = 2 * p + e
            pe_blk = q_pe[:, (hd // 4) * 128:(hd // 4 + 1) * 128]
            sel = (lane >= (hd % 4) * QK_ROPE) & (lane < (hd % 4 + 1) * QK_ROPE)
            pe_blk = jnp.where(sel, pe_blk, 0.0)
            lat = q_lat[:, e * kv_lora:(e + 1) * kv_lora]
            for g in range(nblk):
                q_ref[g, hd, :, 0:kv_lora] = lat[g * tqb:(g + 1) * tqb].astype(q_ref.dtype)
                q_ref[g, hd, :, kv_lora:kv_lora + 128] = pe_blk[g * tqb:(g + 1) * tqb].astype(q_ref.dtype)


def _qkv_prep(proj3, small, pos, qg, kvg, wq, wuk, inv128, sign128, tqb, dtype):
    nslab, m, _ = proj3.shape
    kv_lora = kvg.shape[1]
    tm = _row_tile(m, 512)
    assert tm % tqb == 0
    nblk = tm // tqb
    return pl.pallas_call(
        functools.partial(_qkv_kernel, tqb=tqb),
        out_shape=(jax.ShapeDtypeStruct((m, kv_lora), F32), jax.ShapeDtypeStruct((m, 128), F32),
                   jax.ShapeDtypeStruct((m, QK_DIM), dtype),
                   jax.ShapeDtypeStruct((m // tqb, MLA_HEADS, tqb, QK_DIM), dtype)),
        grid=(m // tm,),
        in_specs=[pl.BlockSpec((1, tm, SLAB), lambda i: (nslab - 1, i, 0)),
                  pl.BlockSpec((tm, small.shape[1]), lambda i: (i, 0)),
                  pl.BlockSpec((tm, 1), lambda i: (i, 0)),
                  _resident(qg.shape), _resident(kvg.shape), _resident(wq.shape), _resident(wuk.shape),
                  _resident((1, 128)), _resident((1, 128))],
        out_specs=(pl.BlockSpec((tm, kv_lora), lambda i: (i, 0)), pl.BlockSpec((tm, 128), lambda i: (i, 0)),
                   pl.BlockSpec((tm, QK_DIM), lambda i: (i, 0)),
                   pl.BlockSpec((nblk, MLA_HEADS, tqb, QK_DIM), lambda i: (i, 0, 0, 0))),
        compiler_params=_cparams(("arbitrary",), 48),
        name="qkv_prep",
    )(proj3, small, pos, qg, kvg, wq, wuk, inv128, sign128)


def _ssd_chunk(xh_ref, dt_raw, state_ref, apad_ref, y_ref, lo, hi, p):
    nheads = p["dt_bias"].shape[1]
    d_inner = nheads * SSM_HEAD_DIM
    gw = d_inner // SSM_GROUPS
    rows = lax.broadcasted_iota(jnp.int32, (CHUNK, 1), 0)
    valid = (rows >= lo) & (rows < hi)

    conv = p["conv_b"][...]
    for k in range(CONV_K):
        conv = conv + p["conv_w"][k:k + 1, :] * xh_ref[pl.ds(8 - (CONV_K - 1) + k, CHUNK), :]
    xc = jnp.where(valid, jax.nn.silu(conv), 0.0)
    xs = xc[:, 0:d_inner]
    bm = xc[:, d_inner:d_inner + SSM_GROUPS * D_STATE]
    cm = xc[:, d_inner + SSM_GROUPS * D_STATE:]

    dt = jnp.where(valid, _softplus(dt_raw[:, 0:nheads] + p["dt_bias"][...]), 0.0)
    da = dt * (-jnp.exp(p["a_log"][...]))
    a_cs = _dot_exact(p["tril"][...], da)
    expand = p["expand"][...]
    dt_x = _dot_exact(dt, expand)
    acs_x = _dot_exact(a_cs, expand)
    alast_x = acs_x[CHUNK - 1:CHUNK, :]
    acs_b = _dot_exact(a_cs, p["expand128"][...])
    apad_ref[...] = _dot_exact(a_cs, p["ipad"][...]).T
    xdt = xs * dt_x
    xds = (xdt * jnp.exp(alast_x - acs_x)).astype(BF16)
    exp_acs = jnp.exp(acs_x)
    causal = rows >= lax.broadcasted_iota(jnp.int32, (1, CHUNK), 1)
    lane = lax.broadcasted_iota(jnp.int32, (1, 128), 1)
    hpg = nheads // SSM_GROUPS

    for g in range(SSM_GROUPS):
        gs = slice(g * gw, (g + 1) * gw)
        bg = bm[:, g * D_STATE:(g + 1) * D_STATE]
        cg = cm[:, g * D_STATE:(g + 1) * D_STATE].astype(BF16)
        cb = _dot_nt(cg, bg.astype(BF16))
        st = state_ref[:, gs]
        y_off = _dot(cg, st.astype(BF16)) * exp_acs[:, gs]
        state_ref[:, gs] = st * jnp.exp(alast_x[:, gs]) + _dot(bg.T.astype(BF16), xds[:, gs])
        y_ref[:, gs] = y_off + xs[:, gs] * p["dskip_x"][:, gs]
        for pr in range(hpg // 2):
            w2 = []
            for e in range(2):
                hd = g * hpg + 2 * pr + e
                seg = acs_b[:, hd * 128:(hd + 1) * 128] - apad_ref[pl.ds(hd, 1), :]
                decay = jnp.exp(jnp.where(causal, seg, NEG))
                w2.append((cb * decay).astype(BF16))
            ls = slice(g * gw + pr * 128, g * gw + (pr + 1) * 128)
            xp = xdt[:, ls]
            x2 = jnp.concatenate([jnp.where(lane < SSM_HEAD_DIM, xp, 0.0),
                                  jnp.where(lane >= SSM_HEAD_DIM, xp, 0.0)], axis=0).astype(BF16)
            y_ref[:, ls] = y_ref[:, ls] + _dot(jnp.concatenate(w2, axis=1), x2)


def _gate_norm(y, z, g, d_inner):
    yg = y * jax.nn.silu(z)
    gw = d_inner // SSM_GROUPS
    outs = []
    for k in range(SSM_GROUPS):
        v = yg[:, k * gw:(k + 1) * gw]
        outs.append(_rms(v, g[:, k * gw:(k + 1) * gw]))
    return jnp.concatenate(outs, axis=1)


_SSD_PARAM_NAMES = ("conv_w", "conv_b", "dt_bias", "a_log", "dskip_x", "norm_g", "tril", "expand", "expand128",
                    "ipad")


def _ssd_prompt_kernel(x0_ref, x1_ref, bc_ref, z0_ref, z1_ref, small_ref, *rest, nc):
    np_ = len(_SSD_PARAM_NAMES)
    p = dict(zip(_SSD_PARAM_NAMES, rest[:np_]))
    yn_ref, hfin_ref, convnew_ref, xh_ref, state_ref, apad_ref, y_ref = rest[np_:]
    c = pl.program_id(1)
    d_inner = state_ref.shape[1]

    @pl.when(c == 0)
    def _():
        xh_ref[0:8, :] = jnp.zeros((8, xh_ref.shape[1]), F32)
        state_ref[...] = jnp.zeros(state_ref.shape, F32)

    lo = jnp.where(c == 0, LEAD, 0)
    rows = lax.broadcasted_iota(jnp.int32, (CHUNK, 1), 0)
    valid = rows >= lo
    xh_ref[8:8 + CHUNK, 0:SLAB] = jnp.where(valid, x0_ref[0].astype(F32), 0.0)
    xh_ref[8:8 + CHUNK, SLAB:2 * SLAB] = jnp.where(valid, x1_ref[0].astype(F32), 0.0)
    xh_ref[8:8 + CHUNK, 2 * SLAB:3 * SLAB] = jnp.where(valid, bc_ref[0].astype(F32), 0.0)
    _ssd_chunk(xh_ref, small_ref[:, 0:128], state_ref, apad_ref, y_ref, lo, CHUNK, p)
    z = jnp.concatenate([z0_ref[0], z1_ref[0]], axis=1).astype(F32)
    yn_ref[...] = _gate_norm(y_ref[...], z, p["norm_g"][...], d_inner).astype(yn_ref.dtype)
    xh_ref[0:8, :] = xh_ref[CHUNK:CHUNK + 8, :]

    @pl.when(c == nc - 1)
    def _():
        convnew_ref[0] = xh_ref[pl.ds(8 + CHUNK - (CONV_K - 1), CONV_K - 1), :]
        for j in range(d_inner // 128):
            hfin_ref[0, j * 128:(j + 1) * 128, :] = state_ref[:, j * 128:(j + 1) * 128].T


def _ssd_prompt(proj3, small, params, bp, nc):
    d_inner = params["expand"].shape[1]
    conv_dim = params["conv_b"].shape[1]
    m = proj3.shape[1]
    ns = small.shape[1]
    plist = [params[k] for k in _SSD_PARAM_NAMES]

    def slab(s):
        return pl.BlockSpec((1, CHUNK, SLAB), lambda b, c, s=s: (s, b * nc + c, 0))

    return pl.pallas_call(
        functools.partial(_ssd_prompt_kernel, nc=nc),
        out_shape=(jax.ShapeDtypeStruct((m, d_inner), BF16),
                   jax.ShapeDtypeStruct((bp, d_inner, D_STATE), F32),
                   jax.ShapeDtypeStruct((bp, CONV_K - 1, conv_dim), F32)),
        grid=(bp, nc),
        in_specs=[slab(2), slab(3), slab(4), slab(0), slab(1),
                  pl.BlockSpec((CHUNK, ns), lambda b, c: (b * nc + c, 0))] + [_resident(a.shape) for a in plist],
        out_specs=(pl.BlockSpec((CHUNK, d_inner), lambda b, c: (b * nc + c, 0)),
                   pl.BlockSpec((1, d_inner, D_STATE), lambda b, c: (b, 0, 0)),
                   pl.BlockSpec((1, CONV_K - 1, conv_dim), lambda b, c: (b, 0, 0))),
        scratch_shapes=[pltpu.VMEM((8 + CHUNK, conv_dim), F32), pltpu.VMEM((D_STATE, d_inner), F32),
                        pltpu.VMEM((CHUNK, 128), F32), pltpu.VMEM((CHUNK, d_inner), F32)],
        compiler_params=_cparams(("arbitrary", "arbitrary"), 40),
        name="ssd_prompt",
    )(proj3, proj3, proj3, proj3, proj3, small, *plist)


def _ssd_sample_kernel(x0_ref, x1_ref, bc_ref, z0_ref, z1_ref, small_ref, conv0_ref, h0_ref, *rest, tdec):
    np_ = len(_SSD_PARAM_NAMES)
    p = dict(zip(_SSD_PARAM_NAMES, rest[:np_]))
    yn_ref, hnew_ref, convnew_ref, xh_ref, state_ref, apad_ref, y_ref, dt_ref = rest[np_:]
    d_inner = state_ref.shape[1]
    nk = CONV_K - 1
    xh_ref[...] = jnp.zeros(xh_ref.shape, F32)
    xh_ref[8 - nk:8, :] = conv0_ref[0]
    xh_ref[8:8 + tdec, 0:SLAB] = x0_ref[0]
    xh_ref[8:8 + tdec, SLAB:2 * SLAB] = x1_ref[0]
    xh_ref[8:8 + tdec, 2 * SLAB:3 * SLAB] = bc_ref[0]
    dt_ref[...] = jnp.zeros(dt_ref.shape, F32)
    dt_ref[0:tdec, :] = small_ref[:, 0:128]
    for j in range(d_inner // 128):
        state_ref[:, j * 128:(j + 1) * 128] = h0_ref[0, j * 128:(j + 1) * 128, :].T
    _ssd_chunk(xh_ref, dt_ref[...], state_ref, apad_ref, y_ref, 0, tdec, p)
    z = jnp.concatenate([z0_ref[0], z1_ref[0]], axis=1)
    yn_ref[...] = _gate_norm(y_ref[0:tdec, :], z, p["norm_g"][...], d_inner)
    convnew_ref[0] = xh_ref[pl.ds(8 + tdec - nk, nk), :]
    for j in range(d_inner // 128):
        hnew_ref[0, j * 128:(j + 1) * 128, :] = state_ref[:, j * 128:(j + 1) * 128].T


def _ssd_sample(proj3, small, conv0, h0, params, bs, tdec):
    d_inner = params["expand"].shape[1]
    conv_dim = params["conv_b"].shape[1]
    ns = small.shape[1]
    plist = [params[k] for k in _SSD_PARAM_NAMES]

    def slab(s):
        return pl.BlockSpec((1, tdec, SLAB), lambda b, s=s: (s, b, 0))

    return pl.pallas_call(
        functools.partial(_ssd_sample_kernel, tdec=tdec),
        out_shape=(jax.ShapeDtypeStruct((bs * tdec, d_inner), F32),
                   jax.ShapeDtypeStruct((bs, d_inner, D_STATE), F32),
                   jax.ShapeDtypeStruct((bs, CONV_K - 1, conv_dim), F32)),
        grid=(bs,),
        in_specs=[slab(2), slab(3), slab(4), slab(0), slab(1), pl.BlockSpec((tdec, ns), lambda b: (b, 0)),
                  pl.BlockSpec((1, CONV_K - 1, conv_dim), lambda b: (b, 0, 0)),
                  pl.BlockSpec((1, d_inner, D_STATE), lambda b: (b, 0, 0))] + [_resident(a.shape) for a in plist],
        out_specs=(pl.BlockSpec((tdec, d_inner), lambda b: (b, 0)),
                   pl.BlockSpec((1, d_inner, D_STATE), lambda b: (b, 0, 0)),
                   pl.BlockSpec((1, CONV_K - 1, conv_dim), lambda b: (b, 0, 0))),
        scratch_shapes=[pltpu.VMEM((8 + CHUNK, conv_dim), F32), pltpu.VMEM((D_STATE, d_inner), F32),
                        pltpu.VMEM((CHUNK, 128), F32), pltpu.VMEM((CHUNK, d_inner), F32),
                        pltpu.VMEM((CHUNK, 128), F32)],
        compiler_params=_cparams(("arbitrary",), 40),
        name="ssd_sample",
    )(proj3, proj3, proj3, proj3, proj3, small, conv0, h0, *plist)


def _attn_out(acc, l, wuv_ref, nh, tq):
    o_lat = acc / l
    outs = []
    for p in range(nh // 2):
        pair = jnp.concatenate([o_lat[(2 * p) * tq:(2 * p + 1) * tq], o_lat[(2 * p + 1) * tq:(2 * p + 2) * tq]],
                               axis=1).astype(BF16)
        outs.append(_dot(pair, wuv_ref[p]))
    return jnp.concatenate(outs, axis=1)


def _attn_prompt_kernel(q_ref, k_ref, v_ref, wuv_ref, o_ref, m_ref, l_ref, acc_ref, *, tk, lp):
    qi = pl.program_id(1)
    nh, tq = q_ref.shape[1], q_ref.shape[2]
    q = q_ref[0].reshape(nh * tq, q_ref.shape[3])
    m_ref[...] = jnp.full(m_ref.shape, NEG, F32)
    l_ref[...] = jnp.zeros(l_ref.shape, F32)
    acc_ref[...] = jnp.zeros(acc_ref.shape, F32)
    qidx = qi * tq + (lax.broadcasted_iota(jnp.int32, (nh * tq, 1), 0) & (tq - 1))
    nkc = ((qi + 1) * tq + tk - 1) // tk

    def body(kc, carry):
        start = jnp.minimum(kc * tk, lp - tk)
        start = pl.multiple_of(start, 128)
        k = k_ref[pl.ds(start, tk), :]
        v = v_ref[pl.ds(start, tk), :]
        s = _dot_nt(q, k)
        kidx = start + lax.broadcasted_iota(jnp.int32, (1, tk), 1)
        ok = (kidx <= qidx) & (kidx >= jnp.maximum(kc * tk, LEAD))
        s = jnp.where(ok, s, NEG)
        m_old = m_ref[...]
        m_new = jnp.maximum(m_old, jnp.max(s, axis=1, keepdims=True))
        pexp = jnp.exp(s - m_new)
        alpha = jnp.exp(m_old - m_new)
        l_ref[...] = alpha * l_ref[...] + jnp.sum(pexp, axis=1, keepdims=True)
        acc_ref[...] = alpha * acc_ref[...] + _dot(pexp.astype(BF16), v)
        m_ref[...] = m_new
        return carry

    lax.fori_loop(0, nkc, body, 0)
    o_ref[...] = _attn_out(acc_ref[...], l_ref[...], wuv_ref, nh, tq).astype(o_ref.dtype)


def _attn_prompt(q4, kcat, v, wuv, bp, lp):
    nq = lp // CHUNK
    nh = MLA_HEADS
    kv_lora = v.shape[1]
    tk = 256
    assert lp % 128 == 0 and lp >= tk
    return pl.pallas_call(
        functools.partial(_attn_prompt_kernel, tk=tk, lp=lp),
        out_shape=jax.ShapeDtypeStruct((bp * lp, nh * V_HEAD), BF16),
        grid=(bp, nq),
        in_specs=[pl.BlockSpec((1, nh, CHUNK, QK_DIM), lambda b, i: (b * nq + i, 0, 0, 0)),
                  pl.BlockSpec((lp, QK_DIM), lambda b, i: (b, 0)),
                  pl.BlockSpec((lp, kv_lora), lambda b, i: (b, 0)),
                  _resident(wuv.shape)],
        out_specs=pl.BlockSpec((CHUNK, nh * V_HEAD), lambda b, i: (b * nq + i, 0)),
        scratch_shapes=[pltpu.VMEM((nh * CHUNK, 1), F32), pltpu.VMEM((nh * CHUNK, 1), F32),
                        pltpu.VMEM((nh * CHUNK, kv_lora), F32)],
        compiler_params=_cparams(("arbitrary", "arbitrary"), 40),
        name="attn_prompt",
    )(q4, kcat, v, wuv)


def _attn_sample_kernel(pt_ref, q_ref, knew_ref, vnew_ref, fold_ref, wuv_ref, ckv_hbm, kr_hbm, o_ref,
                        cbuf, rbuf, sem, m_ref, l_ref, acc_ref, q_scr, *, npg, nchunk, tdec):
    b = pl.program_id(0)
    c = pl.program_id(1)
    nb = pl.num_programs(0)
    step = b * nchunk + c
    slot = step % 2
    nh = q_ref.shape[1]
    rows = nh * tdec
    kv_lora = cbuf.shape[-1]
    page = cbuf.shape[2]

    def copies(bb, cc, sl):
        out = []
        for j in range(npg):
            pg = pt_ref[bb, cc * npg + j]
            out.append(pltpu.make_async_copy(ckv_hbm.at[pg], cbuf.at[sl, j], sem.at[0, sl]))
            out.append(pltpu.make_async_copy(kr_hbm.at[pg], rbuf.at[sl, j], sem.at[1, sl]))
        return out

    @pl.when(step == 0)
    def _():
        for cp in copies(b, c, slot):
            cp.start()

    @pl.when(step + 1 < nb * nchunk)
    def _():
        nxt = step + 1
        for cp in copies(nxt // nchunk, nxt % nchunk, 1 - slot):
            cp.start()

    @pl.when(c == 0)
    def _():
        m_ref[...] = jnp.full(m_ref.shape, NEG, F32)
        l_ref[...] = jnp.zeros(l_ref.shape, F32)
        acc_ref[...] = jnp.zeros(acc_ref.shape, F32)
        q_scr[...] = q_ref[0].reshape(rows, q_ref.shape[3])

    for cp in copies(b, c, slot):
        cp.wait()

    q = q_scr[...]
    q_lat = q[:, 0:kv_lora].astype(BF16)
    q_pe = _dot_exact(q[:, kv_lora:kv_lora + 128], fold_ref[...]).astype(BF16)

    def update(s, v):
        m_old = m_ref[...]
        m_new = jnp.maximum(m_old, jnp.max(s, axis=1, keepdims=True))
        pexp = jnp.exp(s - m_new)
        alpha = jnp.exp(m_old - m_new)
        l_ref[...] = alpha * l_ref[...] + jnp.sum(pexp, axis=1, keepdims=True)
        acc_ref[...] = alpha * acc_ref[...] + _dot(pexp.astype(BF16), v)
        m_ref[...] = m_new

    kc = cbuf[slot].reshape(npg * page, kv_lora).astype(BF16)
    kr = rbuf[slot].reshape(npg * page, rbuf.shape[-1]).astype(BF16)
    update(_dot_nt(q_lat, kc) + _dot_nt(q_pe, kr), kc)

    @pl.when(c == nchunk - 1)
    def _():
        knew = knew_ref[...].astype(BF16)
        s = _dot_nt(q[:, 0:QK_DIM].astype(BF16), knew)
        nk = knew.shape[0]
        kid = lax.broadcasted_iota(jnp.int32, (1, nk), 1)
        t = lax.broadcasted_iota(jnp.int32, (rows, 1), 0) & (tdec - 1)
        ok = (kid >= b * tdec) & (kid <= b * tdec + t)
        update(jnp.where(ok, s, NEG), vnew_ref[...].astype(BF16))
        o_ref[...] = _attn_out(acc_ref[...], l_ref[...], wuv_ref, nh, tdec)


def _attn_sample(page_table, q4, kcat_new, v_new, fold, wuv, cache_ckv, cache_kr, bs, tdec):
    nh = MLA_HEADS
    n_pages = page_table.shape[1]
    page, kv_lora = cache_ckv.shape[1], cache_ckv.shape[2]
    rope = cache_kr.shape[2]
    npg = 16
    while n_pages % npg:
        npg //= 2
    nchunk = n_pages // npg
    rows = nh * tdec
    assert tdec & (tdec - 1) == 0
    grid_spec = pltpu.PrefetchScalarGridSpec(
        num_scalar_prefetch=1,
        grid=(bs, nchunk),
        in_specs=[pl.BlockSpec((1, nh, tdec, QK_DIM), lambda b, c, pt: (b, 0, 0, 0)),
                  pl.BlockSpec(kcat_new.shape, lambda b, c, pt: (0, 0), pipeline_mode=pl.Buffered(1)),
                  pl.BlockSpec(v_new.shape, lambda b, c, pt: (0, 0), pipeline_mode=pl.Buffered(1)),
                  pl.BlockSpec(fold.shape, lambda b, c, pt: (0, 0), pipeline_mode=pl.Buffered(1)),
                  pl.BlockSpec(wuv.shape, lambda b, c, pt: (0, 0, 0), pipeline_mode=pl.Buffered(1)),
                  pl.BlockSpec(memory_space=pl.ANY), pl.BlockSpec(memory_space=pl.ANY)],
        out_specs=pl.BlockSpec((tdec, nh * V_HEAD), lambda b, c, pt: (b, 0)),
        scratch_shapes=[pltpu.VMEM((2, npg, page, kv_lora), F32), pltpu.VMEM((2, npg, page, rope), F32),
                        pltpu.SemaphoreType.DMA((2, 2)),
                        pltpu.VMEM((rows, 1), F32), pltpu.VMEM((rows, 1), F32), pltpu.VMEM((rows, kv_lora), F32),
                        pltpu.VMEM((rows, QK_DIM), F32)],
    )
    return pl.pallas_call(
        functools.partial(_attn_sample_kernel, npg=npg, nchunk=nchunk, tdec=tdec),
        out_shape=jax.ShapeDtypeStruct((bs * tdec, nh * V_HEAD), F32),
        grid_spec=grid_spec,
        compiler_params=_cparams(("arbitrary", "arbitrary"), 40),
        name="attn_sample",
    )(page_table, q4, kcat_new, v_new, fold, wuv, cache_ckv, cache_kr)


def _merge_kernel(yn_ref, o_ref, ga_ref, gb_ref, h_ref, wa_ref, wb_ref, wo_ref, g_ref, out_ref):
    a = _dot(yn_ref[...].astype(BF16), wa_ref[...])
    bb = _dot(o_ref[...].astype(BF16), wb_ref[...])
    merged = jax.nn.sigmoid(ga_ref[0].astype(F32)) * a + jax.nn.sigmoid(gb_ref[0].astype(F32)) * bb
    mix = _dot(merged.astype(BF16), wo_ref[...])
    out_ref[...] = h_ref[...] + _rms(mix, g_ref[...])


def _merge(yn, o, proj3, h, wa, wb, wo, g):
    m, d = h.shape
    tm = _row_tile(m, 512)
    return pl.pallas_call(
        _merge_kernel,
        out_shape=jax.ShapeDtypeStruct((m, d), F32),
        grid=(m // tm,),
        in_specs=[pl.BlockSpec((tm, yn.shape[1]), lambda i: (i, 0)), pl.BlockSpec((tm, o.shape[1]), lambda i: (i, 0)),
                  pl.BlockSpec((1, tm, SLAB), lambda i: (5, i, 0)), pl.BlockSpec((1, tm, SLAB), lambda i: (6, i, 0)),
                  pl.BlockSpec((tm, d), lambda i: (i, 0)),
                  _resident(wa.shape), _resident(wb.shape), _resident(wo.shape), _resident((1, d))],
        out_specs=pl.BlockSpec((tm, d), lambda i: (i, 0)),
        compiler_params=_cparams(("arbitrary",), 40),
        name="gated_merge",
    )(yn, o, proj3, proj3, h, wa, wb, wo, g)


def _prep_weights(w, d_inner, conv_dim, nheads, q_lora, kv_lora):
    d = w["w_in"].shape[0]
    offs = np.cumsum([0, d_inner, conv_dim, nheads, q_lora, kv_lora, QK_ROPE, d, d])
    seg = {k: w["w_in"][:, offs[i]:offs[i + 1]]
           for i, k in enumerate(("z", "xbc", "dt", "q_a", "kv_a", "k_pe", "ga", "gb"))}
    pad = SLAB - q_lora - kv_lora
    wbig = jnp.concatenate([seg["z"], seg["xbc"], seg["ga"], seg["gb"], seg["q_a"], seg["kv_a"],
                            jnp.zeros((d, pad), F32)], axis=1).astype(BF16)
    swap = np.concatenate([np.arange(QK_ROPE // 2, QK_ROPE), np.arange(QK_ROPE // 2)])
    wsmall = jnp.concatenate([seg["dt"], jnp.zeros((d, 128 - nheads), F32), jnp.tile(seg["k_pe"], (1, 4)),
                              jnp.tile(seg["k_pe"][:, swap], (1, 4))], axis=1).astype(BF16)
    nh = MLA_HEADS
    wq3 = w["w_q_b"].reshape(q_lora, nh, QK_NOPE + QK_ROPE)
    wq = jnp.concatenate([wq3[:, :, :QK_NOPE].reshape(q_lora, nh * QK_NOPE),
                          wq3[:, :, QK_NOPE:].reshape(q_lora, nh * QK_ROPE),
                          wq3[:, :, QK_NOPE:][:, :, swap].reshape(q_lora, nh * QK_ROPE)], axis=1).astype(BF16)
    wk = jnp.transpose(w["w_uk"], (1, 2, 0)).reshape(nh // 2, 2, QK_NOPE, kv_lora)
    zk = jnp.zeros((nh // 2, QK_NOPE, kv_lora), F32)
    wuk = jnp.concatenate([jnp.concatenate([wk[:, 0], zk], axis=2),
                           jnp.concatenate([zk, wk[:, 1]], axis=2)], axis=1).astype(BF16)
    wv = jnp.transpose(w["w_uv"], (1, 0, 2)).reshape(nh // 2, 2, kv_lora, V_HEAD)
    zv = jnp.zeros((nh // 2, kv_lora, V_HEAD), F32)
    wuv = jnp.concatenate([jnp.concatenate([wv[:, 0], zv], axis=2),
                           jnp.concatenate([zv, wv[:, 1]], axis=2)], axis=1).astype(BF16)
    expand = np.kron(np.eye(nheads, dtype=np.float32), np.ones((1, SSM_HEAD_DIM), np.float32))
    ssd = dict(conv_w=w["conv_w"], conv_b=w["conv_b"][None], dt_bias=w["dt_bias"][None], a_log=w["a_log"][None],
               dskip_x=jnp.repeat(w["d_skip"], SSM_HEAD_DIM)[None], norm_g=w["ssm_norm_g"][None],
               tril=jnp.asarray(np.tril(np.ones((CHUNK, CHUNK), np.float32))),
               expand=jnp.asarray(expand),
               expand128=jnp.asarray(np.kron(np.eye(nheads, dtype=np.float32), np.ones((1, 128), np.float32))),
               ipad=jnp.asarray(np.eye(nheads, 128, dtype=np.float32)))
    inv = ROPE_THETA ** (-jnp.arange(0, QK_ROPE, 2, dtype=F32) / QK_ROPE)
    inv128 = jnp.tile(jnp.concatenate([inv, inv]), 4)[None]
    sign128 = jnp.asarray(np.tile(np.concatenate([-np.ones(QK_ROPE // 2), np.ones(QK_ROPE // 2)]), 4)[None], F32)
    fold = jnp.asarray(np.tile(np.eye(QK_ROPE, dtype=np.float32), (4, 1)))
    bf = lambda a: a.astype(BF16)
    row = lambda a: a[None]
    return dict(
        wbig=wbig, wsmall=wsmall, wq=wq, wuk=wuk, wuv=wuv, ssd=ssd, inv128=inv128, sign128=sign128, fold=fold,
        ffn1=(row(w["ffn1_pre_g"]), bf(w["ffn1_w_gate"]), bf(w["ffn1_w_up"]), bf(w["ffn1_w_down"]),
              row(w["ffn1_post_g"])),
        ffn2=(row(w["ffn2_pre_g"]), bf(w["ffn2_w_gate"]), bf(w["ffn2_w_up"]), bf(w["ffn2_w_down"]),
              row(w["ffn2_post_g"])),
        mix_pre_g=row(w["mix_pre_g"]), mix_post_g=row(w["mix_post_g"]), q_g=row(w["q_a_norm_g"]),
        kv_g=row(w["kv_a_norm_g"]), wa=bf(w["w_a_out"]), wb=bf(w["w_b_out"]), wo=bf(w["w_o"]))


def kernel(x_prompt, x_sample, cache_kv_latent, cache_k_rope, state_ssm, state_conv, page_table, meta_tokens,
           ffn1_pre_g, ffn1_w_gate, ffn1_w_up, ffn1_w_down, ffn1_post_g,
           mix_pre_g, w_in, conv_w, conv_b, dt_bias, a_log, d_skip, ssm_norm_g,
           q_a_norm_g, w_q_b, kv_a_norm_g, w_uk, w_uv, w_a_out, w_b_out, w_o, mix_post_g,
           ffn2_pre_g, ffn2_w_gate, ffn2_w_up, ffn2_w_down, ffn2_post_g):
    names = ("ffn1_pre_g", "ffn1_w_gate", "ffn1_w_up", "ffn1_w_down", "ffn1_post_g", "mix_pre_g", "w_in", "conv_w",
             "conv_b", "dt_bias", "a_log", "d_skip", "ssm_norm_g", "q_a_norm_g", "w_q_b", "kv_a_norm_g", "w_uk",
             "w_uv", "w_a_out", "w_b_out", "w_o", "mix_post_g", "ffn2_pre_g", "ffn2_w_gate", "ffn2_w_up",
             "ffn2_w_down", "ffn2_post_g")
    stacked = dict(zip(names, (ffn1_pre_g, ffn1_w_gate, ffn1_w_up, ffn1_w_down, ffn1_post_g, mix_pre_g, w_in,
                               conv_w, conv_b, dt_bias, a_log, d_skip, ssm_norm_g, q_a_norm_g, w_q_b, kv_a_norm_g,
                               w_uk, w_uv, w_a_out, w_b_out, w_o, mix_post_g, ffn2_pre_g, ffn2_w_gate, ffn2_w_up,
                               ffn2_w_down, ffn2_post_g)))
    depth = w_in.shape[0]
    bp, seq, d = x_prompt.shape
    bs, tdec, _ = x_sample.shape
    nheads = dt_bias.shape[1]
    d_inner = nheads * SSM_HEAD_DIM
    conv_dim = conv_b.shape[1]
    q_lora, kv_lora = q_a_norm_g.shape[1], kv_a_norm_g.shape[1]
    assert seq % CHUNK == 0 and conv_dim == 3 * SLAB and d_inner == 2 * SLAB and d == SLAB
    assert q_lora + kv_lora <= SLAB and CHUNK % tdec == 0
    n_pages, page = page_table.shape[1], cache_kv_latent.shape[2]
    past_len = n_pages * page
    lp = LEAD + N_META + seq
    nc = lp // CHUNK
    real = LEAD

    hp = jnp.concatenate([jnp.zeros((bp, LEAD, d), F32), jnp.broadcast_to(meta_tokens[None], (bp, N_META, d)),
                          x_prompt], axis=1).reshape(bp * lp, d)
    hs = x_sample.reshape(bs * tdec, d)
    pos_p = jnp.tile(jnp.arange(lp, dtype=jnp.int32) - LEAD, bp).astype(F32)[:, None]
    pos_s = jnp.tile(past_len + jnp.arange(tdec, dtype=jnp.int32), bs).astype(F32)[:, None]

    outs = [[] for _ in range(8)]
    for l in range(depth):
        w = _prep_weights({k: v[l] for k, v in stacked.items()}, d_inner, conv_dim, nheads, q_lora, kv_lora)

        h1 = _ffn(hp, *w["ffn1"])
        proj3, small = _inproj(h1, w["mix_pre_g"], w["wbig"], w["wsmall"], BF16)
        yn, h_fin, conv_new = _ssd_prompt(proj3, small, w["ssd"], bp, nc)
        c_kv, kpe, kcat, q4 = _qkv_prep(proj3, small, pos_p, w["q_g"], w["kv_g"], w["wq"], w["wuk"], w["inv128"],
                                        w["sign128"], CHUNK, BF16)
        o = _attn_prompt(q4, kcat, c_kv.astype(BF16), w["wuv"], bp, lp)
        h2 = _merge(yn, o, proj3, h1, w["wa"], w["wb"], w["wo"], w["mix_post_g"])
        hp = _ffn(h2, *w["ffn2"])
        outs[0].append(c_kv.reshape(bp, lp, kv_lora)[:, real:])
        outs[1].append(kpe.reshape(bp, lp, 128)[:, real:, :QK_ROPE])
        outs[2].append(h_fin.reshape(bp, nheads, SSM_HEAD_DIM, D_STATE))
        outs[3].append(conv_new)

        g1 = _ffn(hs, *w["ffn1"])
        proj3s, smalls = _inproj(g1, w["mix_pre_g"], w["wbig"], w["wsmall"], F32)
        yns, h_new, conv_new_s = _ssd_sample(proj3s, smalls, state_conv[l],
                                             state_ssm[l].reshape(bs, d_inner, D_STATE), w["ssd"], bs, tdec)
        c_kv_s, kpe_s, kcat_s, q4s = _qkv_prep(proj3s, smalls, pos_s, w["q_g"], w["kv_g"], w["wq"], w["wuk"],
                                               w["inv128"], w["sign128"], tdec, F32)
        os_ = _attn_sample(page_table, q4s, kcat_s, c_kv_s, w["fold"], w["wuv"], cache_kv_latent[l],
                           cache_k_rope[l], bs, tdec)
        g2 = _merge(yns, os_, proj3s, g1, w["wa"], w["wb"], w["wo"], w["mix_post_g"])
        hs = _ffn(g2, *w["ffn2"])
        outs[4].append(c_kv_s.reshape(bs, tdec, kv_lora))
        outs[5].append(kpe_s.reshape(bs, tdec, 128)[:, :, :QK_ROPE])
        outs[6].append(h_new.reshape(bs, nheads, SSM_HEAD_DIM, D_STATE))
        outs[7].append(conv_new_s)

    y_prompt = hp.reshape(bp, lp, d)[:, real + N_META:]
    y_sample = hs.reshape(bs, tdec, d)
    return (y_prompt, y_sample) + tuple(jnp.stack(o) for o in outs)
```

```python
import functools
import math

import numpy as np
import jax
import jax.numpy as jnp
from jax import lax
from jax.experimental import pallas as pl
from jax.experimental.pallas import tpu as pltpu

F32 = jnp.float32
BF16 = jnp.bfloat16
HIGHEST = lax.Precision.HIGHEST

EPS = 1e-6
N_META = 16
CHUNK = 128
LEAD = (-N_META) % CHUNK
SSM_HEAD_DIM = 64
SSM_GROUPS = 4
D_STATE = 128
CONV_K = 4
MLA_HEADS = 16
QK_NOPE = 64
QK_ROPE = 32
V_HEAD = 64
ROPE_THETA = 10000.0
ATTN_SCALE = (QK_NOPE + QK_ROPE) ** -0.5
NEG = -1e30
SLAB = 1024
QK_DIM = 384
MIB = 1024 * 1024


def _cparams(sem, vmem_mib):
    return pltpu.CompilerParams(dimension_semantics=sem, vmem_limit_bytes=int(vmem_mib * MIB))


def _resident(shape):
    nd = len(shape)
    return pl.BlockSpec(shape, lambda *_: (0,) * nd, pipeline_mode=pl.Buffered(1))


def _rms(x, g):
    return x * lax.rsqrt(jnp.mean(x * x, axis=-1, keepdims=True) + EPS) * g


def _dot(a, b):
    return jnp.dot(a, b, preferred_element_type=F32)


def _dot_exact(a, b):
    return jnp.dot(a, b, preferred_element_type=F32, precision=HIGHEST)


def _dot_nt(a, b):
    return lax.dot_general(a, b, (((1,), (1,)), ((), ())), preferred_element_type=F32)


def _expand_exact(x, e3):
    hi = x.astype(BF16)
    r1 = x - hi.astype(F32)
    mid = r1.astype(BF16)
    lo = (r1 - mid.astype(F32)).astype(BF16)
    return _dot(jnp.concatenate([hi, mid, lo], axis=1), e3)


def _softplus(x):
    return jnp.maximum(x, 0.0) + jnp.log1p(jnp.exp(-jnp.abs(x)))


def _row_tile(m, cap, mult=16):
    for t in range(min(cap, m), mult - 1, -1):
        if m % t == 0 and t % mult == 0:
            return t
    raise ValueError(f"no row tile for {m}")


def _ffn_kernel(x_ref, pre_ref, wg_ref, wu_ref, wd_ref, post_ref, o_ref, *, chunks):
    x = x_ref[...]
    xn = _rms(x, pre_ref[...]).astype(BF16)
    acc = jnp.zeros(x.shape, F32)
    for lo, sz in chunks:
        g = _dot(xn, wg_ref[:, lo:lo + sz])
        u = _dot(xn, wu_ref[:, lo:lo + sz])
        h = (jax.nn.silu(g) * u).astype(BF16)
        acc = acc + _dot(h, wd_ref[lo:lo + sz, :])
    o_ref[...] = x + 0.5 * _rms(acc, post_ref[...])


def _ffn_chunks(f):
    out, lo = [], 0
    while lo < f:
        sz = min(1024, f - lo)
        out.append((lo, sz))
        lo += sz
    return tuple(out)


def _ffn(x, pre_g, wg, wu, wd, post_g):
    m, d = x.shape
    f = wg.shape[1]
    tm = _row_tile(m, 512)
    return pl.pallas_call(
        functools.partial(_ffn_kernel, chunks=_ffn_chunks(f)),
        out_shape=jax.ShapeDtypeStruct((m, d), F32),
        grid=(m // tm,),
        in_specs=[pl.BlockSpec((tm, d), lambda i: (i, 0)), _resident((1, d)), _resident((d, f)),
                  _resident((d, f)), _resident((f, d)), _resident((1, d))],
        out_specs=pl.BlockSpec((tm, d), lambda i: (i, 0)),
        compiler_params=_cparams(("arbitrary",), 52),
        name="ffn_block",
    )(x, pre_g, wg, wu, wd, post_g)


def _inproj_kernel(x_ref, g_ref, wbig_ref, wsmall_ref, proj_ref, small_ref, *, nslab):
    xn = _rms(x_ref[...], g_ref[...]).astype(BF16)
    for s in range(nslab):
        proj_ref[s] = _dot(xn, wbig_ref[:, s * SLAB:(s + 1) * SLAB]).astype(proj_ref.dtype)
    small_ref[...] = _dot(xn, wsmall_ref[...])


def _inproj(h, g, wbig, wsmall, proj_dtype):
    m, d = h.shape
    nslab = wbig.shape[1] // SLAB
    ns = wsmall.shape[1]
    tm = _row_tile(m, 512)
    return pl.pallas_call(
        functools.partial(_inproj_kernel, nslab=nslab),
        out_shape=(jax.ShapeDtypeStruct((nslab, m, SLAB), proj_dtype), jax.ShapeDtypeStruct((m, ns), F32)),
        grid=(m // tm,),
        in_specs=[pl.BlockSpec((tm, d), lambda i: (i, 0)), _resident((1, d)), _resident(wbig.shape),
                  _resident(wsmall.shape)],
        out_specs=(pl.BlockSpec((nslab, tm, SLAB), lambda i: (0, i, 0)), pl.BlockSpec((tm, ns), lambda i: (i, 0))),
        compiler_params=_cparams(("arbitrary",), 52),
        name="in_proj",
    )(h, g, wbig, wsmall)


def _qkv_kernel(s7_ref, small_ref, pos_ref, qg_ref, kvg_ref, wq_ref, wuk_ref, inv_ref, sign_ref,
                ckv_ref, kpe_ref, kcat_ref, q_ref, *, tqb):
    s7 = s7_ref[0].astype(F32)
    tm = s7.shape[0]
    q_lora = qg_ref.shape[1]
    kv_lora = kvg_ref.shape[1]
    nh = MLA_HEADS
    c_kv = _rms(s7[:, q_lora:q_lora + kv_lora], kvg_ref[...])
    ckv_ref[...] = c_kv
    ang = pos_ref[...] * inv_ref[...]
    cos = jnp.cos(ang)
    sin = jnp.sin(ang) * sign_ref[...]
    small = small_ref[...]
    kpe = small[:, 128:256] * cos + small[:, 256:384] * sin
    kpe_ref[...] = kpe
    kcat_ref[:, 0:kv_lora] = c_kv.astype(kcat_ref.dtype)
    kcat_ref[:, kv_lora:kv_lora + 128] = kpe.astype(kcat_ref.dtype)

    qn = _rms(s7[:, 0:q_lora], qg_ref[...]).astype(BF16)
    nope_w = nh * QK_NOPE
    pe_w = nh * QK_ROPE
    q_nope = _dot(qn, wq_ref[:, 0:nope_w])
    q_pe = _dot(qn, wq_ref[:, nope_w:nope_w + pe_w])
    q_rot = _dot(qn, wq_ref[:, nope_w + pe_w:nope_w + 2 * pe_w])
    cos_w = jnp.concatenate([cos] * (pe_w // 128), axis=1)
    sin_w = jnp.concatenate([sin] * (pe_w // 128), axis=1)
    q_pe = (q_pe * cos_w + q_rot * sin_w) * ATTN_SCALE
    lane = lax.broadcasted_iota(jnp.int32, (1, 128), 1)
    nblk = tm // tqb
    for p in range(nh // 2):
        qn_pair = q_nope[:, p * 128:(p + 1) * 128].astype(BF16)
        q_lat = _dot(qn_pair, wuk_ref[p]) * ATTN_SCALE
        for e in range(2):
            hd = 2 * p + e
            pe_blk = q_pe[:, (hd // 4) * 128:(hd // 4 + 1) * 128]
            sel = (lane >= (hd % 4) * QK_ROPE) & (lane < (hd % 4 + 1) * QK_ROPE)
            pe_blk = jnp.where(sel, pe_blk, 0.0)
            lat = q_lat[:, e * kv_lora:(e + 1) * kv_lora]
            for g in range(nblk):
                q_ref[g, hd, :, 0:kv_lora] = lat[g * tqb:(g + 1) * tqb].astype(q_ref.dtype)
                q_ref[g, hd, :, kv_lora:kv_lora + 128] = pe_blk[g * tqb:(g + 1) * tqb].astype(q_ref.dtype)


def _qkv_prep(proj3, small, pos, qg, kvg, wq, wuk, inv128, sign128, tqb, dtype):
    nslab, m, _ = proj3.shape
    kv_lora = kvg.shape[1]
    tm = _row_tile(m, 512, max(16, tqb))
    nblk = tm // tqb
    return pl.pallas_call(
        functools.partial(_qkv_kernel, tqb=tqb),
        out_shape=(jax.ShapeDtypeStruct((m, kv_lora), F32), jax.ShapeDtypeStruct((m, 128), F32),
                   jax.ShapeDtypeStruct((m, QK_DIM), dtype),
                   jax.ShapeDtypeStruct((m // tqb, MLA_HEADS, tqb, QK_DIM), dtype)),
        grid=(m // tm,),
        in_specs=[pl.BlockSpec((1, tm, SLAB), lambda i: (nslab - 1, i, 0)),
                  pl.BlockSpec((tm, small.shape[1]), lambda i: (i, 0)),
                  pl.BlockSpec((tm, 1), lambda i: (i, 0)),
                  _resident(qg.shape), _resident(kvg.shape), _resident(wq.shape), _resident(wuk.shape),
                  _resident((1, 128)), _resident((1, 128))],
        out_specs=(pl.BlockSpec((tm, kv_lora), lambda i: (i, 0)), pl.BlockSpec((tm, 128), lambda i: (i, 0)),
                   pl.BlockSpec((tm, QK_DIM), lambda i: (i, 0)),
                   pl.BlockSpec((nblk, MLA_HEADS, tqb, QK_DIM), lambda i: (i, 0, 0, 0))),
        compiler_params=_cparams(("arbitrary",), 48),
        name="qkv_prep",
    )(proj3, small, pos, qg, kvg, wq, wuk, inv128, sign128)


def _ssd_chunk(xh_ref, dt_raw, state_ref, apad_ref, y_ref, lo, hi, p):
    d_inner = state_ref.shape[1]
    nheads = d_inner // SSM_HEAD_DIM
    gw = d_inner // SSM_GROUPS
    rows = lax.broadcasted_iota(jnp.int32, (CHUNK, 1), 0)
    valid = (rows >= lo) & (rows < hi)

    conv = p["conv_b"][...]
    for k in range(CONV_K):
        conv = conv + p["conv_w"][k:k + 1, :] * xh_ref[pl.ds(8 - (CONV_K - 1) + k, CHUNK), :]
    xc = jnp.where(valid, jax.nn.silu(conv), 0.0)
    xs = xc[:, 0:d_inner]
    bm = xc[:, d_inner:d_inner + SSM_GROUPS * D_STATE]
    cm = xc[:, d_inner + SSM_GROUPS * D_STATE:]

    dt = jnp.where(valid, _softplus(dt_raw + p["dt_bias"][...]), 0.0)
    da = dt * (-jnp.exp(p["a_log"][...]))
    a_cs = _dot_exact(p["tril"][...], da)
    dt_x = _expand_exact(dt, p["expand"][...])
    acs_x = _expand_exact(a_cs, p["expand"][...])
    alast_x = acs_x[CHUNK - 1:CHUNK, :]
    acs_b = _expand_exact(a_cs, p["expand128"][...])
    apad_ref[...] = a_cs.T
    xdt = xs * dt_x
    xds = (xdt * jnp.exp(alast_x - acs_x)).astype(BF16)
    exp_acs = jnp.exp(acs_x)
    causal = rows >= lax.broadcasted_iota(jnp.int32, (1, CHUNK), 1)
    lane = lax.broadcasted_iota(jnp.int32, (1, 128), 1)
    hpg = nheads // SSM_GROUPS

    for g in range(SSM_GROUPS):
        gs = slice(g * gw, (g + 1) * gw)
        bg = bm[:, g * D_STATE:(g + 1) * D_STATE]
        cg = cm[:, g * D_STATE:(g + 1) * D_STATE].astype(BF16)
        cb = _dot_nt(cg, bg.astype(BF16))
        st = state_ref[:, gs]
        y_off = _dot(cg, st.astype(BF16)) * exp_acs[:, gs]
        state_ref[:, gs] = st * jnp.exp(alast_x[:, gs]) + _dot(bg.T.astype(BF16), xds[:, gs])
        y_ref[:, gs] = y_off + xs[:, gs] * p["dskip_x"][:, gs]
        for pr in range(hpg // 2):
            w2 = []
            for e in range(2):
                hd = g * hpg + 2 * pr + e
                seg = acs_b[:, hd * 128:(hd + 1) * 128] - apad_ref[pl.ds(hd, 1), :]
                decay = jnp.exp(jnp.where(causal, seg, NEG))
                w2.append((cb * decay).astype(BF16))
            ls = slice(g * gw + pr * 128, g * gw + (pr + 1) * 128)
            xp = xdt[:, ls]
            x2 = jnp.concatenate([jnp.where(lane < SSM_HEAD_DIM, xp, 0.0),
                                  jnp.where(lane >= SSM_HEAD_DIM, xp, 0.0)], axis=0).astype(BF16)
            y_ref[:, ls] = y_ref[:, ls] + _dot(jnp.concatenate(w2, axis=1), x2)


def _gate_norm(y, z, g, d_inner):
    yg = y * jax.nn.silu(z)
    gw = d_inner // SSM_GROUPS
    outs = []
    for k in range(SSM_GROUPS):
        v = yg[:, k * gw:(k + 1) * gw]
        outs.append(_rms(v, g[:, k * gw:(k + 1) * gw]))
    return jnp.concatenate(outs, axis=1)


_SSD_PARAM_NAMES = ("conv_w", "conv_b", "dt_bias", "a_log", "dskip_x", "norm_g", "tril", "expand", "expand128")


def _ssd_prompt_kernel(x0_ref, x1_ref, bc_ref, z0_ref, z1_ref, small_ref, *rest, nc):
    np_ = len(_SSD_PARAM_NAMES)
    p = dict(zip(_SSD_PARAM_NAMES, rest[:np_]))
    yn_ref, hfin_ref, convnew_ref, xh_ref, state_ref, apad_ref, y_ref = rest[np_:]
    c = pl.program_id(1)
    d_inner = state_ref.shape[1]

    @pl.when(c == 0)
    def _():
        xh_ref[0:8, :] = jnp.zeros((8, xh_ref.shape[1]), F32)
        state_ref[...] = jnp.zeros(state_ref.shape, F32)

    lo = jnp.where(c == 0, LEAD, 0)
    rows = lax.broadcasted_iota(jnp.int32, (CHUNK, 1), 0)
    valid = rows >= lo
    xh_ref[8:8 + CHUNK, 0:SLAB] = jnp.where(valid, x0_ref[0].astype(F32), 0.0)
    xh_ref[8:8 + CHUNK, SLAB:2 * SLAB] = jnp.where(valid, x1_ref[0].astype(F32), 0.0)
    xh_ref[8:8 + CHUNK, 2 * SLAB:3 * SLAB] = jnp.where(valid, bc_ref[0].astype(F32), 0.0)
    _ssd_chunk(xh_ref, small_ref[:, 0:128], state_ref, apad_ref, y_ref, lo, CHUNK, p)
    z = jnp.concatenate([z0_ref[0], z1_ref[0]], axis=1).astype(F32)
    yn_ref[...] = _gate_norm(y_ref[...], z, p["norm_g"][...], d_inner).astype(yn_ref.dtype)
    xh_ref[0:8, :] = xh_ref[CHUNK:CHUNK + 8, :]

    @pl.when(c == nc - 1)
    def _():
        convnew_ref[0] = xh_ref[pl.ds(8 + CHUNK - (CONV_K - 1), CONV_K - 1), :]
        for j in range(d_inner // 128):
            hfin_ref[0, j * 128:(j + 1) * 128, :] = state_ref[:, j * 128:(j + 1) * 128].T


def _ssd_prompt(proj3, small, params, bp, nc):
    d_inner = params["expand"].shape[1]
    conv_dim = params["conv_b"].shape[1]
    m = proj3.shape[1]
    ns = small.shape[1]
    plist = [params[k] for k in _SSD_PARAM_NAMES]

    def slab(s):
        return pl.BlockSpec((1, CHUNK, SLAB), lambda b, c, s=s: (s, b * nc + c, 0))

    return pl.pallas_call(
        functools.partial(_ssd_prompt_kernel, nc=nc),
        out_shape=(jax.ShapeDtypeStruct((m, d_inner), BF16),
                   jax.ShapeDtypeStruct((bp, d_inner, D_STATE), F32),
                   jax.ShapeDtypeStruct((bp, CONV_K - 1, conv_dim), F32)),
        grid=(bp, nc),
        in_specs=[slab(2), slab(3), slab(4), slab(0), slab(1),
                  pl.BlockSpec((CHUNK, ns), lambda b, c: (b * nc + c, 0))] + [_resident(a.shape) for a in plist],
        out_specs=(pl.BlockSpec((CHUNK, d_inner), lambda b, c: (b * nc + c, 0)),
                   pl.BlockSpec((1, d_inner, D_STATE), lambda b, c: (b, 0, 0)),
                   pl.BlockSpec((1, CONV_K - 1, conv_dim), lambda b, c: (b, 0, 0))),
        scratch_shapes=[pltpu.VMEM((8 + CHUNK, conv_dim), F32), pltpu.VMEM((D_STATE, d_inner), F32),
                        pltpu.VMEM((CHUNK, 128), F32), pltpu.VMEM((CHUNK, d_inner), F32)],
        compiler_params=_cparams(("arbitrary", "arbitrary"), 40),
        name="ssd_prompt",
    )(proj3, proj3, proj3, proj3, proj3, small, *plist)


def _ssd_sample_kernel(x0_ref, x1_ref, bc_ref, z0_ref, z1_ref, small_ref, conv0_ref, h0_ref, *rest, tdec):
    np_ = len(_SSD_PARAM_NAMES)
    p = dict(zip(_SSD_PARAM_NAMES, rest[:np_]))
    yn_ref, hnew_ref, convnew_ref, xh_ref, state_ref, apad_ref, y_ref, dt_ref = rest[np_:]
    d_inner = state_ref.shape[1]
    nk = CONV_K - 1
    xh_ref[...] = jnp.zeros(xh_ref.shape, F32)
    xh_ref[8 - nk:8, :] = conv0_ref[0]
    xh_ref[8:8 + tdec, 0:SLAB] = x0_ref[0]
    xh_ref[8:8 + tdec, SLAB:2 * SLAB] = x1_ref[0]
    xh_ref[8:8 + tdec, 2 * SLAB:3 * SLAB] = bc_ref[0]
    dt_ref[...] = jnp.zeros(dt_ref.shape, F32)
    dt_ref[0:tdec, :] = small_ref[:, 0:128]
    for j in range(d_inner // 128):
        state_ref[:, j * 128:(j + 1) * 128] = h0_ref[0, j * 128:(j + 1) * 128, :].T
    _ssd_chunk(xh_ref, dt_ref[...], state_ref, apad_ref, y_ref, 0, tdec, p)
    z = jnp.concatenate([z0_ref[0], z1_ref[0]], axis=1)
    yn_ref[...] = _gate_norm(y_ref[0:tdec, :], z, p["norm_g"][...], d_inner)
    convnew_ref[0] = xh_ref[pl.ds(8 + tdec - nk, nk), :]
    for j in range(d_inner // 128):
        hnew_ref[0, j * 128:(j + 1) * 128, :] = state_ref[:, j * 128:(j + 1) * 128].T


def _ssd_sample(proj3, small, conv0, h0, params, bs, tdec):
    d_inner = params["expand"].shape[1]
    conv_dim = params["conv_b"].shape[1]
    ns = small.shape[1]
    plist = [params[k] for k in _SSD_PARAM_NAMES]

    def slab(s):
        return pl.BlockSpec((1, tdec, SLAB), lambda b, s=s: (s, b, 0))

    return pl.pallas_call(
        functools.partial(_ssd_sample_kernel, tdec=tdec),
        out_shape=(jax.ShapeDtypeStruct((bs * tdec, d_inner), F32),
                   jax.ShapeDtypeStruct((bs, d_inner, D_STATE), F32),
                   jax.ShapeDtypeStruct((bs, CONV_K - 1, conv_dim), F32)),
        grid=(bs,),
        in_specs=[slab(2), slab(3), slab(4), slab(0), slab(1), pl.BlockSpec((tdec, ns), lambda b: (b, 0)),
                  pl.BlockSpec((1, CONV_K - 1, conv_dim), lambda b: (b, 0, 0)),
                  pl.BlockSpec((1, d_inner, D_STATE), lambda b: (b, 0, 0))] + [_resident(a.shape) for a in plist],
        out_specs=(pl.BlockSpec((tdec, d_inner), lambda b: (b, 0)),
                   pl.BlockSpec((1, d_inner, D_STATE), lambda b: (b, 0, 0)),
                   pl.BlockSpec((1, CONV_K - 1, conv_dim), lambda b: (b, 0, 0))),
        scratch_shapes=[pltpu.VMEM((8 + CHUNK, conv_dim), F32), pltpu.VMEM((D_STATE, d_inner), F32),
                        pltpu.VMEM((CHUNK, 128), F32), pltpu.VMEM((CHUNK, d_inner), F32),
                        pltpu.VMEM((CHUNK, 128), F32)],
        compiler_params=_cparams(("arbitrary",), 40),
        name="ssd_sample",
    )(proj3, proj3, proj3, proj3, proj3, small, conv0, h0, *plist)


def _attn_out(acc, l, wuv_ref, nh, tq):
    o_lat = acc / l
    outs = []
    for p in range(nh // 2):
        pair = jnp.concatenate([o_lat[(2 * p) * tq:(2 * p + 1) * tq], o_lat[(2 * p + 1) * tq:(2 * p + 2) * tq]],
                               axis=1).astype(BF16)
        outs.append(_dot(pair, wuv_ref[p]))
    return jnp.concatenate(outs, axis=1)


def _attn_prompt_kernel(q_ref, k_ref, vt_ref, wuvt_ref, o_ref, m_ref, l_ref, acc_ref, *, tk, lp):
    qi = pl.program_id(1)
    nh, tq = q_ref.shape[1], q_ref.shape[2]
    hpg = q_ref.shape[1] // m_ref.shape[0]
    ngrp = nh // hpg
    gq = hpg * tq
    m_ref[...] = jnp.full(m_ref.shape, NEG, F32)
    l_ref[...] = jnp.zeros(l_ref.shape, F32)
    acc_ref[...] = jnp.zeros(acc_ref.shape, F32)
    qidx = qi * tq + (lax.broadcasted_iota(jnp.int32, (1, gq), 1) & (tq - 1))
    nkc = ((qi + 1) * tq + tk - 1) // tk

    def body(kc, carry):
        start = pl.multiple_of(jnp.minimum(kc * tk, lp - tk), 128)
        k = k_ref[pl.ds(start, tk), :]
        vt = vt_ref[0, :, pl.ds(start, tk)]
        kidx = start + lax.broadcasted_iota(jnp.int32, (tk, 1), 0)
        ok = (kidx <= qidx) & (kidx >= jnp.maximum(kc * tk, LEAD))
        for r in range(ngrp):
            q = q_ref[0, hpg * r:hpg * (r + 1)].reshape(gq, q_ref.shape[3])
            s = jnp.where(ok, _dot_nt(k, q), NEG)
            m_old = m_ref[r:r + 1, :]
            m_new = jnp.maximum(m_old, jnp.max(s, axis=0, keepdims=True))
            pexp = jnp.exp(s - m_new)
            alpha = jnp.exp(m_old - m_new)
            l_ref[r:r + 1, :] = alpha * l_ref[r:r + 1, :] + jnp.sum(pexp, axis=0, keepdims=True)
            acc_ref[r] = alpha * acc_ref[r] + _dot(vt, pexp.astype(BF16))
            m_ref[r:r + 1, :] = m_new
        return carry

    lax.fori_loop(0, nkc, body, 0)
    for r in range(ngrp):
        o_lat_t = (acc_ref[r] / l_ref[r:r + 1, :]).astype(BF16)
        for j in range(hpg // 2):
            h0 = hpg * r + 2 * j
            pair = jnp.concatenate([_dot(wuvt_ref[h0 + e], o_lat_t[:, (2 * j + e) * tq:(2 * j + e + 1) * tq])
                                    for e in range(2)], axis=0)
            o_ref[:, h0 * V_HEAD:(h0 + 2) * V_HEAD] = pair.T.astype(o_ref.dtype)


def _attn_prompt(q4, kcat, vt, wuvt, bp, lp):
    nq = lp // CHUNK
    nh = MLA_HEADS
    kv_lora = vt.shape[1]
    tk = 512
    hpg = 4
    assert lp % 128 == 0 and lp >= tk and 2 * V_HEAD == 128 and nh % hpg == 0
    return pl.pallas_call(
        functools.partial(_attn_prompt_kernel, tk=tk, lp=lp),
        out_shape=jax.ShapeDtypeStruct((bp * lp, nh * V_HEAD), BF16),
        grid=(bp, nq),
        in_specs=[pl.BlockSpec((1, nh, CHUNK, QK_DIM), lambda b, i: (b * nq + i, 0, 0, 0)),
                  pl.BlockSpec((lp, QK_DIM), lambda b, i: (b, 0)),
                  pl.BlockSpec((1, kv_lora, lp), lambda b, i: (b, 0, 0)),
                  _resident(wuvt.shape)],
        out_specs=pl.BlockSpec((CHUNK, nh * V_HEAD), lambda b, i: (b * nq + i, 0)),
        scratch_shapes=[pltpu.VMEM((nh // hpg, hpg * CHUNK), F32), pltpu.VMEM((nh // hpg, hpg * CHUNK), F32),
                        pltpu.VMEM((nh // hpg, kv_lora, hpg * CHUNK), F32)],
        compiler_params=_cparams(("arbitrary", "arbitrary"), 40),
        name="attn_prompt",
    )(q4, kcat, vt, wuvt)


def _attn_sample_kernel(pt_ref, q_ref, knew_ref, vnew_ref, foldt_ref, wuv_ref, ckv_hbm, krt_hbm, o_ref,
                        cbuf, rbuf, sem, m_ref, l_ref, acc_ref, q_scr, qlt_scr, qpt_scr, *, npg, nchunk, tdec):
    b = pl.program_id(0)
    c = pl.program_id(1)
    nb = pl.num_programs(0)
    step = b * nchunk + c
    slot = step % 2
    nh = q_ref.shape[1]
    rows = nh * tdec
    kv_lora = cbuf.shape[-1]
    page = cbuf.shape[2]

    def copies(bb, cc, sl):
        out = []
        for j in range(npg):
            pg = pt_ref[bb, cc * npg + j]
            out.append(pltpu.make_async_copy(ckv_hbm.at[pg], cbuf.at[sl, j], sem.at[0, sl]))
            out.append(pltpu.make_async_copy(krt_hbm.at[pg], rbuf.at[sl, :, pl.ds(j * page, page)], sem.at[1, sl]))
        return out

    @pl.when(step == 0)
    def _():
        for cp in copies(b, c, slot):
            cp.start()

    @pl.when(step + 1 < nb * nchunk)
    def _():
        nxt = step + 1
        for cp in copies(nxt // nchunk, nxt % nchunk, 1 - slot):
            cp.start()

    @pl.when(c == 0)
    def _():
        m_ref[...] = jnp.full(m_ref.shape, NEG, F32)
        l_ref[...] = jnp.zeros(l_ref.shape, F32)
        acc_ref[...] = jnp.zeros(acc_ref.shape, F32)
        q = q_ref[0].reshape(rows, q_ref.shape[3])
        q_scr[...] = q.astype(BF16)
        for j in range(kv_lora // 128):
            qlt_scr[j * 128:(j + 1) * 128, :] = q[:, j * 128:(j + 1) * 128].T.astype(BF16)
        qpt_scr[...] = lax.dot_general(foldt_ref[...], q[:, kv_lora:kv_lora + 128], (((1,), (1,)), ((), ())),
                                       preferred_element_type=F32, precision=HIGHEST).astype(BF16)

    for cp in copies(b, c, slot):
        cp.wait()

    def lanes_to_rows(x):
        return jnp.broadcast_to(x, (rows, rows)).T

    def update(s_t, v):
        m_old = m_ref[...]
        m_new = jnp.maximum(m_old, jnp.max(s_t, axis=0, keepdims=True))
        p_t = jnp.exp(s_t - m_new)
        alpha = jnp.exp(m_old - m_new)
        l_ref[...] = alpha * l_ref[...] + jnp.sum(p_t, axis=0, keepdims=True)
        m_ref[...] = m_new
        nk = s_t.shape[0]
        if nk % 128:
            pmat = p_t.T
        else:
            pmat = jnp.concatenate([p_t[j * 128:(j + 1) * 128, :].T for j in range(nk // 128)], axis=1)
        a_rows = lanes_to_rows(alpha)
        acc_ref[...] = (acc_ref[...] * jnp.concatenate([a_rows] * (kv_lora // 128), axis=1)
                        + _dot(pmat.astype(BF16), v))

    kc = cbuf[slot].reshape(npg * page, kv_lora).astype(BF16)
    kr_t = rbuf[slot]
    kr = jnp.concatenate([kr_t[:, j * page:(j + 1) * page].T for j in range(npg)], axis=0).astype(BF16)
    update(_dot(kc, qlt_scr[...]) + _dot(kr, qpt_scr[...]), kc)

    @pl.when(c == nchunk - 1)
    def _():
        knew = knew_ref[...].astype(BF16)
        s_t = _dot_nt(knew, q_scr[...])
        nk = knew.shape[0]
        kid = lax.broadcasted_iota(jnp.int32, (nk, 1), 0)
        t = lax.broadcasted_iota(jnp.int32, (1, rows), 1) & (tdec - 1)
        ok = (kid >= b * tdec) & (kid <= b * tdec + t)
        update(jnp.where(ok, s_t, NEG), vnew_ref[...].astype(BF16))
        l_rows = jnp.concatenate([lanes_to_rows(l_ref[...])] * (kv_lora // 128), axis=1)
        o_ref[...] = _attn_out(acc_ref[...], l_rows, wuv_ref, nh, tdec)


def _attn_sample(page_table, q4, kcat_new, v_new, fold, wuv, cache_ckv, cache_krt, bs, tdec):
    nh = MLA_HEADS
    n_pages = page_table.shape[1]
    page, kv_lora = cache_ckv.shape[1], cache_ckv.shape[2]
    rope = cache_krt.shape[1]
    npg = 16
    while n_pages % npg:
        npg //= 2
    nchunk = n_pages // npg
    rows = nh * tdec
    assert tdec & (tdec - 1) == 0 and rows == 128 and page == 128
    grid_spec = pltpu.PrefetchScalarGridSpec(
        num_scalar_prefetch=1,
        grid=(bs, nchunk),
        in_specs=[pl.BlockSpec((1, nh, tdec, QK_DIM), lambda b, c, pt: (b, 0, 0, 0)),
                  pl.BlockSpec(kcat_new.shape, lambda b, c, pt: (0, 0), pipeline_mode=pl.Buffered(1)),
                  pl.BlockSpec(v_new.shape, lambda b, c, pt: (0, 0), pipeline_mode=pl.Buffered(1)),
                  pl.BlockSpec(fold.shape, lambda b, c, pt: (0, 0), pipeline_mode=pl.Buffered(1)),
                  pl.BlockSpec(wuv.shape, lambda b, c, pt: (0, 0, 0), pipeline_mode=pl.Buffered(1)),
                  pl.BlockSpec(memory_space=pl.ANY), pl.BlockSpec(memory_space=pl.ANY)],
        out_specs=pl.BlockSpec((tdec, nh * V_HEAD), lambda b, c, pt: (b, 0)),
        scratch_shapes=[pltpu.VMEM((2, npg, page, kv_lora), F32), pltpu.VMEM((2, rope, npg * page), F32),
                        pltpu.SemaphoreType.DMA((2, 2)),
                        pltpu.VMEM((1, rows), F32), pltpu.VMEM((1, rows), F32), pltpu.VMEM((rows, kv_lora), F32),
                        pltpu.VMEM((rows, QK_DIM), BF16), pltpu.VMEM((kv_lora, rows), BF16),
                        pltpu.VMEM((rope, rows), BF16)],
    )
    return pl.pallas_call(
        functools.partial(_attn_sample_kernel, npg=npg, nchunk=nchunk, tdec=tdec),
        out_shape=jax.ShapeDtypeStruct((bs * tdec, nh * V_HEAD), F32),
        grid_spec=grid_spec,
        compiler_params=_cparams(("arbitrary", "arbitrary"), 40),
        name="attn_sample",
    )(page_table, q4, kcat_new, v_new, fold, wuv, cache_ckv, cache_krt)


def _merge_kernel(yn_ref, o_ref, ga_ref, gb_ref, h_ref, wa_ref, wb_ref, wo_ref, g_ref, out_ref):
    a = _dot(yn_ref[...].astype(BF16), wa_ref[...])
    bb = _dot(o_ref[...].astype(BF16), wb_ref[...])
    merged = jax.nn.sigmoid(ga_ref[0].astype(F32)) * a + jax.nn.sigmoid(gb_ref[0].astype(F32)) * bb
    mix = _dot(merged.astype(BF16), wo_ref[...])
    out_ref[...] = h_ref[...] + _rms(mix, g_ref[...])


def _merge(yn, o, proj3, h, wa, wb, wo, g):
    m, d = h.shape
    tm = _row_tile(m, 512)
    return pl.pallas_call(
        _merge_kernel,
        out_shape=jax.ShapeDtypeStruct((m, d), F32),
        grid=(m // tm,),
        in_specs=[pl.BlockSpec((tm, yn.shape[1]), lambda i: (i, 0)), pl.BlockSpec((tm, o.shape[1]), lambda i: (i, 0)),
                  pl.BlockSpec((1, tm, SLAB), lambda i: (5, i, 0)), pl.BlockSpec((1, tm, SLAB), lambda i: (6, i, 0)),
                  pl.BlockSpec((tm, d), lambda i: (i, 0)),
                  _resident(wa.shape), _resident(wb.shape), _resident(wo.shape), _resident((1, d))],
        out_specs=pl.BlockSpec((tm, d), lambda i: (i, 0)),
        compiler_params=_cparams(("arbitrary",), 40),
        name="gated_merge",
    )(yn, o, proj3, proj3, h, wa, wb, wo, g)


def _prep_weights(w, d_inner, conv_dim, nheads, q_lora, kv_lora):
    d = w["w_in"].shape[0]
    offs = np.cumsum([0, d_inner, conv_dim, nheads, q_lora, kv_lora, QK_ROPE, d, d])
    seg = {k: w["w_in"][:, offs[i]:offs[i + 1]]
           for i, k in enumerate(("z", "xbc", "dt", "q_a", "kv_a", "k_pe", "ga", "gb"))}
    pad = SLAB - q_lora - kv_lora
    wbig = jnp.concatenate([seg["z"], seg["xbc"], seg["ga"], seg["gb"], seg["q_a"], seg["kv_a"],
                            jnp.zeros((d, pad), F32)], axis=1).astype(BF16)
    swap = np.concatenate([np.arange(QK_ROPE // 2, QK_ROPE), np.arange(QK_ROPE // 2)])
    wsmall = jnp.concatenate([seg["dt"], jnp.zeros((d, 128 - nheads), F32), jnp.tile(seg["k_pe"], (1, 4)),
                              jnp.tile(seg["k_pe"][:, swap], (1, 4))], axis=1).astype(BF16)
    nh = MLA_HEADS
    wq3 = w["w_q_b"].reshape(q_lora, nh, QK_NOPE + QK_ROPE)
    wq = jnp.concatenate([wq3[:, :, :QK_NOPE].reshape(q_lora, nh * QK_NOPE),
                          wq3[:, :, QK_NOPE:].reshape(q_lora, nh * QK_ROPE),
                          wq3[:, :, QK_NOPE:][:, :, swap].reshape(q_lora, nh * QK_ROPE)], axis=1).astype(BF16)
    wk = jnp.transpose(w["w_uk"], (1, 2, 0)).reshape(nh // 2, 2, QK_NOPE, kv_lora)
    zk = jnp.zeros((nh // 2, QK_NOPE, kv_lora), F32)
    wuk = jnp.concatenate([jnp.concatenate([wk[:, 0], zk], axis=2),
                           jnp.concatenate([zk, wk[:, 1]], axis=2)], axis=1).astype(BF16)
    wv = jnp.transpose(w["w_uv"], (1, 0, 2)).reshape(nh // 2, 2, kv_lora, V_HEAD)
    zv = jnp.zeros((nh // 2, kv_lora, V_HEAD), F32)
    wuvt = jnp.transpose(w["w_uv"], (1, 2, 0)).astype(BF16)
    wuv = jnp.concatenate([jnp.concatenate([wv[:, 0], zv], axis=2),
                           jnp.concatenate([zv, wv[:, 1]], axis=2)], axis=1).astype(BF16)
    def expand3(width):
        e = np.kron(np.eye(128, nheads, dtype=np.float32), np.ones((1, width), np.float32))
        return jnp.asarray(np.tile(e, (3, 1)), BF16)

    lane_pad = lambda a: jnp.pad(a, (0, 128 - nheads))[None]
    ssd = dict(conv_w=w["conv_w"], conv_b=w["conv_b"][None], dt_bias=lane_pad(w["dt_bias"]),
               a_log=lane_pad(w["a_log"]),
               dskip_x=jnp.repeat(w["d_skip"], SSM_HEAD_DIM)[None], norm_g=w["ssm_norm_g"][None],
               tril=jnp.asarray(np.tril(np.ones((CHUNK, CHUNK), np.float32))),
               expand=expand3(SSM_HEAD_DIM), expand128=expand3(128))
    inv = ROPE_THETA ** (-jnp.arange(0, QK_ROPE, 2, dtype=F32) / QK_ROPE)
    inv128 = jnp.tile(jnp.concatenate([inv, inv]), 4)[None]
    sign128 = jnp.asarray(np.tile(np.concatenate([-np.ones(QK_ROPE // 2), np.ones(QK_ROPE // 2)]), 4)[None], F32)
    foldt = jnp.asarray(np.tile(np.eye(QK_ROPE, dtype=np.float32), (1, 4)))
    bf = lambda a: a.astype(BF16)
    row = lambda a: a[None]
    return dict(
        wbig=wbig, wsmall=wsmall, wq=wq, wuk=wuk, wuv=wuv, wuvt=wuvt, ssd=ssd, inv128=inv128, sign128=sign128, foldt=foldt,
        ffn1=(row(w["ffn1_pre_g"]), bf(w["ffn1_w_gate"]), bf(w["ffn1_w_up"]), bf(w["ffn1_w_down"]),
              row(w["ffn1_post_g"])),
        ffn2=(row(w["ffn2_pre_g"]), bf(w["ffn2_w_gate"]), bf(w["ffn2_w_up"]), bf(w["ffn2_w_down"]),
              row(w["ffn2_post_g"])),
        mix_pre_g=row(w["mix_pre_g"]), mix_post_g=row(w["mix_post_g"]), q_g=row(w["q_a_norm_g"]),
        kv_g=row(w["kv_a_norm_g"]), wa=bf(w["w_a_out"]), wb=bf(w["w_b_out"]), wo=bf(w["w_o"]))


def kernel(x_prompt, x_sample, cache_kv_latent, cache_k_rope, state_ssm, state_conv, page_table, meta_tokens,
           ffn1_pre_g, ffn1_w_gate, ffn1_w_up, ffn1_w_down, ffn1_post_g,
           mix_pre_g, w_in, conv_w, conv_b, dt_bias, a_log, d_skip, ssm_norm_g,
           q_a_norm_g, w_q_b, kv_a_norm_g, w_uk, w_uv, w_a_out, w_b_out, w_o, mix_post_g,
           ffn2_pre_g, ffn2_w_gate, ffn2_w_up, ffn2_w_down, ffn2_post_g):
    names = ("ffn1_pre_g", "ffn1_w_gate", "ffn1_w_up", "ffn1_w_down", "ffn1_post_g", "mix_pre_g", "w_in", "conv_w",
             "conv_b", "dt_bias", "a_log", "d_skip", "ssm_norm_g", "q_a_norm_g", "w_q_b", "kv_a_norm_g", "w_uk",
             "w_uv", "w_a_out", "w_b_out", "w_o", "mix_post_g", "ffn2_pre_g", "ffn2_w_gate", "ffn2_w_up",
             "ffn2_w_down", "ffn2_post_g")
    stacked = dict(zip(names, (ffn1_pre_g, ffn1_w_gate, ffn1_w_up, ffn1_w_down, ffn1_post_g, mix_pre_g, w_in,
                               conv_w, conv_b, dt_bias, a_log, d_skip, ssm_norm_g, q_a_norm_g, w_q_b, kv_a_norm_g,
                               w_uk, w_uv, w_a_out, w_b_out, w_o, mix_post_g, ffn2_pre_g, ffn2_w_gate, ffn2_w_up,
                               ffn2_w_down, ffn2_post_g)))
    depth = w_in.shape[0]
    bp, seq, d = x_prompt.shape
    bs, tdec, _ = x_sample.shape
    nheads = dt_bias.shape[1]
    d_inner = nheads * SSM_HEAD_DIM
    conv_dim = conv_b.shape[1]
    q_lora, kv_lora = q_a_norm_g.shape[1], kv_a_norm_g.shape[1]
    assert seq % CHUNK == 0 and conv_dim == 3 * SLAB and d_inner == 2 * SLAB and d == SLAB
    assert q_lora + kv_lora <= SLAB and CHUNK % tdec == 0
    n_pages, page = page_table.shape[1], cache_kv_latent.shape[2]
    past_len = n_pages * page
    lp = LEAD + N_META + seq
    nc = lp // CHUNK
    real = LEAD

    hp = jnp.concatenate([jnp.zeros((bp, LEAD, d), F32), jnp.broadcast_to(meta_tokens[None], (bp, N_META, d)),
                          x_prompt], axis=1).reshape(bp * lp, d)
    hs = x_sample.reshape(bs * tdec, d)
    pos_p = jnp.tile(jnp.arange(lp, dtype=jnp.int32) - LEAD, bp).astype(F32)[:, None]
    pos_s = jnp.tile(past_len + jnp.arange(tdec, dtype=jnp.int32), bs).astype(F32)[:, None]

    outs = [[] for _ in range(8)]
    for l in range(depth):
        w = _prep_weights({k: v[l] for k, v in stacked.items()}, d_inner, conv_dim, nheads, q_lora, kv_lora)

        h1 = _ffn(hp, *w["ffn1"])
        proj3, small = _inproj(h1, w["mix_pre_g"], w["wbig"], w["wsmall"], BF16)
        yn, h_fin, conv_new = _ssd_prompt(proj3, small, w["ssd"], bp, nc)
        c_kv, kpe, kcat, q4 = _qkv_prep(proj3, small, pos_p, w["q_g"], w["kv_g"], w["wq"], w["wuk"], w["inv128"],
                                        w["sign128"], CHUNK, BF16)
        vt = jnp.transpose(c_kv.astype(BF16).reshape(bp, lp, kv_lora), (0, 2, 1))
        o = _attn_prompt(q4, kcat, vt, w["wuvt"], bp, lp)
        h2 = _merge(yn, o, proj3, h1, w["wa"], w["wb"], w["wo"], w["mix_post_g"])
        hp = _ffn(h2, *w["ffn2"])
        outs[0].append(c_kv.reshape(bp, lp, kv_lora)[:, real:])
        outs[1].append(kpe.reshape(bp, lp, 128)[:, real:, :QK_ROPE])
        outs[2].append(h_fin.reshape(bp, nheads, SSM_HEAD_DIM, D_STATE))
        outs[3].append(conv_new)

        g1 = _ffn(hs, *w["ffn1"])
        proj3s, smalls = _inproj(g1, w["mix_pre_g"], w["wbig"], w["wsmall"], F32)
        yns, h_new, conv_new_s = _ssd_sample(proj3s, smalls, state_conv[l],
                                             state_ssm[l].reshape(bs, d_inner, D_STATE), w["ssd"], bs, tdec)
        c_kv_s, kpe_s, kcat_s, q4s = _qkv_prep(proj3s, smalls, pos_s, w["q_g"], w["kv_g"], w["wq"], w["wuk"],
                                               w["inv128"], w["sign128"], tdec, F32)
        os_ = _attn_sample(page_table, q4s, kcat_s, c_kv_s, w["foldt"], w["wuv"], cache_kv_latent[l],
                           jnp.transpose(cache_k_rope[l], (0, 2, 1)), bs, tdec)
        g2 = _merge(yns, os_, proj3s, g1, w["wa"], w["wb"], w["wo"], w["mix_post_g"])
        hs = _ffn(g2, *w["ffn2"])
        outs[4].append(c_kv_s.reshape(bs, tdec, kv_lora))
        outs[5].append(kpe_s.reshape(bs, tdec, 128)[:, :, :QK_ROPE])
        outs[6].append(h_new.reshape(bs, nheads, SSM_HEAD_DIM, D_STATE))
        outs[7].append(conv_new_s)

    y_prompt = hp.reshape(bp, lp, d)[:, real + N_META:]
    y_sample = hs.reshape(bs, tdec, d)
    return (y_prompt, y_sample) + tuple(jnp.stack(o) for o in outs)
```

```python
import functools
import math

import numpy as np
import jax
import jax.numpy as jnp
from jax import lax
from jax.experimental import pallas as pl
from jax.experimental.pallas import tpu as pltpu

F32 = jnp.float32
BF16 = jnp.bfloat16
HIGHEST = lax.Precision.HIGHEST

EPS = 1e-6
N_META = 16
CHUNK = 128
LEAD = (-N_META) % CHUNK
SSM_HEAD_DIM = 64
SSM_GROUPS = 4
D_STATE = 128
CONV_K = 4
MLA_HEADS = 16
QK_NOPE = 64
QK_ROPE = 32
V_HEAD = 64
ROPE_THETA = 10000.0
ATTN_SCALE = (QK_NOPE + QK_ROPE) ** -0.5
Q_SCALE = ATTN_SCALE * math.log2(math.e)
NEG = -1e30
SLAB = 1024
QK_DIM = 384
MIB = 1024 * 1024


def _cparams(sem, vmem_mib):
    return pltpu.CompilerParams(dimension_semantics=sem, vmem_limit_bytes=int(vmem_mib * MIB))


def _resident(shape):
    nd = len(shape)
    return pl.BlockSpec(shape, lambda *_: (0,) * nd, pipeline_mode=pl.Buffered(1))


def _rms(x, g):
    return x * lax.rsqrt(jnp.mean(x * x, axis=-1, keepdims=True) + EPS) * g


def _dot(a, b):
    return jnp.dot(a, b, preferred_element_type=F32)


def _dot_exact(a, b):
    return jnp.dot(a, b, preferred_element_type=F32, precision=HIGHEST)


def _dot_nt(a, b):
    return lax.dot_general(a, b, (((1,), (1,)), ((), ())), preferred_element_type=F32)


def _expand_exact(x, e3):
    hi = x.astype(BF16)
    r1 = x - hi.astype(F32)
    mid = r1.astype(BF16)
    lo = (r1 - mid.astype(F32)).astype(BF16)
    return _dot(jnp.concatenate([hi, mid, lo], axis=1), e3)


def _softplus(x):
    return jnp.maximum(x, 0.0) + jnp.log1p(jnp.exp(-jnp.abs(x)))


def _row_tile(m, cap, mult=16):
    for t in range(min(cap, m), mult - 1, -1):
        if m % t == 0 and t % mult == 0:
            return t
    raise ValueError(f"no row tile for {m}")


def _ffn_kernel(x_ref, pre_ref, wg_ref, wu_ref, wd_ref, post_ref, o_ref, *, chunks):
    x = x_ref[...]
    xn = _rms(x, pre_ref[...]).astype(BF16)
    acc = jnp.zeros(x.shape, F32)
    for lo, sz in chunks:
        g = _dot(xn, wg_ref[:, lo:lo + sz])
        u = _dot(xn, wu_ref[:, lo:lo + sz])
        h = (jax.nn.silu(g) * u).astype(BF16)
        acc = acc + _dot(h, wd_ref[lo:lo + sz, :])
    o_ref[...] = x + 0.5 * _rms(acc, post_ref[...])


def _ffn_chunks(f):
    out, lo = [], 0
    while lo < f:
        sz = min(1024, f - lo)
        out.append((lo, sz))
        lo += sz
    return tuple(out)


def _ffn(x, pre_g, wg, wu, wd, post_g):
    m, d = x.shape
    f = wg.shape[1]
    tm = _row_tile(m, 512)
    return pl.pallas_call(
        functools.partial(_ffn_kernel, chunks=_ffn_chunks(f)),
        out_shape=jax.ShapeDtypeStruct((m, d), F32),
        grid=(m // tm,),
        in_specs=[pl.BlockSpec((tm, d), lambda i: (i, 0)), _resident((1, d)), _resident((d, f)),
                  _resident((d, f)), _resident((f, d)), _resident((1, d))],
        out_specs=pl.BlockSpec((tm, d), lambda i: (i, 0)),
        compiler_params=_cparams(("arbitrary",), 52),
        name="ffn_block",
    )(x, pre_g, wg, wu, wd, post_g)


def _inproj_kernel(x_ref, g_ref, wbig_ref, wsmall_ref, proj_ref, small_ref, *, nslab):
    xn = _rms(x_ref[...], g_ref[...]).astype(BF16)
    for s in range(nslab):
        proj_ref[s] = _dot(xn, wbig_ref[:, s * SLAB:(s + 1) * SLAB]).astype(proj_ref.dtype)
    small_ref[...] = _dot(xn, wsmall_ref[...])


def _inproj(h, g, wbig, wsmall, proj_dtype):
    m, d = h.shape
    nslab = wbig.shape[1] // SLAB
    ns = wsmall.shape[1]
    tm = _row_tile(m, 512)
    return pl.pallas_call(
        functools.partial(_inproj_kernel, nslab=nslab),
        out_shape=(jax.ShapeDtypeStruct((nslab, m, SLAB), proj_dtype), jax.ShapeDtypeStruct((m, ns), F32)),
        grid=(m // tm,),
        in_specs=[pl.BlockSpec((tm, d), lambda i: (i, 0)), _resident((1, d)), _resident(wbig.shape),
                  _resident(wsmall.shape)],
        out_specs=(pl.BlockSpec((nslab, tm, SLAB), lambda i: (0, i, 0)), pl.BlockSpec((tm, ns), lambda i: (i, 0))),
        compiler_params=_cparams(("arbitrary",), 52),
        name="in_proj",
    )(h, g, wbig, wsmall)


def _qkv_kernel(s7_ref, small_ref, pos_ref, qg_ref, kvg_ref, wq_ref, wuk_ref, inv_ref, sign_ref,
                ckv_ref, kpe_ref, kcat_ref, q_ref, *, tqb):
    s7 = s7_ref[0].astype(F32)
    tm = s7.shape[0]
    q_lora = qg_ref.shape[1]
    kv_lora = kvg_ref.shape[1]
    nh = MLA_HEADS
    c_kv = _rms(s7[:, q_lora:q_lora + kv_lora], kvg_ref[...])
    ckv_ref[...] = c_kv
    ang = pos_ref[...] * inv_ref[...]
    cos = jnp.cos(ang)
    sin = jnp.sin(ang) * sign_ref[...]
    small = small_ref[...]
    kpe = small[:, 128:256] * cos + small[:, 256:384] * sin
    kpe_ref[...] = kpe
    kcat_ref[:, 0:kv_lora] = c_kv.astype(kcat_ref.dtype)
    kcat_ref[:, kv_lora:kv_lora + 128] = kpe.astype(kcat_ref.dtype)

    qn = _rms(s7[:, 0:q_lora], qg_ref[...]).astype(BF16)
    nope_w = nh * QK_NOPE
    pe_w = nh * QK_ROPE
    q_nope = _dot(qn, wq_ref[:, 0:nope_w])
    q_pe = _dot(qn, wq_ref[:, nope_w:nope_w + pe_w])
    q_rot = _dot(qn, wq_ref[:, nope_w + pe_w:nope_w + 2 * pe_w])
    cos_w = jnp.concatenate([cos] * (pe_w // 128), axis=1)
    sin_w = jnp.concatenate([sin] * (pe_w // 128), axis=1)
    q_pe = (q_pe * cos_w + q_rot * sin_w) * Q_SCALE
    lane = lax.broadcasted_iota(jnp.int32, (1, 128), 1)
    nblk = tm // tqb
    for p in range(nh // 2):
        qn_pair = q_nope[:, p * 128:(p + 1) * 128].astype(BF16)
        q_lat = _dot(qn_pair, wuk_ref[p]) * Q_SCALE
        for e in range(2):
            hd = 2 * p + e
            pe_blk = q_pe[:, (hd // 4) * 128:(hd // 4 + 1) * 128]
            sel = (lane >= (hd % 4) * QK_ROPE) & (lane < (hd % 4 + 1) * QK_ROPE)
            pe_blk = jnp.where(sel, pe_blk, 0.0)
            lat = q_lat[:, e * kv_lora:(e + 1) * kv_lora]
            for g in range(nblk):
                q_ref[g, hd, :, 0:kv_lora] = lat[g * tqb:(g + 1) * tqb].astype(q_ref.dtype)
                q_ref[g, hd, :, kv_lora:kv_lora + 128] = pe_blk[g * tqb:(g + 1) * tqb].astype(q_ref.dtype)


def _qkv_prep(proj3, small, pos, qg, kvg, wq, wuk, inv128, sign128, tqb, dtype):
    nslab, m, _ = proj3.shape
    kv_lora = kvg.shape[1]
    tm = _row_tile(m, 512, max(16, tqb))
    nblk = tm // tqb
    return pl.pallas_call(
        functools.partial(_qkv_kernel, tqb=tqb),
        out_shape=(jax.ShapeDtypeStruct((m, kv_lora), F32), jax.ShapeDtypeStruct((m, 128), F32),
                   jax.ShapeDtypeStruct((m, QK_DIM), dtype),
                   jax.ShapeDtypeStruct((m // tqb, MLA_HEADS, tqb, QK_DIM), dtype)),
        grid=(m // tm,),
        in_specs=[pl.BlockSpec((1, tm, SLAB), lambda i: (nslab - 1, i, 0)),
                  pl.BlockSpec((tm, small.shape[1]), lambda i: (i, 0)),
                  pl.BlockSpec((tm, 1), lambda i: (i, 0)),
                  _resident(qg.shape), _resident(kvg.shape), _resident(wq.shape), _resident(wuk.shape),
                  _resident((1, 128)), _resident((1, 128))],
        out_specs=(pl.BlockSpec((tm, kv_lora), lambda i: (i, 0)), pl.BlockSpec((tm, 128), lambda i: (i, 0)),
                   pl.BlockSpec((tm, QK_DIM), lambda i: (i, 0)),
                   pl.BlockSpec((nblk, MLA_HEADS, tqb, QK_DIM), lambda i: (i, 0, 0, 0))),
        compiler_params=_cparams(("arbitrary",), 48),
        name="qkv_prep",
    )(proj3, small, pos, qg, kvg, wq, wuk, inv128, sign128)


def _ssd_chunk(xh_ref, dt_raw, state_ref, apad_ref, y_ref, lo, hi, p, bf16_input):
    d_inner = state_ref.shape[1]
    nheads = d_inner // SSM_HEAD_DIM
    gw = d_inner // SSM_GROUPS
    rows = lax.broadcasted_iota(jnp.int32, (CHUNK, 1), 0)
    valid = (rows >= lo) & (rows < hi)
    ntap = CONV_K - 1

    def conv_rows(nrows):
        out = p["conv_b"][...]
        for k in range(CONV_K):
            out = out + p["conv_w"][k:k + 1, :] * xh_ref[pl.ds(8 - ntap + k, nrows), :]
        return out

    if bf16_input:
        cur = xh_ref[8:8 + CHUNK, :]
        shifted = _dot(p["shift"][...], cur.astype(BF16))
        conv = p["conv_b"][...]
        for k in range(ntap):
            conv = conv + p["conv_w"][k:k + 1, :] * shifted[k * CHUNK:(k + 1) * CHUNK]
        conv = conv + p["conv_w"][ntap:CONV_K, :] * cur
        conv = jnp.concatenate([conv_rows(8), conv[8:]], axis=0)
    else:
        conv = conv_rows(CHUNK)
    xc = jnp.where(valid, jax.nn.silu(conv), 0.0)
    xs = xc[:, 0:d_inner]
    bm = xc[:, d_inner:d_inner + SSM_GROUPS * D_STATE]
    cm = xc[:, d_inner + SSM_GROUPS * D_STATE:]

    dt = jnp.where(valid, _softplus(dt_raw + p["dt_bias"][...]), 0.0)
    da = dt * (-jnp.exp(p["a_log"][...]))
    a_cs = _dot_exact(p["tril"][...], da) * math.log2(math.e)
    dt_x = _expand_exact(dt, p["expand"][...])
    acs_x = _expand_exact(a_cs, p["expand"][...])
    alast_x = acs_x[CHUNK - 1:CHUNK, :]
    acs_b = _expand_exact(a_cs, p["expand128"][...])
    apad_ref[...] = a_cs.T
    xdt = xs * dt_x
    xds = (xdt * jnp.exp2(alast_x - acs_x)).astype(BF16)
    exp_acs = jnp.exp2(acs_x)
    causal = rows >= lax.broadcasted_iota(jnp.int32, (1, CHUNK), 1)
    lane = lax.broadcasted_iota(jnp.int32, (1, 128), 1)
    hpg = nheads // SSM_GROUPS

    for g in range(SSM_GROUPS):
        gs = slice(g * gw, (g + 1) * gw)
        bg = bm[:, g * D_STATE:(g + 1) * D_STATE]
        cg = cm[:, g * D_STATE:(g + 1) * D_STATE].astype(BF16)
        cb = _dot_nt(cg, bg.astype(BF16))
        st = state_ref[:, gs]
        y_off = _dot(cg, st.astype(BF16)) * exp_acs[:, gs]
        state_ref[:, gs] = st * jnp.exp2(alast_x[:, gs]) + _dot(bg.T.astype(BF16), xds[:, gs])
        y_ref[:, gs] = y_off + xs[:, gs] * p["dskip_x"][:, gs]
        for pr in range(hpg // 2):
            w2 = []
            for e in range(2):
                hd = g * hpg + 2 * pr + e
                seg = acs_b[:, hd * 128:(hd + 1) * 128] - apad_ref[pl.ds(hd, 1), :]
                decay = jnp.exp2(jnp.where(causal, seg, NEG))
                w2.append((cb * decay).astype(BF16))
            ls = slice(g * gw + pr * 128, g * gw + (pr + 1) * 128)
            xp = xdt[:, ls]
            x2 = jnp.concatenate([jnp.where(lane < SSM_HEAD_DIM, xp, 0.0),
                                  jnp.where(lane >= SSM_HEAD_DIM, xp, 0.0)], axis=0).astype(BF16)
            y_ref[:, ls] = y_ref[:, ls] + _dot(jnp.concatenate(w2, axis=1), x2)


def _gate_norm(y, z, g, d_inner):
    yg = y * jax.nn.silu(z)
    gw = d_inner // SSM_GROUPS
    outs = []
    for k in range(SSM_GROUPS):
        v = yg[:, k * gw:(k + 1) * gw]
        outs.append(_rms(v, g[:, k * gw:(k + 1) * gw]))
    return jnp.concatenate(outs, axis=1)


_SSD_PARAM_NAMES = ("conv_w", "conv_b", "dt_bias", "a_log", "dskip_x", "norm_g", "tril", "expand", "expand128",
                    "shift")


def _ssd_prompt_kernel(x0_ref, x1_ref, bc_ref, z0_ref, z1_ref, small_ref, *rest, nc):
    np_ = len(_SSD_PARAM_NAMES)
    p = dict(zip(_SSD_PARAM_NAMES, rest[:np_]))
    yn_ref, hfin_ref, convnew_ref, xh_ref, state_ref, apad_ref, y_ref = rest[np_:]
    c = pl.program_id(1)
    d_inner = state_ref.shape[1]

    @pl.when(c == 0)
    def _():
        xh_ref[0:8, :] = jnp.zeros((8, xh_ref.shape[1]), F32)
        state_ref[...] = jnp.zeros(state_ref.shape, F32)

    lo = jnp.where(c == 0, LEAD, 0)
    rows = lax.broadcasted_iota(jnp.int32, (CHUNK, 1), 0)
    valid = rows >= lo
    xh_ref[8:8 + CHUNK, 0:SLAB] = jnp.where(valid, x0_ref[0].astype(F32), 0.0)
    xh_ref[8:8 + CHUNK, SLAB:2 * SLAB] = jnp.where(valid, x1_ref[0].astype(F32), 0.0)
    xh_ref[8:8 + CHUNK, 2 * SLAB:3 * SLAB] = jnp.where(valid, bc_ref[0].astype(F32), 0.0)
    _ssd_chunk(xh_ref, small_ref[:, 0:128], state_ref, apad_ref, y_ref, lo, CHUNK, p, x0_ref.dtype == BF16)
    z = jnp.concatenate([z0_ref[0], z1_ref[0]], axis=1).astype(F32)
    yn_ref[...] = _gate_norm(y_ref[...], z, p["norm_g"][...], d_inner).astype(yn_ref.dtype)
    xh_ref[0:8, :] = xh_ref[CHUNK:CHUNK + 8, :]

    @pl.when(c == nc - 1)
    def _():
        convnew_ref[0] = xh_ref[pl.ds(8 + CHUNK - (CONV_K - 1), CONV_K - 1), :]
        for j in range(d_inner // 128):
            hfin_ref[0, j * 128:(j + 1) * 128, :] = state_ref[:, j * 128:(j + 1) * 128].T


def _ssd_prompt(proj3, small, params, bp, nc):
    d_inner = params["expand"].shape[1]
    conv_dim = params["conv_b"].shape[1]
    m = proj3.shape[1]
    ns = small.shape[1]
    plist = [params[k] for k in _SSD_PARAM_NAMES]

    def slab(s):
        return pl.BlockSpec((1, CHUNK, SLAB), lambda b, c, s=s: (s, b * nc + c, 0))

    return pl.pallas_call(
        functools.partial(_ssd_prompt_kernel, nc=nc),
        out_shape=(jax.ShapeDtypeStruct((m, d_inner), BF16),
                   jax.ShapeDtypeStruct((bp, d_inner, D_STATE), F32),
                   jax.ShapeDtypeStruct((bp, CONV_K - 1, conv_dim), F32)),
        grid=(bp, nc),
        in_specs=[slab(2), slab(3), slab(4), slab(0), slab(1),
                  pl.BlockSpec((CHUNK, ns), lambda b, c: (b * nc + c, 0))] + [_resident(a.shape) for a in plist],
        out_specs=(pl.BlockSpec((CHUNK, d_inner), lambda b, c: (b * nc + c, 0)),
                   pl.BlockSpec((1, d_inner, D_STATE), lambda b, c: (b, 0, 0)),
                   pl.BlockSpec((1, CONV_K - 1, conv_dim), lambda b, c: (b, 0, 0))),
        scratch_shapes=[pltpu.VMEM((8 + CHUNK, conv_dim), F32), pltpu.VMEM((D_STATE, d_inner), F32),
                        pltpu.VMEM((CHUNK, 128), F32), pltpu.VMEM((CHUNK, d_inner), F32)],
        compiler_params=_cparams(("arbitrary", "arbitrary"), 40),
        name="ssd_prompt",
    )(proj3, proj3, proj3, proj3, proj3, small, *plist)


def _ssd_sample_kernel(x0_ref, x1_ref, bc_ref, z0_ref, z1_ref, small_ref, conv0_ref, h0_ref, *rest, tdec):
    np_ = len(_SSD_PARAM_NAMES)
    p = dict(zip(_SSD_PARAM_NAMES, rest[:np_]))
    yn_ref, hnew_ref, convnew_ref, xh_ref, state_ref, apad_ref, y_ref, dt_ref = rest[np_:]
    d_inner = state_ref.shape[1]
    nk = CONV_K - 1
    xh_ref[...] = jnp.zeros(xh_ref.shape, F32)
    xh_ref[8 - nk:8, :] = conv0_ref[0]
    xh_ref[8:8 + tdec, 0:SLAB] = x0_ref[0]
    xh_ref[8:8 + tdec, SLAB:2 * SLAB] = x1_ref[0]
    xh_ref[8:8 + tdec, 2 * SLAB:3 * SLAB] = bc_ref[0]
    dt_ref[...] = jnp.zeros(dt_ref.shape, F32)
    dt_ref[0:tdec, :] = small_ref[:, 0:128]
    for j in range(d_inner // 128):
        state_ref[:, j * 128:(j + 1) * 128] = h0_ref[0, j * 128:(j + 1) * 128, :].T
    _ssd_chunk(xh_ref, dt_ref[...], state_ref, apad_ref, y_ref, 0, tdec, p, False)
    z = jnp.concatenate([z0_ref[0], z1_ref[0]], axis=1)
    yn_ref[...] = _gate_norm(y_ref[0:tdec, :], z, p["norm_g"][...], d_inner)
    convnew_ref[0] = xh_ref[pl.ds(8 + tdec - nk, nk), :]
    for j in range(d_inner // 128):
        hnew_ref[0, j * 128:(j + 1) * 128, :] = state_ref[:, j * 128:(j + 1) * 128].T


def _ssd_sample(proj3, small, conv0, h0, params, bs, tdec):
    d_inner = params["expand"].shape[1]
    conv_dim = params["conv_b"].shape[1]
    ns = small.shape[1]
    plist = [params[k] for k in _SSD_PARAM_NAMES]

    def slab(s):
        return pl.BlockSpec((1, tdec, SLAB), lambda b, s=s: (s, b, 0))

    return pl.pallas_call(
        functools.partial(_ssd_sample_kernel, tdec=tdec),
        out_shape=(jax.ShapeDtypeStruct((bs * tdec, d_inner), F32),
                   jax.ShapeDtypeStruct((bs, d_inner, D_STATE), F32),
                   jax.ShapeDtypeStruct((bs, CONV_K - 1, conv_dim), F32)),
        grid=(bs,),
        in_specs=[slab(2), slab(3), slab(4), slab(0), slab(1), pl.BlockSpec((tdec, ns), lambda b: (b, 0)),
                  pl.BlockSpec((1, CONV_K - 1, conv_dim), lambda b: (b, 0, 0)),
                  pl.BlockSpec((1, d_inner, D_STATE), lambda b: (b, 0, 0))] + [_resident(a.shape) for a in plist],
        out_specs=(pl.BlockSpec((tdec, d_inner), lambda b: (b, 0)),
                   pl.BlockSpec((1, d_inner, D_STATE), lambda b: (b, 0, 0)),
                   pl.BlockSpec((1, CONV_K - 1, conv_dim), lambda b: (b, 0, 0))),
        scratch_shapes=[pltpu.VMEM((8 + CHUNK, conv_dim), F32), pltpu.VMEM((D_STATE, d_inner), F32),
                        pltpu.VMEM((CHUNK, 128), F32), pltpu.VMEM((CHUNK, d_inner), F32),
                        pltpu.VMEM((CHUNK, 128), F32)],
        compiler_params=_cparams(("arbitrary",), 40),
        name="ssd_sample",
    )(proj3, proj3, proj3, proj3, proj3, small, conv0, h0, *plist)


def _attn_out(acc, l, wuv_ref, nh, tq):
    o_lat = acc / l
    outs = []
    for p in range(nh // 2):
        pair = jnp.concatenate([o_lat[(2 * p) * tq:(2 * p + 1) * tq], o_lat[(2 * p + 1) * tq:(2 * p + 2) * tq]],
                               axis=1).astype(BF16)
        outs.append(_dot(pair, wuv_ref[p]))
    return jnp.concatenate(outs, axis=1)


def _attn_prompt_kernel(q_ref, k_ref, vt_ref, wuvt_ref, o_ref, m_ref, l_ref, acc_ref, p_ref, *, tk, lp):
    qi = pl.program_id(1)
    nh, tq = q_ref.shape[1], q_ref.shape[2]
    hpg = q_ref.shape[1] // m_ref.shape[0]
    ngrp = nh // hpg
    gq = hpg * tq
    m_ref[...] = jnp.full(m_ref.shape, NEG, F32)
    l_ref[...] = jnp.zeros(l_ref.shape, F32)
    acc_ref[...] = jnp.zeros(acc_ref.shape, F32)
    qidx = qi * tq + (lax.broadcasted_iota(jnp.int32, (1, gq), 1) & (tq - 1))
    nkc = ((qi + 1) * tq + tk - 1) // tk

    def chunk_start(kc):
        return pl.multiple_of(jnp.minimum(kc * tk, lp - tk), 128)

    def stage(kc, scores, values, diagonal):
        if scores:
            k = k_ref[pl.ds(chunk_start(kc), tk), :]
            kidx = chunk_start(kc) + lax.broadcasted_iota(jnp.int32, (tk, 1), 0)
        if values:
            vt = vt_ref[0, :, pl.ds(chunk_start(kc - 1), tk)]
        for r in range(ngrp):
            acc = acc_ref[r]
            if values:
                acc = acc + _dot(vt, p_ref[r])
            if scores:
                q = q_ref[0, hpg * r:hpg * (r + 1)].reshape(gq, q_ref.shape[3])
                s = _dot_nt(k, q)
                if diagonal:
                    s = jnp.where((kidx <= qidx) & (kidx >= jnp.maximum(kc * tk, LEAD)), s, NEG)
                else:
                    s = jnp.concatenate([jnp.where(kidx[0:CHUNK] >= LEAD, s[0:CHUNK], NEG), s[CHUNK:]], axis=0)
                m_old = m_ref[r:r + 1, :]
                m_new = jnp.maximum(m_old, jnp.max(s, axis=0, keepdims=True))
                pexp = jnp.exp2(s - m_new)
                alpha = jnp.exp2(m_old - m_new)
                l_ref[r:r + 1, :] = alpha * l_ref[r:r + 1, :] + jnp.sum(pexp, axis=0, keepdims=True)
                p_ref[r] = pexp.astype(BF16)
                m_ref[r:r + 1, :] = m_new
                acc = alpha * acc
            acc_ref[r] = acc

    def body(kc, carry):
        stage(kc, True, True, False)
        return carry

    @pl.when(nkc == 1)
    def _():
        stage(0, True, False, True)

    @pl.when(nkc > 1)
    def _():
        stage(0, True, False, False)
        lax.fori_loop(1, nkc - 1, body, 0)
        stage(nkc - 1, True, True, True)

    stage(nkc, False, True, False)
    for r in range(ngrp):
        o_lat_t = (acc_ref[r] / l_ref[r:r + 1, :]).astype(BF16)
        for j in range(hpg // 2):
            h0 = hpg * r + 2 * j
            pair = jnp.concatenate([_dot(wuvt_ref[h0 + e], o_lat_t[:, (2 * j + e) * tq:(2 * j + e + 1) * tq])
                                    for e in range(2)], axis=0)
            o_ref[:, h0 * V_HEAD:(h0 + 2) * V_HEAD] = pair.T.astype(o_ref.dtype)


def _attn_prompt(q4, kcat, vt, wuvt, bp, lp):
    nq = lp // CHUNK
    nh = MLA_HEADS
    kv_lora = vt.shape[1]
    tk = 512
    hpg = 4
    assert lp % 128 == 0 and lp >= tk and 2 * V_HEAD == 128 and nh % hpg == 0
    return pl.pallas_call(
        functools.partial(_attn_prompt_kernel, tk=tk, lp=lp),
        out_shape=jax.ShapeDtypeStruct((bp * lp, nh * V_HEAD), BF16),
        grid=(bp, nq),
        in_specs=[pl.BlockSpec((1, nh, CHUNK, QK_DIM), lambda b, i: (b * nq + i, 0, 0, 0)),
                  pl.BlockSpec((lp, QK_DIM), lambda b, i: (b, 0)),
                  pl.BlockSpec((1, kv_lora, lp), lambda b, i: (b, 0, 0)),
                  _resident(wuvt.shape)],
        out_specs=pl.BlockSpec((CHUNK, nh * V_HEAD), lambda b, i: (b * nq + i, 0)),
        scratch_shapes=[pltpu.VMEM((nh // hpg, hpg * CHUNK), F32), pltpu.VMEM((nh // hpg, hpg * CHUNK), F32),
                        pltpu.VMEM((nh // hpg, kv_lora, hpg * CHUNK), F32),
                        pltpu.VMEM((nh // hpg, tk, hpg * CHUNK), BF16)],
        compiler_params=_cparams(("arbitrary", "arbitrary"), 40),
        name="attn_prompt",
    )(q4, kcat, vt, wuvt)


def _attn_sample_kernel(pt_ref, q_ref, knew_ref, vnew_ref, foldt_ref, wuv_ref, ckv_hbm, krt_hbm, o_ref,
                        cbuf, rbuf, sem, m_ref, l_ref, acc_ref, q_scr, qlt_scr, qpt_scr, *, npg, nchunk, tdec):
    b = pl.program_id(0)
    c = pl.program_id(1)
    nb = pl.num_programs(0)
    step = b * nchunk + c
    slot = step % 2
    nh = q_ref.shape[1]
    rows = nh * tdec
    kv_lora = cbuf.shape[-1]
    page = cbuf.shape[2]

    def copies(bb, cc, sl):
        out = []
        for j in range(npg):
            pg = pt_ref[bb, cc * npg + j]
            out.append(pltpu.make_async_copy(ckv_hbm.at[pg], cbuf.at[sl, j], sem.at[0, sl]))
            out.append(pltpu.make_async_copy(krt_hbm.at[pg], rbuf.at[sl, :, pl.ds(j * page, page)], sem.at[1, sl]))
        return out

    @pl.when(step == 0)
    def _():
        for cp in copies(b, c, slot):
            cp.start()

    @pl.when(step + 1 < nb * nchunk)
    def _():
        nxt = step + 1
        for cp in copies(nxt // nchunk, nxt % nchunk, 1 - slot):
            cp.start()

    @pl.when(c == 0)
    def _():
        m_ref[...] = jnp.full(m_ref.shape, NEG, F32)
        l_ref[...] = jnp.zeros(l_ref.shape, F32)
        acc_ref[...] = jnp.zeros(acc_ref.shape, F32)
        q = q_ref[0].reshape(rows, q_ref.shape[3])
        q_scr[...] = q.astype(BF16)
        for j in range(kv_lora // 128):
            qlt_scr[j * 128:(j + 1) * 128, :] = q[:, j * 128:(j + 1) * 128].T.astype(BF16)
        qpt_scr[...] = lax.dot_general(foldt_ref[...], q[:, kv_lora:kv_lora + 128], (((1,), (1,)), ((), ())),
                                       preferred_element_type=F32, precision=HIGHEST).astype(BF16)

    for cp in copies(b, c, slot):
        cp.wait()

    def lanes_to_rows(x):
        return jnp.broadcast_to(x, (rows, rows)).T

    def widen(x):
        return jnp.concatenate([lanes_to_rows(x)] * (kv_lora // 128), axis=1)

    def update(st, s_t, v):
        m_old = m_ref[st:st + 1, :]
        m_new = jnp.maximum(m_old, jnp.max(s_t, axis=0, keepdims=True))
        p_t = jnp.exp2(s_t - m_new)
        alpha = jnp.exp2(m_old - m_new)
        l_ref[st:st + 1, :] = alpha * l_ref[st:st + 1, :] + jnp.sum(p_t, axis=0, keepdims=True)
        m_ref[st:st + 1, :] = m_new
        nk = s_t.shape[0]
        if nk % 128:
            pmat = p_t.T
        else:
            pmat = jnp.concatenate([p_t[j * 128:(j + 1) * 128, :].T for j in range(nk // 128)], axis=1)
        acc_ref[st] = acc_ref[st] * widen(alpha) + _dot(pmat.astype(BF16), v)

    nstream = m_ref.shape[0]
    pps = npg // nstream
    for st in range(nstream):
        kc = cbuf[slot, st * pps:(st + 1) * pps].reshape(pps * page, kv_lora).astype(BF16)
        kr = jnp.concatenate([rbuf[slot, :, j * page:(j + 1) * page].T for j in range(st * pps, (st + 1) * pps)],
                             axis=0).astype(BF16)
        update(st, _dot(kc, qlt_scr[...]) + _dot(kr, qpt_scr[...]), kc)

    @pl.when(c == nchunk - 1)
    def _():
        knew = knew_ref[...].astype(BF16)
        s_t = _dot_nt(knew, q_scr[...])
        nk = knew.shape[0]
        kid = lax.broadcasted_iota(jnp.int32, (nk, 1), 0)
        t = lax.broadcasted_iota(jnp.int32, (1, rows), 1) & (tdec - 1)
        ok = (kid >= b * tdec) & (kid <= b * tdec + t)
        update(0, jnp.where(ok, s_t, NEG), vnew_ref[...].astype(BF16))
        m_all = jnp.max(m_ref[...], axis=0, keepdims=True)
        l_all = jnp.zeros((1, rows), F32)
        acc = jnp.zeros((rows, kv_lora), F32)
        for st in range(nstream):
            w_st = jnp.exp2(m_ref[st:st + 1, :] - m_all)
            l_all = l_all + w_st * l_ref[st:st + 1, :]
            acc = acc + widen(w_st) * acc_ref[st]
        o_ref[...] = _attn_out(acc, widen(l_all), wuv_ref, nh, tdec)


def _attn_sample(page_table, q4, kcat_new, v_new, fold, wuv, cache_ckv, cache_krt, bs, tdec):
    nh = MLA_HEADS
    n_pages = page_table.shape[1]
    page, kv_lora = cache_ckv.shape[1], cache_ckv.shape[2]
    rope = cache_krt.shape[1]
    npg = 16
    while n_pages % npg:
        npg //= 2
    nchunk = n_pages // npg
    nstream = 1
    rows = nh * tdec
    assert tdec & (tdec - 1) == 0 and rows == 128 and page == 128
    grid_spec = pltpu.PrefetchScalarGridSpec(
        num_scalar_prefetch=1,
        grid=(bs, nchunk),
        in_specs=[pl.BlockSpec((1, nh, tdec, QK_DIM), lambda b, c, pt: (b, 0, 0, 0)),
                  pl.BlockSpec(kcat_new.shape, lambda b, c, pt: (0, 0), pipeline_mode=pl.Buffered(1)),
                  pl.BlockSpec(v_new.shape, lambda b, c, pt: (0, 0), pipeline_mode=pl.Buffered(1)),
                  pl.BlockSpec(fold.shape, lambda b, c, pt: (0, 0), pipeline_mode=pl.Buffered(1)),
                  pl.BlockSpec(wuv.shape, lambda b, c, pt: (0, 0, 0), pipeline_mode=pl.Buffered(1)),
                  pl.BlockSpec(memory_space=pl.ANY), pl.BlockSpec(memory_space=pl.ANY)],
        out_specs=pl.BlockSpec((tdec, nh * V_HEAD), lambda b, c, pt: (b, 0)),
        scratch_shapes=[pltpu.VMEM((2, npg, page, kv_lora), F32), pltpu.VMEM((2, rope, npg * page), F32),
                        pltpu.SemaphoreType.DMA((2, 2)),
                        pltpu.VMEM((nstream, rows), F32), pltpu.VMEM((nstream, rows), F32),
                        pltpu.VMEM((nstream, rows, kv_lora), F32),
                        pltpu.VMEM((rows, QK_DIM), BF16), pltpu.VMEM((kv_lora, rows), BF16),
                        pltpu.VMEM((rope, rows), BF16)],
    )
    return pl.pallas_call(
        functools.partial(_attn_sample_kernel, npg=npg, nchunk=nchunk, tdec=tdec),
        out_shape=jax.ShapeDtypeStruct((bs * tdec, nh * V_HEAD), F32),
        grid_spec=grid_spec,
        compiler_params=_cparams(("arbitrary", "arbitrary"), 40),
        name="attn_sample",
    )(page_table, q4, kcat_new, v_new, fold, wuv, cache_ckv, cache_krt)


def _merge_kernel(yn_ref, o_ref, ga_ref, gb_ref, h_ref, wa_ref, wb_ref, wo_ref, g_ref, out_ref):
    a = _dot(yn_ref[...].astype(BF16), wa_ref[...])
    bb = _dot(o_ref[...].astype(BF16), wb_ref[...])
    merged = jax.nn.sigmoid(ga_ref[0].astype(F32)) * a + jax.nn.sigmoid(gb_ref[0].astype(F32)) * bb
    mix = _dot(merged.astype(BF16), wo_ref[...])
    out_ref[...] = h_ref[...] + _rms(mix, g_ref[...])


def _merge(yn, o, proj3, h, wa, wb, wo, g):
    m, d = h.shape
    tm = _row_tile(m, 512)
    return pl.pallas_call(
        _merge_kernel,
        out_shape=jax.ShapeDtypeStruct((m, d), F32),
        grid=(m // tm,),
        in_specs=[pl.BlockSpec((tm, yn.shape[1]), lambda i: (i, 0)), pl.BlockSpec((tm, o.shape[1]), lambda i: (i, 0)),
                  pl.BlockSpec((1, tm, SLAB), lambda i: (5, i, 0)), pl.BlockSpec((1, tm, SLAB), lambda i: (6, i, 0)),
                  pl.BlockSpec((tm, d), lambda i: (i, 0)),
                  _resident(wa.shape), _resident(wb.shape), _resident(wo.shape), _resident((1, d))],
        out_specs=pl.BlockSpec((tm, d), lambda i: (i, 0)),
        compiler_params=_cparams(("arbitrary",), 40),
        name="gated_merge",
    )(yn, o, proj3, proj3, h, wa, wb, wo, g)


def _prep_weights(w, d_inner, conv_dim, nheads, q_lora, kv_lora):
    d = w["w_in"].shape[0]
    offs = np.cumsum([0, d_inner, conv_dim, nheads, q_lora, kv_lora, QK_ROPE, d, d])
    seg = {k: w["w_in"][:, offs[i]:offs[i + 1]]
           for i, k in enumerate(("z", "xbc", "dt", "q_a", "kv_a", "k_pe", "ga", "gb"))}
    pad = SLAB - q_lora - kv_lora
    wbig = jnp.concatenate([seg["z"], seg["xbc"], seg["ga"], seg["gb"], seg["q_a"], seg["kv_a"],
                            jnp.zeros((d, pad), F32)], axis=1).astype(BF16)
    swap = np.concatenate([np.arange(QK_ROPE // 2, QK_ROPE), np.arange(QK_ROPE // 2)])
    wsmall = jnp.concatenate([seg["dt"], jnp.zeros((d, 128 - nheads), F32), jnp.tile(seg["k_pe"], (1, 4)),
                              jnp.tile(seg["k_pe"][:, swap], (1, 4))], axis=1).astype(BF16)
    nh = MLA_HEADS
    wq3 = w["w_q_b"].reshape(q_lora, nh, QK_NOPE + QK_ROPE)
    wq = jnp.concatenate([wq3[:, :, :QK_NOPE].reshape(q_lora, nh * QK_NOPE),
                          wq3[:, :, QK_NOPE:].reshape(q_lora, nh * QK_ROPE),
                          wq3[:, :, QK_NOPE:][:, :, swap].reshape(q_lora, nh * QK_ROPE)], axis=1).astype(BF16)
    wk = jnp.transpose(w["w_uk"], (1, 2, 0)).reshape(nh // 2, 2, QK_NOPE, kv_lora)
    zk = jnp.zeros((nh // 2, QK_NOPE, kv_lora), F32)
    wuk = jnp.concatenate([jnp.concatenate([wk[:, 0], zk], axis=2),
                           jnp.concatenate([zk, wk[:, 1]], axis=2)], axis=1).astype(BF16)
    wv = jnp.transpose(w["w_uv"], (1, 0, 2)).reshape(nh // 2, 2, kv_lora, V_HEAD)
    zv = jnp.zeros((nh // 2, kv_lora, V_HEAD), F32)
    wuvt = jnp.transpose(w["w_uv"], (1, 2, 0)).astype(BF16)
    wuv = jnp.concatenate([jnp.concatenate([wv[:, 0], zv], axis=2),
                           jnp.concatenate([zv, wv[:, 1]], axis=2)], axis=1).astype(BF16)
    def expand3(width):
        e = np.kron(np.eye(128, nheads, dtype=np.float32), np.ones((1, width), np.float32))
        return jnp.asarray(np.tile(e, (3, 1)), BF16)

    lane_pad = lambda a: jnp.pad(a, (0, 128 - nheads))[None]
    ssd = dict(conv_w=w["conv_w"], conv_b=w["conv_b"][None], dt_bias=lane_pad(w["dt_bias"]),
               a_log=lane_pad(w["a_log"]),
               dskip_x=jnp.repeat(w["d_skip"], SSM_HEAD_DIM)[None], norm_g=w["ssm_norm_g"][None],
               tril=jnp.asarray(np.tril(np.ones((CHUNK, CHUNK), np.float32))),
               expand=expand3(SSM_HEAD_DIM), expand128=expand3(128),
               shift=jnp.asarray(np.concatenate([np.eye(CHUNK, k=k - (CONV_K - 1), dtype=np.float32)
                                                 for k in range(CONV_K - 1)], axis=0), BF16))
    inv = ROPE_THETA ** (-jnp.arange(0, QK_ROPE, 2, dtype=F32) / QK_ROPE)
    inv128 = jnp.tile(jnp.concatenate([inv, inv]), 4)[None]
    sign128 = jnp.asarray(np.tile(np.concatenate([-np.ones(QK_ROPE // 2), np.ones(QK_ROPE // 2)]), 4)[None], F32)
    foldt = jnp.asarray(np.tile(np.eye(QK_ROPE, dtype=np.float32), (1, 4)))
    bf = lambda a: a.astype(BF16)
    row = lambda a: a[None]
    return dict(
        wbig=wbig, wsmall=wsmall, wq=wq, wuk=wuk, wuv=wuv, wuvt=wuvt, ssd=ssd, inv128=inv128, sign128=sign128, foldt=foldt,
        ffn1=(row(w["ffn1_pre_g"]), bf(w["ffn1_w_gate"]), bf(w["ffn1_w_up"]), bf(w["ffn1_w_down"]),
              row(w["ffn1_post_g"])),
        ffn2=(row(w["ffn2_pre_g"]), bf(w["ffn2_w_gate"]), bf(w["ffn2_w_up"]), bf(w["ffn2_w_down"]),
              row(w["ffn2_post_g"])),
        mix_pre_g=row(w["mix_pre_g"]), mix_post_g=row(w["mix_post_g"]), q_g=row(w["q_a_norm_g"]),
        kv_g=row(w["kv_a_norm_g"]), wa=bf(w["w_a_out"]), wb=bf(w["w_b_out"]), wo=bf(w["w_o"]))


def kernel(x_prompt, x_sample, cache_kv_latent, cache_k_rope, state_ssm, state_conv, page_table, meta_tokens,
           ffn1_pre_g, ffn1_w_gate, ffn1_w_up, ffn1_w_down, ffn1_post_g,
           mix_pre_g, w_in, conv_w, conv_b, dt_bias, a_log, d_skip, ssm_norm_g,
           q_a_norm_g, w_q_b, kv_a_norm_g, w_uk, w_uv, w_a_out, w_b_out, w_o, mix_post_g,
           ffn2_pre_g, ffn2_w_gate, ffn2_w_up, ffn2_w_down, ffn2_post_g):
    names = ("ffn1_pre_g", "ffn1_w_gate", "ffn1_w_up", "ffn1_w_down", "ffn1_post_g", "mix_pre_g", "w_in", "conv_w",
             "conv_b", "dt_bias", "a_log", "d_skip", "ssm_norm_g", "q_a_norm_g", "w_q_b", "kv_a_norm_g", "w_uk",
             "w_uv", "w_a_out", "w_b_out", "w_o", "mix_post_g", "ffn2_pre_g", "ffn2_w_gate", "ffn2_w_up",
             "ffn2_w_down", "ffn2_post_g")
    stacked = dict(zip(names, (ffn1_pre_g, ffn1_w_gate, ffn1_w_up, ffn1_w_down, ffn1_post_g, mix_pre_g, w_in,
                               conv_w, conv_b, dt_bias, a_log, d_skip, ssm_norm_g, q_a_norm_g, w_q_b, kv_a_norm_g,
                               w_uk, w_uv, w_a_out, w_b_out, w_o, mix_post_g, ffn2_pre_g, ffn2_w_gate, ffn2_w_up,
                               ffn2_w_down, ffn2_post_g)))
    depth = w_in.shape[0]
    bp, seq, d = x_prompt.shape
    bs, tdec, _ = x_sample.shape
    nheads = dt_bias.shape[1]
    d_inner = nheads * SSM_HEAD_DIM
    conv_dim = conv_b.shape[1]
    q_lora, kv_lora = q_a_norm_g.shape[1], kv_a_norm_g.shape[1]
    assert seq % CHUNK == 0 and conv_dim == 3 * SLAB and d_inner == 2 * SLAB and d == SLAB
    assert q_lora + kv_lora <= SLAB and CHUNK % tdec == 0
    n_pages, page = page_table.shape[1], cache_kv_latent.shape[2]
    past_len = n_pages * page
    lp = LEAD + N_META + seq
    nc = lp // CHUNK
    real = LEAD

    hp = jnp.concatenate([jnp.zeros((bp, LEAD, d), F32), jnp.broadcast_to(meta_tokens[None], (bp, N_META, d)),
                          x_prompt], axis=1).reshape(bp * lp, d)
    hs = x_sample.reshape(bs * tdec, d)
    pos_p = jnp.tile(jnp.arange(lp, dtype=jnp.int32) - LEAD, bp).astype(F32)[:, None]
    pos_s = jnp.tile(past_len + jnp.arange(tdec, dtype=jnp.int32), bs).astype(F32)[:, None]

    outs = [[] for _ in range(8)]
    for l in range(depth):
        w = _prep_weights({k: v[l] for k, v in stacked.items()}, d_inner, conv_dim, nheads, q_lora, kv_lora)

        h1 = _ffn(hp, *w["ffn1"])
        proj3, small = _inproj(h1, w["mix_pre_g"], w["wbig"], w["wsmall"], BF16)
        yn, h_fin, conv_new = _ssd_prompt(proj3, small, w["ssd"], bp, nc)
        c_kv, kpe, kcat, q4 = _qkv_prep(proj3, small, pos_p, w["q_g"], w["kv_g"], w["wq"], w["wuk"], w["inv128"],
                                        w["sign128"], CHUNK, BF16)
        vt = jnp.transpose(c_kv.astype(BF16).reshape(bp, lp, kv_lora), (0, 2, 1))
        o = _attn_prompt(q4, kcat, vt, w["wuvt"], bp, lp)
        h2 = _merge(yn, o, proj3, h1, w["wa"], w["wb"], w["wo"], w["mix_post_g"])
        hp = _ffn(h2, *w["ffn2"])
        outs[0].append(c_kv.reshape(bp, lp, kv_lora)[:, real:])
        outs[1].append(kpe.reshape(bp, lp, 128)[:, real:, :QK_ROPE])
        outs[2].append(h_fin.reshape(bp, nheads, SSM_HEAD_DIM, D_STATE))
        outs[3].append(conv_new)

        g1 = _ffn(hs, *w["ffn1"])
        proj3s, smalls = _inproj(g1, w["mix_pre_g"], w["wbig"], w["wsmall"], F32)
        yns, h_new, conv_new_s = _ssd_sample(proj3s, smalls, state_conv[l],
                                             state_ssm[l].reshape(bs, d_inner, D_STATE), w["ssd"], bs, tdec)
        c_kv_s, kpe_s, kcat_s, q4s = _qkv_prep(proj3s, smalls, pos_s, w["q_g"], w["kv_g"], w["wq"], w["wuk"],
                                               w["inv128"], w["sign128"], tdec, F32)
        os_ = _attn_sample(page_table, q4s, kcat_s, c_kv_s, w["foldt"], w["wuv"], cache_kv_latent[l],
                           jnp.transpose(cache_k_rope[l], (0, 2, 1)), bs, tdec)
        g2 = _merge(yns, os_, proj3s, g1, w["wa"], w["wb"], w["wo"], w["mix_post_g"])
        hs = _ffn(g2, *w["ffn2"])
        outs[4].append(c_kv_s.reshape(bs, tdec, kv_lora))
        outs[5].append(kpe_s.reshape(bs, tdec, 128)[:, :, :QK_ROPE])
        outs[6].append(h_new.reshape(bs, nheads, SSM_HEAD_DIM, D_STATE))
        outs[7].append(conv_new_s)

    y_prompt = hp.reshape(bp, lp, d)[:, real + N_META:]
    y_sample = hs.reshape(bs, tdec, d)
    return (y_prompt, y_sample) + tuple(jnp.stack(o) for o in outs)
```

```python
import functools
import math

import numpy as np
import jax
import jax.numpy as jnp
from jax import lax
from jax.experimental import pallas as pl
from jax.experimental.pallas import tpu as pltpu

F32 = jnp.float32
BF16 = jnp.bfloat16
HIGHEST = lax.Precision.HIGHEST

EPS = 1e-6
N_META = 16
CHUNK = 128
LEAD = (-N_META) % CHUNK
SSM_HEAD_DIM = 64
SSM_GROUPS = 4
D_STATE = 128
CONV_K = 4
MLA_HEADS = 16
QK_NOPE = 64
QK_ROPE = 32
V_HEAD = 64
ROPE_THETA = 10000.0
ATTN_SCALE = (QK_NOPE + QK_ROPE) ** -0.5
Q_SCALE = ATTN_SCALE * math.log2(math.e)
NEG = -1e30
SLAB = 1024
QK_DIM = 384
MIB = 1024 * 1024


def _cparams(sem, vmem_mib):
    return pltpu.CompilerParams(dimension_semantics=sem, vmem_limit_bytes=int(vmem_mib * MIB))


def _resident(shape):
    nd = len(shape)
    return pl.BlockSpec(shape, lambda *_: (0,) * nd, pipeline_mode=pl.Buffered(1))


def _rms(x, g):
    return x * lax.rsqrt(jnp.mean(x * x, axis=-1, keepdims=True) + EPS) * g


def _dot(a, b):
    return jnp.dot(a, b, preferred_element_type=F32)


def _dot_exact(a, b):
    return jnp.dot(a, b, preferred_element_type=F32, precision=HIGHEST)


def _dot_nt(a, b):
    return lax.dot_general(a, b, (((1,), (1,)), ((), ())), preferred_element_type=F32)


def _expand_exact(x, e3):
    hi = x.astype(BF16)
    r1 = x - hi.astype(F32)
    mid = r1.astype(BF16)
    lo = (r1 - mid.astype(F32)).astype(BF16)
    return _dot(jnp.concatenate([hi, mid, lo], axis=1), e3)


def _softplus(x):
    return jnp.maximum(x, 0.0) + jnp.log1p(jnp.exp(-jnp.abs(x)))


def _row_tile(m, cap, mult=16):
    for t in range(min(cap, m), mult - 1, -1):
        if m % t == 0 and t % mult == 0:
            return t
    raise ValueError(f"no row tile for {m}")


def _ffn_kernel(x_ref, pre_ref, wg_ref, wu_ref, wd_ref, post_ref, o_ref, *, chunks):
    x = x_ref[...]
    xn = _rms(x, pre_ref[...]).astype(BF16)
    acc = jnp.zeros(x.shape, F32)
    for lo, sz in chunks:
        g = _dot(xn, wg_ref[:, lo:lo + sz])
        u = _dot(xn, wu_ref[:, lo:lo + sz])
        h = (jax.nn.silu(g) * u).astype(BF16)
        acc = acc + _dot(h, wd_ref[lo:lo + sz, :])
    o_ref[...] = x + 0.5 * _rms(acc, post_ref[...])


def _ffn_chunks(f):
    out, lo = [], 0
    while lo < f:
        sz = min(1024, f - lo)
        out.append((lo, sz))
        lo += sz
    return tuple(out)


def _ffn(x, pre_g, wg, wu, wd, post_g):
    m, d = x.shape
    f = wg.shape[1]
    tm = _row_tile(m, 512)
    return pl.pallas_call(
        functools.partial(_ffn_kernel, chunks=_ffn_chunks(f)),
        out_shape=jax.ShapeDtypeStruct((m, d), F32),
        grid=(m // tm,),
        in_specs=[pl.BlockSpec((tm, d), lambda i: (i, 0)), _resident((1, d)), _resident((d, f)),
                  _resident((d, f)), _resident((f, d)), _resident((1, d))],
        out_specs=pl.BlockSpec((tm, d), lambda i: (i, 0)),
        compiler_params=_cparams(("arbitrary",), 52),
        name="ffn_block",
    )(x, pre_g, wg, wu, wd, post_g)


def _inproj_kernel(x_ref, g_ref, wbig_ref, wsmall_ref, proj_ref, small_ref, *, nslab):
    xn = _rms(x_ref[...], g_ref[...]).astype(BF16)
    for s in range(nslab):
        proj_ref[s] = _dot(xn, wbig_ref[:, s * SLAB:(s + 1) * SLAB]).astype(proj_ref.dtype)
    small_ref[...] = _dot(xn, wsmall_ref[...])


def _inproj(h, g, wbig, wsmall, proj_dtype):
    m, d = h.shape
    nslab = wbig.shape[1] // SLAB
    ns = wsmall.shape[1]
    tm = _row_tile(m, 512)
    return pl.pallas_call(
        functools.partial(_inproj_kernel, nslab=nslab),
        out_shape=(jax.ShapeDtypeStruct((nslab, m, SLAB), proj_dtype), jax.ShapeDtypeStruct((m, ns), F32)),
        grid=(m // tm,),
        in_specs=[pl.BlockSpec((tm, d), lambda i: (i, 0)), _resident((1, d)), _resident(wbig.shape),
                  _resident(wsmall.shape)],
        out_specs=(pl.BlockSpec((nslab, tm, SLAB), lambda i: (0, i, 0)), pl.BlockSpec((tm, ns), lambda i: (i, 0))),
        compiler_params=_cparams(("arbitrary",), 52),
        name="in_proj",
    )(h, g, wbig, wsmall)


def _qkv_kernel(s7_ref, small_ref, pos_ref, qg_ref, kvg_ref, wq_ref, wuk_ref, inv_ref, sign_ref,
                ckv_ref, kpe_ref, kcat_ref, q_ref, *, tqb):
    s7 = s7_ref[0].astype(F32)
    tm = s7.shape[0]
    q_lora = qg_ref.shape[1]
    kv_lora = kvg_ref.shape[1]
    nh = MLA_HEADS
    c_kv = _rms(s7[:, q_lora:q_lora + kv_lora], kvg_ref[...])
    ckv_ref[...] = c_kv
    ang = pos_ref[...] * inv_ref[...]
    cos = jnp.cos(ang)
    sin = jnp.sin(ang) * sign_ref[...]
    small = small_ref[...]
    kpe = small[:, 128:256] * cos + small[:, 256:384] * sin
    kpe_ref[...] = kpe
    kcat_ref[:, 0:kv_lora] = c_kv.astype(kcat_ref.dtype)
    kcat_ref[:, kv_lora:kv_lora + 128] = kpe.astype(kcat_ref.dtype)

    qn = _rms(s7[:, 0:q_lora], qg_ref[...]).astype(BF16)
    nope_w = nh * QK_NOPE
    pe_w = nh * QK_ROPE
    q_nope = _dot(qn, wq_ref[:, 0:nope_w])
    q_pe = _dot(qn, wq_ref[:, nope_w:nope_w + pe_w])
    q_rot = _dot(qn, wq_ref[:, nope_w + pe_w:nope_w + 2 * pe_w])
    cos_w = jnp.concatenate([cos] * (pe_w // 128), axis=1)
    sin_w = jnp.concatenate([sin] * (pe_w // 128), axis=1)
    q_pe = (q_pe * cos_w + q_rot * sin_w) * Q_SCALE
    lane = lax.broadcasted_iota(jnp.int32, (1, 128), 1)
    nblk = tm // tqb
    for p in range(nh // 2):
        qn_pair = q_nope[:, p * 128:(p + 1) * 128].astype(BF16)
        q_lat = _dot(qn_pair, wuk_ref[p]) * Q_SCALE
        for e in range(2):
            hd = 2 * p + e
            pe_blk = q_pe[:, (hd // 4) * 128:(hd // 4 + 1) * 128]
            sel = (lane >= (hd % 4) * QK_ROPE) & (lane < (hd % 4 + 1) * QK_ROPE)
            pe_blk = jnp.where(sel, pe_blk, 0.0)
            lat = q_lat[:, e * kv_lora:(e + 1) * kv_lora]
            for g in range(nblk):
                q_ref[g, hd, :, 0:kv_lora] = lat[g * tqb:(g + 1) * tqb].astype(q_ref.dtype)
                q_ref[g, hd, :, kv_lora:kv_lora + 128] = pe_blk[g * tqb:(g + 1) * tqb].astype(q_ref.dtype)


def _qkv_prep(proj3, small, pos, qg, kvg, wq, wuk, inv128, sign128, tqb, dtype):
    nslab, m, _ = proj3.shape
    kv_lora = kvg.shape[1]
    tm = _row_tile(m, 512, max(16, tqb))
    nblk = tm // tqb
    return pl.pallas_call(
        functools.partial(_qkv_kernel, tqb=tqb),
        out_shape=(jax.ShapeDtypeStruct((m, kv_lora), F32), jax.ShapeDtypeStruct((m, 128), F32),
                   jax.ShapeDtypeStruct((m, QK_DIM), dtype),
                   jax.ShapeDtypeStruct((m // tqb, MLA_HEADS, tqb, QK_DIM), dtype)),
        grid=(m // tm,),
        in_specs=[pl.BlockSpec((1, tm, SLAB), lambda i: (nslab - 1, i, 0)),
                  pl.BlockSpec((tm, small.shape[1]), lambda i: (i, 0)),
                  pl.BlockSpec((tm, 1), lambda i: (i, 0)),
                  _resident(qg.shape), _resident(kvg.shape), _resident(wq.shape), _resident(wuk.shape),
                  _resident((1, 128)), _resident((1, 128))],
        out_specs=(pl.BlockSpec((tm, kv_lora), lambda i: (i, 0)), pl.BlockSpec((tm, 128), lambda i: (i, 0)),
                   pl.BlockSpec((tm, QK_DIM), lambda i: (i, 0)),
                   pl.BlockSpec((nblk, MLA_HEADS, tqb, QK_DIM), lambda i: (i, 0, 0, 0))),
        compiler_params=_cparams(("arbitrary",), 48),
        name="qkv_prep",
    )(proj3, small, pos, qg, kvg, wq, wuk, inv128, sign128)


def _ssd_chunk(xh_ref, dt_raw, state_ref, apad_ref, y_ref, lo, hi, p, bf16_input):
    d_inner = state_ref.shape[1]
    nheads = d_inner // SSM_HEAD_DIM
    gw = d_inner // SSM_GROUPS
    rows = lax.broadcasted_iota(jnp.int32, (CHUNK, 1), 0)
    valid = (rows >= lo) & (rows < hi)
    ntap = CONV_K - 1

    def conv_rows(nrows):
        out = p["conv_b"][...]
        for k in range(CONV_K):
            out = out + p["conv_w"][k:k + 1, :] * xh_ref[pl.ds(8 - ntap + k, nrows), :]
        return out

    if bf16_input:
        cur = xh_ref[8:8 + CHUNK, :]
        shifted = _dot(p["shift"][...], cur.astype(BF16))
        conv = p["conv_b"][...]
        for k in range(ntap):
            conv = conv + p["conv_w"][k:k + 1, :] * shifted[k * CHUNK:(k + 1) * CHUNK]
        conv = conv + p["conv_w"][ntap:CONV_K, :] * cur
        conv = jnp.concatenate([conv_rows(8), conv[8:]], axis=0)
    else:
        conv = conv_rows(CHUNK)
    xc = jnp.where(valid, jax.nn.silu(conv), 0.0)
    xs = xc[:, 0:d_inner]
    bm = xc[:, d_inner:d_inner + SSM_GROUPS * D_STATE]
    cm = xc[:, d_inner + SSM_GROUPS * D_STATE:]

    dt = jnp.where(valid, _softplus(dt_raw + p["dt_bias"][...]), 0.0)
    da = dt * (-jnp.exp(p["a_log"][...]))
    a_cs = _dot_exact(p["tril"][...], da) * math.log2(math.e)
    dt_x = _expand_exact(dt, p["expand"][...])
    acs_x = _expand_exact(a_cs, p["expand"][...])
    alast_x = acs_x[CHUNK - 1:CHUNK, :]
    acs_b = _expand_exact(a_cs, p["expand128"][...])
    apad_ref[...] = a_cs.T
    xdt = xs * dt_x
    xds = (xdt * jnp.exp2(alast_x - acs_x)).astype(BF16)
    exp_acs = jnp.exp2(acs_x)
    causal = rows >= lax.broadcasted_iota(jnp.int32, (1, CHUNK), 1)
    lane = lax.broadcasted_iota(jnp.int32, (1, 128), 1)
    hpg = nheads // SSM_GROUPS

    for g in range(SSM_GROUPS):
        gs = slice(g * gw, (g + 1) * gw)
        bg = bm[:, g * D_STATE:(g + 1) * D_STATE]
        cg = cm[:, g * D_STATE:(g + 1) * D_STATE].astype(BF16)
        cb = _dot_nt(cg, bg.astype(BF16))
        st = state_ref[:, gs]
        y_off = _dot(cg, st.astype(BF16)) * exp_acs[:, gs]
        state_ref[:, gs] = st * jnp.exp2(alast_x[:, gs]) + _dot(bg.T.astype(BF16), xds[:, gs])
        y_ref[:, gs] = y_off + xs[:, gs] * p["dskip_x"][:, gs]
        for pr in range(hpg // 2):
            w2 = []
            for e in range(2):
                hd = g * hpg + 2 * pr + e
                seg = acs_b[:, hd * 128:(hd + 1) * 128] - apad_ref[pl.ds(hd, 1), :]
                decay = jnp.exp2(jnp.where(causal, seg, NEG))
                w2.append((cb * decay).astype(BF16))
            ls = slice(g * gw + pr * 128, g * gw + (pr + 1) * 128)
            xp = xdt[:, ls]
            x2 = jnp.concatenate([jnp.where(lane < SSM_HEAD_DIM, xp, 0.0),
                                  jnp.where(lane >= SSM_HEAD_DIM, xp, 0.0)], axis=0).astype(BF16)
            y_ref[:, ls] = y_ref[:, ls] + _dot(jnp.concatenate(w2, axis=1), x2)


def _gate_norm(y, z, g, d_inner):
    yg = y * jax.nn.silu(z)
    gw = d_inner // SSM_GROUPS
    outs = []
    for k in range(SSM_GROUPS):
        v = yg[:, k * gw:(k + 1) * gw]
        outs.append(_rms(v, g[:, k * gw:(k + 1) * gw]))
    return jnp.concatenate(outs, axis=1)


_SSD_PARAM_NAMES = ("conv_w", "conv_b", "dt_bias", "a_log", "dskip_x", "norm_g", "tril", "expand", "expand128",
                    "shift")


def _ssd_prompt_kernel(x0_ref, x1_ref, bc_ref, z0_ref, z1_ref, small_ref, m0_ref, m1_ref, mbc_ref, msmall_ref,
                       *rest, nc):
    np_ = len(_SSD_PARAM_NAMES)
    p = dict(zip(_SSD_PARAM_NAMES, rest[:np_]))
    yn_ref, hfin_ref, convnew_ref, xh_ref, state_ref, apad_ref, y_ref = rest[np_:]
    c = pl.program_id(1)
    d_inner = state_ref.shape[1]
    assert x0_ref.dtype == BF16

    @pl.when(c == 0)
    def _():
        xh_ref[0:8, :] = jnp.zeros((8, xh_ref.shape[1]), F32)
        state_ref[...] = jnp.zeros(state_ref.shape, F32)
        valid = lax.broadcasted_iota(jnp.int32, (CHUNK, 1), 0) >= LEAD
        for j, ref in enumerate((m0_ref, m1_ref, mbc_ref)):
            xh_ref[8:8 + CHUNK, j * SLAB:(j + 1) * SLAB] = jnp.where(valid, ref[0].astype(BF16).astype(F32), 0.0)

    @pl.when(c > 0)
    def _():
        for j, ref in enumerate((x0_ref, x1_ref, bc_ref)):
            xh_ref[8:8 + CHUNK, j * SLAB:(j + 1) * SLAB] = ref[0].astype(F32)

    lo = jnp.where(c == 0, LEAD, 0)
    dt_raw = jnp.where(c == 0, msmall_ref[:, 0:128], small_ref[:, 0:128])
    _ssd_chunk(xh_ref, dt_raw, state_ref, apad_ref, y_ref, lo, CHUNK, p, True)
    z = jnp.concatenate([z0_ref[0], z1_ref[0]], axis=1).astype(F32)
    yn_ref[...] = _gate_norm(y_ref[...], z, p["norm_g"][...], d_inner).astype(yn_ref.dtype)
    xh_ref[0:8, :] = xh_ref[CHUNK:CHUNK + 8, :]

    @pl.when(c == nc - 1)
    def _():
        convnew_ref[0] = xh_ref[pl.ds(8 + CHUNK - (CONV_K - 1), CONV_K - 1), :]
        for j in range(d_inner // 128):
            hfin_ref[0, j * 128:(j + 1) * 128, :] = state_ref[:, j * 128:(j + 1) * 128].T


def _ssd_prompt(proj3, small, proj3m, smallm, meta_tile, params, bp, nc):
    d_inner = params["expand"].shape[1]
    conv_dim = params["conv_b"].shape[1]
    m = proj3.shape[1]
    ns = small.shape[1]
    ncx = nc - 1
    plist = [params[k] for k in _SSD_PARAM_NAMES]

    def xblk(b, c):
        return b * ncx + jnp.maximum(c - 1, 0)

    def slab(s):
        return pl.BlockSpec((1, CHUNK, SLAB), lambda b, c, s=s: (s, xblk(b, c), 0))

    def mslab(s):
        return pl.BlockSpec((1, CHUNK, SLAB), lambda b, c, s=s: (s, meta_tile, 0))

    return pl.pallas_call(
        functools.partial(_ssd_prompt_kernel, nc=nc),
        out_shape=(jax.ShapeDtypeStruct((m, d_inner), BF16),
                   jax.ShapeDtypeStruct((bp, d_inner, D_STATE), F32),
                   jax.ShapeDtypeStruct((bp, CONV_K - 1, conv_dim), F32)),
        grid=(bp, nc),
        in_specs=[slab(2), slab(3), slab(4), slab(0), slab(1),
                  pl.BlockSpec((CHUNK, ns), lambda b, c: (xblk(b, c), 0)),
                  mslab(2), mslab(3), mslab(4), pl.BlockSpec((CHUNK, ns), lambda b, c: (meta_tile, 0))]
                 + [_resident(a.shape) for a in plist],
        out_specs=(pl.BlockSpec((CHUNK, d_inner), lambda b, c: (xblk(b, c), 0)),
                   pl.BlockSpec((1, d_inner, D_STATE), lambda b, c: (b, 0, 0)),
                   pl.BlockSpec((1, CONV_K - 1, conv_dim), lambda b, c: (b, 0, 0))),
        scratch_shapes=[pltpu.VMEM((8 + CHUNK, conv_dim), F32), pltpu.VMEM((D_STATE, d_inner), F32),
                        pltpu.VMEM((CHUNK, 128), F32), pltpu.VMEM((CHUNK, d_inner), F32)],
        compiler_params=_cparams(("arbitrary", "arbitrary"), 40),
        name="ssd_prompt",
    )(proj3, proj3, proj3, proj3, proj3, small, proj3m, proj3m, proj3m, smallm, *plist)


def _ssd_sample_kernel(x0_ref, x1_ref, bc_ref, z0_ref, z1_ref, small_ref, conv0_ref, h0_ref, *rest, tdec):
    np_ = len(_SSD_PARAM_NAMES)
    p = dict(zip(_SSD_PARAM_NAMES, rest[:np_]))
    yn_ref, hnew_ref, convnew_ref, xh_ref, state_ref, apad_ref, y_ref, dt_ref = rest[np_:]
    d_inner = state_ref.shape[1]
    nk = CONV_K - 1
    xh_ref[...] = jnp.zeros(xh_ref.shape, F32)
    xh_ref[8 - nk:8, :] = conv0_ref[0]
    xh_ref[8:8 + tdec, 0:SLAB] = x0_ref[0]
    xh_ref[8:8 + tdec, SLAB:2 * SLAB] = x1_ref[0]
    xh_ref[8:8 + tdec, 2 * SLAB:3 * SLAB] = bc_ref[0]
    dt_ref[...] = jnp.zeros(dt_ref.shape, F32)
    dt_ref[0:tdec, :] = small_ref[:, 0:128]
    for j in range(d_inner // 128):
        state_ref[:, j * 128:(j + 1) * 128] = h0_ref[0, j * 128:(j + 1) * 128, :].T
    _ssd_chunk(xh_ref, dt_ref[...], state_ref, apad_ref, y_ref, 0, tdec, p, False)
    z = jnp.concatenate([z0_ref[0], z1_ref[0]], axis=1)
    yn_ref[...] = _gate_norm(y_ref[0:tdec, :], z, p["norm_g"][...], d_inner)
    convnew_ref[0] = xh_ref[pl.ds(8 + tdec - nk, nk), :]
    for j in range(d_inner // 128):
        hnew_ref[0, j * 128:(j + 1) * 128, :] = state_ref[:, j * 128:(j + 1) * 128].T


def _ssd_sample(proj3, small, conv0, h0, params, bs, tdec):
    d_inner = params["expand"].shape[1]
    conv_dim = params["conv_b"].shape[1]
    ns = small.shape[1]
    plist = [params[k] for k in _SSD_PARAM_NAMES]

    def slab(s):
        return pl.BlockSpec((1, tdec, SLAB), lambda b, s=s: (s, b, 0))

    return pl.pallas_call(
        functools.partial(_ssd_sample_kernel, tdec=tdec),
        out_shape=(jax.ShapeDtypeStruct((bs * tdec, d_inner), F32),
                   jax.ShapeDtypeStruct((bs, d_inner, D_STATE), F32),
                   jax.ShapeDtypeStruct((bs, CONV_K - 1, conv_dim), F32)),
        grid=(bs,),
        in_specs=[slab(2), slab(3), slab(4), slab(0), slab(1), pl.BlockSpec((tdec, ns), lambda b: (b, 0)),
                  pl.BlockSpec((1, CONV_K - 1, conv_dim), lambda b: (b, 0, 0)),
                  pl.BlockSpec((1, d_inner, D_STATE), lambda b: (b, 0, 0))] + [_resident(a.shape) for a in plist],
        out_specs=(pl.BlockSpec((tdec, d_inner), lambda b: (b, 0)),
                   pl.BlockSpec((1, d_inner, D_STATE), lambda b: (b, 0, 0)),
                   pl.BlockSpec((1, CONV_K - 1, conv_dim), lambda b: (b, 0, 0))),
        scratch_shapes=[pltpu.VMEM((8 + CHUNK, conv_dim), F32), pltpu.VMEM((D_STATE, d_inner), F32),
                        pltpu.VMEM((CHUNK, 128), F32), pltpu.VMEM((CHUNK, d_inner), F32),
                        pltpu.VMEM((CHUNK, 128), F32)],
        compiler_params=_cparams(("arbitrary",), 40),
        name="ssd_sample",
    )(proj3, proj3, proj3, proj3, proj3, small, conv0, h0, *plist)


def _attn_out(acc, l, wuv_ref, nh, tq):
    o_lat = acc / l
    outs = []
    for p in range(nh // 2):
        pair = jnp.concatenate([o_lat[(2 * p) * tq:(2 * p + 1) * tq], o_lat[(2 * p + 1) * tq:(2 * p + 2) * tq]],
                               axis=1).astype(BF16)
        outs.append(_dot(pair, wuv_ref[p]))
    return jnp.concatenate(outs, axis=1)


def _attn_prompt_kernel(q_ref, k_ref, vt_ref, wuvt_ref, o_ref, m_ref, l_ref, acc_ref, p_ref, *, tk, lp, q_off):
    qi = pl.program_id(1) + q_off
    nh, tq = q_ref.shape[1], q_ref.shape[2]
    hpg = q_ref.shape[1] // m_ref.shape[0]
    ngrp = nh // hpg
    gq = hpg * tq
    m_ref[...] = jnp.full(m_ref.shape, NEG, F32)
    l_ref[...] = jnp.zeros(l_ref.shape, F32)
    acc_ref[...] = jnp.zeros(acc_ref.shape, F32)
    qidx = qi * tq + (lax.broadcasted_iota(jnp.int32, (1, gq), 1) & (tq - 1))
    nkc = ((qi + 1) * tq + tk - 1) // tk

    def chunk_start(kc):
        return pl.multiple_of(jnp.minimum(kc * tk, lp - tk), 128)

    def stage(kc, scores, values, diagonal):
        if scores:
            k = k_ref[pl.ds(chunk_start(kc), tk), :]
            kidx = chunk_start(kc) + lax.broadcasted_iota(jnp.int32, (tk, 1), 0)
        if values:
            vt = vt_ref[0, :, pl.ds(chunk_start(kc - 1), tk)]
        for r in range(ngrp):
            acc = acc_ref[r]
            if values:
                acc = acc + _dot(vt, p_ref[r])
            if scores:
                q = q_ref[0, hpg * r:hpg * (r + 1)].reshape(gq, q_ref.shape[3])
                s = _dot_nt(k, q)
                if diagonal:
                    s = jnp.where((kidx <= qidx) & (kidx >= jnp.maximum(kc * tk, LEAD)), s, NEG)
                else:
                    s = jnp.concatenate([jnp.where(kidx[0:CHUNK] >= LEAD, s[0:CHUNK], NEG), s[CHUNK:]], axis=0)
                m_old = m_ref[r:r + 1, :]
                m_new = jnp.maximum(m_old, jnp.max(s, axis=0, keepdims=True))
                pexp = jnp.exp2(s - m_new)
                alpha = jnp.exp2(m_old - m_new)
                l_ref[r:r + 1, :] = alpha * l_ref[r:r + 1, :] + jnp.sum(pexp, axis=0, keepdims=True)
                p_ref[r] = pexp.astype(BF16)
                m_ref[r:r + 1, :] = m_new
                acc = alpha * acc
            acc_ref[r] = acc

    def body(kc, carry):
        stage(kc, True, True, False)
        return carry

    @pl.when(nkc == 1)
    def _():
        stage(0, True, False, True)

    @pl.when(nkc > 1)
    def _():
        stage(0, True, False, False)
        lax.fori_loop(1, nkc - 1, body, 0)
        stage(nkc - 1, True, True, True)

    stage(nkc, False, True, False)
    for r in range(ngrp):
        o_lat_t = (acc_ref[r] / l_ref[r:r + 1, :]).astype(BF16)
        for j in range(hpg // 2):
            h0 = hpg * r + 2 * j
            pair = jnp.concatenate([_dot(wuvt_ref[h0 + e], o_lat_t[:, (2 * j + e) * tq:(2 * j + e + 1) * tq])
                                    for e in range(2)], axis=0)
            o_ref[:, h0 * V_HEAD:(h0 + 2) * V_HEAD] = pair.T.astype(o_ref.dtype)


def _attn_prompt(q4, kcat, vt, wuvt, bp, lp):
    q_off = 1
    nq = lp // CHUNK - q_off
    nh = MLA_HEADS
    kv_lora = vt.shape[1]
    tk = 512
    hpg = 4
    assert lp % 128 == 0 and lp >= tk and 2 * V_HEAD == 128 and nh % hpg == 0
    return pl.pallas_call(
        functools.partial(_attn_prompt_kernel, tk=tk, lp=lp, q_off=q_off),
        out_shape=jax.ShapeDtypeStruct((bp * nq * CHUNK, nh * V_HEAD), BF16),
        grid=(bp, nq),
        in_specs=[pl.BlockSpec((1, nh, CHUNK, QK_DIM), lambda b, i: (b * nq + i, 0, 0, 0)),
                  pl.BlockSpec((lp, QK_DIM), lambda b, i: (b, 0)),
                  pl.BlockSpec((1, kv_lora, lp), lambda b, i: (b, 0, 0)),
                  _resident(wuvt.shape)],
        out_specs=pl.BlockSpec((CHUNK, nh * V_HEAD), lambda b, i: (b * nq + i, 0)),
        scratch_shapes=[pltpu.VMEM((nh // hpg, hpg * CHUNK), F32), pltpu.VMEM((nh // hpg, hpg * CHUNK), F32),
                        pltpu.VMEM((nh // hpg, kv_lora, hpg * CHUNK), F32),
                        pltpu.VMEM((nh // hpg, tk, hpg * CHUNK), BF16)],
        compiler_params=_cparams(("arbitrary", "arbitrary"), 40),
        name="attn_prompt",
    )(q4, kcat, vt, wuvt)


def _attn_sample_kernel(pt_ref, q_ref, knew_ref, vnew_ref, foldt_ref, wuv_ref, ckv_hbm, krt_hbm, o_ref,
                        cbuf, rbuf, sem, m_ref, l_ref, acc_ref, q_scr, qlt_scr, qpt_scr, *, npg, nchunk, tdec):
    b = pl.program_id(0)
    c = pl.program_id(1)
    nb = pl.num_programs(0)
    step = b * nchunk + c
    slot = step % 2
    nh = q_ref.shape[1]
    rows = nh * tdec
    kv_lora = cbuf.shape[-1]
    page = cbuf.shape[2]

    def copies(bb, cc, sl):
        out = []
        for j in range(npg):
            pg = pt_ref[bb, cc * npg + j]
            out.append(pltpu.make_async_copy(ckv_hbm.at[pg], cbuf.at[sl, j], sem.at[0, sl]))
            out.append(pltpu.make_async_copy(krt_hbm.at[pg], rbuf.at[sl, :, pl.ds(j * page, page)], sem.at[1, sl]))
        return out

    @pl.when(step == 0)
    def _():
        for cp in copies(b, c, slot):
            cp.start()

    @pl.when(step + 1 < nb * nchunk)
    def _():
        nxt = step + 1
        for cp in copies(nxt // nchunk, nxt % nchunk, 1 - slot):
            cp.start()

    @pl.when(c == 0)
    def _():
        m_ref[...] = jnp.full(m_ref.shape, NEG, F32)
        l_ref[...] = jnp.zeros(l_ref.shape, F32)
        acc_ref[...] = jnp.zeros(acc_ref.shape, F32)
        q = q_ref[0].reshape(rows, q_ref.shape[3])
        q_scr[...] = q.astype(BF16)
        for j in range(kv_lora // 128):
            qlt_scr[j * 128:(j + 1) * 128, :] = q[:, j * 128:(j + 1) * 128].T.astype(BF16)
        qpt_scr[...] = lax.dot_general(foldt_ref[...], q[:, kv_lora:kv_lora + 128], (((1,), (1,)), ((), ())),
                                       preferred_element_type=F32, precision=HIGHEST).astype(BF16)

    for cp in copies(b, c, slot):
        cp.wait()

    def lanes_to_rows(x):
        return jnp.broadcast_to(x, (rows, rows)).T

    def widen(x):
        return jnp.concatenate([lanes_to_rows(x)] * (kv_lora // 128), axis=1)

    def update(st, s_t, v):
        m_old = m_ref[st:st + 1, :]
        m_new = jnp.maximum(m_old, jnp.max(s_t, axis=0, keepdims=True))
        p_t = jnp.exp2(s_t - m_new)
        alpha = jnp.exp2(m_old - m_new)
        l_ref[st:st + 1, :] = alpha * l_ref[st:st + 1, :] + jnp.sum(p_t, axis=0, keepdims=True)
        m_ref[st:st + 1, :] = m_new
        nk = s_t.shape[0]
        if nk % 128:
            pmat = p_t.T
        else:
            pmat = jnp.concatenate([p_t[j * 128:(j + 1) * 128, :].T for j in range(nk // 128)], axis=1)
        acc_ref[st] = acc_ref[st] * widen(alpha) + _dot(pmat.astype(BF16), v)

    nstream = m_ref.shape[0]
    pps = npg // nstream
    for st in range(nstream):
        kc = cbuf[slot, st * pps:(st + 1) * pps].reshape(pps * page, kv_lora).astype(BF16)
        kr = jnp.concatenate([rbuf[slot, :, j * page:(j + 1) * page].T for j in range(st * pps, (st + 1) * pps)],
                             axis=0).astype(BF16)
        update(st, _dot(kc, qlt_scr[...]) + _dot(kr, qpt_scr[...]), kc)

    @pl.when(c == nchunk - 1)
    def _():
        knew = knew_ref[...].astype(BF16)
        s_t = _dot_nt(knew, q_scr[...])
        nk = knew.shape[0]
        kid = lax.broadcasted_iota(jnp.int32, (nk, 1), 0)
        t = lax.broadcasted_iota(jnp.int32, (1, rows), 1) & (tdec - 1)
        ok = (kid >= b * tdec) & (kid <= b * tdec + t)
        update(0, jnp.where(ok, s_t, NEG), vnew_ref[...].astype(BF16))
        m_all = jnp.max(m_ref[...], axis=0, keepdims=True)
        l_all = jnp.zeros((1, rows), F32)
        acc = jnp.zeros((rows, kv_lora), F32)
        for st in range(nstream):
            w_st = jnp.exp2(m_ref[st:st + 1, :] - m_all)
            l_all = l_all + w_st * l_ref[st:st + 1, :]
            acc = acc + widen(w_st) * acc_ref[st]
        o_ref[...] = _attn_out(acc, widen(l_all), wuv_ref, nh, tdec)


def _attn_sample(page_table, q4, kcat_new, v_new, fold, wuv, cache_ckv, cache_krt, bs, tdec):
    nh = MLA_HEADS
    n_pages = page_table.shape[1]
    page, kv_lora = cache_ckv.shape[1], cache_ckv.shape[2]
    rope = cache_krt.shape[1]
    npg = 16
    while n_pages % npg:
        npg //= 2
    nchunk = n_pages // npg
    nstream = 1
    rows = nh * tdec
    assert tdec & (tdec - 1) == 0 and rows == 128 and page == 128
    grid_spec = pltpu.PrefetchScalarGridSpec(
        num_scalar_prefetch=1,
        grid=(bs, nchunk),
        in_specs=[pl.BlockSpec((1, nh, tdec, QK_DIM), lambda b, c, pt: (b, 0, 0, 0)),
                  pl.BlockSpec(kcat_new.shape, lambda b, c, pt: (0, 0), pipeline_mode=pl.Buffered(1)),
                  pl.BlockSpec(v_new.shape, lambda b, c, pt: (0, 0), pipeline_mode=pl.Buffered(1)),
                  pl.BlockSpec(fold.shape, lambda b, c, pt: (0, 0), pipeline_mode=pl.Buffered(1)),
                  pl.BlockSpec(wuv.shape, lambda b, c, pt: (0, 0, 0), pipeline_mode=pl.Buffered(1)),
                  pl.BlockSpec(memory_space=pl.ANY), pl.BlockSpec(memory_space=pl.ANY)],
        out_specs=pl.BlockSpec((tdec, nh * V_HEAD), lambda b, c, pt: (b, 0)),
        scratch_shapes=[pltpu.VMEM((2, npg, page, kv_lora), F32), pltpu.VMEM((2, rope, npg * page), F32),
                        pltpu.SemaphoreType.DMA((2, 2)),
                        pltpu.VMEM((nstream, rows), F32), pltpu.VMEM((nstream, rows), F32),
                        pltpu.VMEM((nstream, rows, kv_lora), F32),
                        pltpu.VMEM((rows, QK_DIM), BF16), pltpu.VMEM((kv_lora, rows), BF16),
                        pltpu.VMEM((rope, rows), BF16)],
    )
    return pl.pallas_call(
        functools.partial(_attn_sample_kernel, npg=npg, nchunk=nchunk, tdec=tdec),
        out_shape=jax.ShapeDtypeStruct((bs * tdec, nh * V_HEAD), F32),
        grid_spec=grid_spec,
        compiler_params=_cparams(("arbitrary", "arbitrary"), 40),
        name="attn_sample",
    )(page_table, q4, kcat_new, v_new, fold, wuv, cache_ckv, cache_krt)


def _merge_kernel(yn_ref, o_ref, ga_ref, gb_ref, h_ref, wa_ref, wb_ref, wo_ref, g_ref, out_ref):
    a = _dot(yn_ref[...].astype(BF16), wa_ref[...])
    bb = _dot(o_ref[...].astype(BF16), wb_ref[...])
    merged = jax.nn.sigmoid(ga_ref[0].astype(F32)) * a + jax.nn.sigmoid(gb_ref[0].astype(F32)) * bb
    mix = _dot(merged.astype(BF16), wo_ref[...])
    out_ref[...] = h_ref[...] + _rms(mix, g_ref[...])


def _merge(yn, o, proj3, h, wa, wb, wo, g):
    m, d = yn.shape[0], h.shape[1]
    tm = _row_tile(m, 512)
    return pl.pallas_call(
        _merge_kernel,
        out_shape=jax.ShapeDtypeStruct((m, d), F32),
        grid=(m // tm,),
        in_specs=[pl.BlockSpec((tm, yn.shape[1]), lambda i: (i, 0)), pl.BlockSpec((tm, o.shape[1]), lambda i: (i, 0)),
                  pl.BlockSpec((1, tm, SLAB), lambda i: (5, i, 0)), pl.BlockSpec((1, tm, SLAB), lambda i: (6, i, 0)),
                  pl.BlockSpec((tm, d), lambda i: (i, 0)),
                  _resident(wa.shape), _resident(wb.shape), _resident(wo.shape), _resident((1, d))],
        out_specs=pl.BlockSpec((tm, d), lambda i: (i, 0)),
        compiler_params=_cparams(("arbitrary",), 40),
        name="gated_merge",
    )(yn, o, proj3, proj3, h, wa, wb, wo, g)


def _prep_weights(w, d_inner, conv_dim, nheads, q_lora, kv_lora):
    d = w["w_in"].shape[0]
    offs = np.cumsum([0, d_inner, conv_dim, nheads, q_lora, kv_lora, QK_ROPE, d, d])
    seg = {k: w["w_in"][:, offs[i]:offs[i + 1]]
           for i, k in enumerate(("z", "xbc", "dt", "q_a", "kv_a", "k_pe", "ga", "gb"))}
    pad = SLAB - q_lora - kv_lora
    wbig = jnp.concatenate([seg["z"], seg["xbc"], seg["ga"], seg["gb"], seg["q_a"], seg["kv_a"],
                            jnp.zeros((d, pad), F32)], axis=1).astype(BF16)
    swap = np.concatenate([np.arange(QK_ROPE // 2, QK_ROPE), np.arange(QK_ROPE // 2)])
    wsmall = jnp.concatenate([seg["dt"], jnp.zeros((d, 128 - nheads), F32), jnp.tile(seg["k_pe"], (1, 4)),
                              jnp.tile(seg["k_pe"][:, swap], (1, 4))], axis=1).astype(BF16)
    nh = MLA_HEADS
    wq3 = w["w_q_b"].reshape(q_lora, nh, QK_NOPE + QK_ROPE)
    wq = jnp.concatenate([wq3[:, :, :QK_NOPE].reshape(q_lora, nh * QK_NOPE),
                          wq3[:, :, QK_NOPE:].reshape(q_lora, nh * QK_ROPE),
                          wq3[:, :, QK_NOPE:][:, :, swap].reshape(q_lora, nh * QK_ROPE)], axis=1).astype(BF16)
    wk = jnp.transpose(w["w_uk"], (1, 2, 0)).reshape(nh // 2, 2, QK_NOPE, kv_lora)
    zk = jnp.zeros((nh // 2, QK_NOPE, kv_lora), F32)
    wuk = jnp.concatenate([jnp.concatenate([wk[:, 0], zk], axis=2),
                           jnp.concatenate([zk, wk[:, 1]], axis=2)], axis=1).astype(BF16)
    wv = jnp.transpose(w["w_uv"], (1, 0, 2)).reshape(nh // 2, 2, kv_lora, V_HEAD)
    zv = jnp.zeros((nh // 2, kv_lora, V_HEAD), F32)
    wuvt = jnp.transpose(w["w_uv"], (1, 2, 0)).astype(BF16)
    wuv = jnp.concatenate([jnp.concatenate([wv[:, 0], zv], axis=2),
                           jnp.concatenate([zv, wv[:, 1]], axis=2)], axis=1).astype(BF16)
    def expand3(width):
        e = np.kron(np.eye(128, nheads, dtype=np.float32), np.ones((1, width), np.float32))
        return jnp.asarray(np.tile(e, (3, 1)), BF16)

    lane_pad = lambda a: jnp.pad(a, (0, 128 - nheads))[None]
    ssd = dict(conv_w=w["conv_w"], conv_b=w["conv_b"][None], dt_bias=lane_pad(w["dt_bias"]),
               a_log=lane_pad(w["a_log"]),
               dskip_x=jnp.repeat(w["d_skip"], SSM_HEAD_DIM)[None], norm_g=w["ssm_norm_g"][None],
               tril=jnp.asarray(np.tril(np.ones((CHUNK, CHUNK), np.float32))),
               expand=expand3(SSM_HEAD_DIM), expand128=expand3(128),
               shift=jnp.asarray(np.concatenate([np.eye(CHUNK, k=k - (CONV_K - 1), dtype=np.float32)
                                                 for k in range(CONV_K - 1)], axis=0), BF16))
    inv = ROPE_THETA ** (-jnp.arange(0, QK_ROPE, 2, dtype=F32) / QK_ROPE)
    inv128 = jnp.tile(jnp.concatenate([inv, inv]), 4)[None]
    sign128 = jnp.asarray(np.tile(np.concatenate([-np.ones(QK_ROPE // 2), np.ones(QK_ROPE // 2)]), 4)[None], F32)
    foldt = jnp.asarray(np.tile(np.eye(QK_ROPE, dtype=np.float32), (1, 4)))
    bf = lambda a: a.astype(BF16)
    row = lambda a: a[None]
    return dict(
        wbig=wbig, wsmall=wsmall, wq=wq, wuk=wuk, wuv=wuv, wuvt=wuvt, ssd=ssd, inv128=inv128, sign128=sign128, foldt=foldt,
        ffn1=(row(w["ffn1_pre_g"]), bf(w["ffn1_w_gate"]), bf(w["ffn1_w_up"]), bf(w["ffn1_w_down"]),
              row(w["ffn1_post_g"])),
        ffn2=(row(w["ffn2_pre_g"]), bf(w["ffn2_w_gate"]), bf(w["ffn2_w_up"]), bf(w["ffn2_w_down"]),
              row(w["ffn2_post_g"])),
        mix_pre_g=row(w["mix_pre_g"]), mix_post_g=row(w["mix_post_g"]), q_g=row(w["q_a_norm_g"]),
        kv_g=row(w["kv_a_norm_g"]), wa=bf(w["w_a_out"]), wb=bf(w["w_b_out"]), wo=bf(w["w_o"]))


def kernel(x_prompt, x_sample, cache_kv_latent, cache_k_rope, state_ssm, state_conv, page_table, meta_tokens,
           ffn1_pre_g, ffn1_w_gate, ffn1_w_up, ffn1_w_down, ffn1_post_g,
           mix_pre_g, w_in, conv_w, conv_b, dt_bias, a_log, d_skip, ssm_norm_g,
           q_a_norm_g, w_q_b, kv_a_norm_g, w_uk, w_uv, w_a_out, w_b_out, w_o, mix_post_g,
           ffn2_pre_g, ffn2_w_gate, ffn2_w_up, ffn2_w_down, ffn2_post_g):
    names = ("ffn1_pre_g", "ffn1_w_gate", "ffn1_w_up", "ffn1_w_down", "ffn1_post_g", "mix_pre_g", "w_in", "conv_w",
             "conv_b", "dt_bias", "a_log", "d_skip", "ssm_norm_g", "q_a_norm_g", "w_q_b", "kv_a_norm_g", "w_uk",
             "w_uv", "w_a_out", "w_b_out", "w_o", "mix_post_g", "ffn2_pre_g", "ffn2_w_gate", "ffn2_w_up",
             "ffn2_w_down", "ffn2_post_g")
    stacked = dict(zip(names, (ffn1_pre_g, ffn1_w_gate, ffn1_w_up, ffn1_w_down, ffn1_post_g, mix_pre_g, w_in,
                               conv_w, conv_b, dt_bias, a_log, d_skip, ssm_norm_g, q_a_norm_g, w_q_b, kv_a_norm_g,
                               w_uk, w_uv, w_a_out, w_b_out, w_o, mix_post_g, ffn2_pre_g, ffn2_w_gate, ffn2_w_up,
                               ffn2_w_down, ffn2_post_g)))
    depth = w_in.shape[0]
    bp, seq, d = x_prompt.shape
    bs, tdec, _ = x_sample.shape
    nheads = dt_bias.shape[1]
    d_inner = nheads * SSM_HEAD_DIM
    conv_dim = conv_b.shape[1]
    q_lora, kv_lora = q_a_norm_g.shape[1], kv_a_norm_g.shape[1]
    assert seq % CHUNK == 0 and conv_dim == 3 * SLAB and d_inner == 2 * SLAB and d == SLAB
    assert q_lora + kv_lora <= SLAB and CHUNK % tdec == 0
    n_pages, page = page_table.shape[1], cache_kv_latent.shape[2]
    past_len = n_pages * page
    lp = LEAD + N_META + seq
    nc = lp // CHUNK
    ns_rows = bs * tdec
    assert depth == 1 and ns_rows % CHUNK == 0
    meta_tile = ns_rows // CHUNK
    meta0 = ns_rows + LEAD

    hx = x_prompt.reshape(bp * seq, d)
    hs = jnp.concatenate([x_sample.reshape(ns_rows, d), jnp.zeros((LEAD, d), F32), meta_tokens], axis=0)
    pos_x = jnp.tile(N_META + jnp.arange(seq, dtype=jnp.int32), bp).astype(F32)[:, None]
    pos_s = jnp.concatenate([jnp.tile(past_len + jnp.arange(tdec, dtype=jnp.int32), bs),
                             jnp.arange(CHUNK, dtype=jnp.int32) - LEAD]).astype(F32)[:, None]

    outs = [[] for _ in range(8)]
    for l in range(depth):
        w = _prep_weights({k: v[l] for k, v in stacked.items()}, d_inner, conv_dim, nheads, q_lora, kv_lora)

        g1 = _ffn(hs, *w["ffn1"])
        proj3s, smalls = _inproj(g1, w["mix_pre_g"], w["wbig"], w["wsmall"], F32)
        c_kv_s, kpe_s, kcat_s, q4s = _qkv_prep(proj3s, smalls, pos_s, w["q_g"], w["kv_g"], w["wq"], w["wuk"],
                                               w["inv128"], w["sign128"], tdec, F32)

        h1 = _ffn(hx, *w["ffn1"])
        proj3, small = _inproj(h1, w["mix_pre_g"], w["wbig"], w["wsmall"], BF16)
        yn, h_fin, conv_new = _ssd_prompt(proj3, small, proj3s, smalls, meta_tile, w["ssd"], bp, nc)
        c_kv, kpe, kcat, q4 = _qkv_prep(proj3, small, pos_x, w["q_g"], w["kv_g"], w["wq"], w["wuk"], w["inv128"],
                                        w["sign128"], CHUNK, BF16)

        def with_meta(tile, rows):
            f = rows.shape[1]
            return jnp.concatenate([jnp.broadcast_to(tile.astype(rows.dtype)[None], (bp,) + tile.shape),
                                    rows.reshape(bp, seq, f)], axis=1)

        k_all = with_meta(kcat_s[ns_rows:], kcat).reshape(bp * lp, QK_DIM)
        vt = jnp.transpose(with_meta(c_kv_s[ns_rows:].astype(BF16), c_kv.astype(BF16)), (0, 2, 1))
        o = _attn_prompt(q4, k_all, vt, w["wuvt"], bp, lp)
        h2 = _merge(yn, o, proj3, h1, w["wa"], w["wb"], w["wo"], w["mix_post_g"])
        hx = _ffn(h2, *w["ffn2"])
        outs[0].append(with_meta(c_kv_s[meta0:], c_kv))
        outs[1].append(with_meta(kpe_s[meta0:, :QK_ROPE], kpe[:, :QK_ROPE]))
        outs[2].append(h_fin.reshape(bp, nheads, SSM_HEAD_DIM, D_STATE))
        outs[3].append(conv_new)

        yns, h_new, conv_new_s = _ssd_sample(proj3s, smalls, state_conv[l],
                                             state_ssm[l].reshape(bs, d_inner, D_STATE), w["ssd"], bs, tdec)
        os_ = _attn_sample(page_table, q4s, kcat_s[:ns_rows], c_kv_s[:ns_rows], w["foldt"], w["wuv"],
                           cache_kv_latent[l], jnp.transpose(cache_k_rope[l], (0, 2, 1)), bs, tdec)
        g2 = _merge(yns, os_, proj3s, g1, w["wa"], w["wb"], w["wo"], w["mix_post_g"])
        hs = _ffn(g2, *w["ffn2"])
        outs[4].append(c_kv_s[:ns_rows].reshape(bs, tdec, kv_lora))
        outs[5].append(kpe_s[:ns_rows, :QK_ROPE].reshape(bs, tdec, QK_ROPE))
        outs[6].append(h_new.reshape(bs, nheads, SSM_HEAD_DIM, D_STATE))
        outs[7].append(conv_new_s)

    y_prompt = hx.reshape(bp, seq, d)
    y_sample = hs.reshape(bs, tdec, d)
    return (y_prompt, y_sample) + tuple(jnp.stack(o) for o in outs)
```

```python
import functools
import math

import numpy as np
import jax
import jax.numpy as jnp
from jax import lax
from jax.experimental import pallas as pl
from jax.experimental.pallas import tpu as pltpu

F32 = jnp.float32
BF16 = jnp.bfloat16
HIGHEST = lax.Precision.HIGHEST

EPS = 1e-6
N_META = 16
CHUNK = 128
LEAD = (-N_META) % CHUNK
SSM_HEAD_DIM = 64
SSM_GROUPS = 4
D_STATE = 128
CONV_K = 4
MLA_HEADS = 16
QK_NOPE = 64
QK_ROPE = 32
V_HEAD = 64
ROPE_THETA = 10000.0
ATTN_SCALE = (QK_NOPE + QK_ROPE) ** -0.5
Q_SCALE = ATTN_SCALE * math.log2(math.e)
NEG = -1e30
SLAB = 1024
QK_DIM = 384
MIB = 1024 * 1024


def _cparams(sem, vmem_mib):
    return pltpu.CompilerParams(dimension_semantics=sem, vmem_limit_bytes=int(vmem_mib * MIB))


def _resident(shape):
    nd = len(shape)
    return pl.BlockSpec(shape, lambda *_: (0,) * nd, pipeline_mode=pl.Buffered(1))


def _rms(x, g):
    return x * lax.rsqrt(jnp.mean(x * x, axis=-1, keepdims=True) + EPS) * g


def _dot(a, b):
    return jnp.dot(a, b, preferred_element_type=F32)


def _dot_exact(a, b):
    return jnp.dot(a, b, preferred_element_type=F32, precision=HIGHEST)


def _dot_nt(a, b):
    return lax.dot_general(a, b, (((1,), (1,)), ((), ())), preferred_element_type=F32)


def _expand_exact(x, e3):
    hi = x.astype(BF16)
    r1 = x - hi.astype(F32)
    mid = r1.astype(BF16)
    lo = (r1 - mid.astype(F32)).astype(BF16)
    return _dot(jnp.concatenate([hi, mid, lo], axis=1), e3)


def _softplus(x):
    return jnp.maximum(x, 0.0) + jnp.log1p(jnp.exp(-jnp.abs(x)))


def _row_tile(m, cap, mult=16):
    for t in range(min(cap, m), mult - 1, -1):
        if m % t == 0 and t % mult == 0:
            return t
    raise ValueError(f"no row tile for {m}")


def _ffn_kernel(x_ref, pre_ref, wg_ref, wu_ref, wd_ref, post_ref, o_ref, *, chunks):
    x = x_ref[...]
    xn = _rms(x, pre_ref[...]).astype(BF16)
    acc = jnp.zeros(x.shape, F32)
    for lo, sz in chunks:
        g = _dot(xn, wg_ref[:, lo:lo + sz])
        u = _dot(xn, wu_ref[:, lo:lo + sz])
        h = (jax.nn.silu(g) * u).astype(BF16)
        acc = acc + _dot(h, wd_ref[lo:lo + sz, :])
    o_ref[...] = x + 0.5 * _rms(acc, post_ref[...])


def _ffn_chunks(f):
    out, lo = [], 0
    while lo < f:
        sz = min(1024, f - lo)
        out.append((lo, sz))
        lo += sz
    return tuple(out)


def _ffn(x, pre_g, wg, wu, wd, post_g):
    m, d = x.shape
    f = wg.shape[1]
    tm = _row_tile(m, 512)
    return pl.pallas_call(
        functools.partial(_ffn_kernel, chunks=_ffn_chunks(f)),
        out_shape=jax.ShapeDtypeStruct((m, d), F32),
        grid=(m // tm,),
        in_specs=[pl.BlockSpec((tm, d), lambda i: (i, 0)), _resident((1, d)), _resident((d, f)),
                  _resident((d, f)), _resident((f, d)), _resident((1, d))],
        out_specs=pl.BlockSpec((tm, d), lambda i: (i, 0)),
        compiler_params=_cparams(("arbitrary",), 52),
        name="ffn_block",
    )(x, pre_g, wg, wu, wd, post_g)


def _inproj_kernel(x_ref, g_ref, wbig_ref, wsmall_ref, proj_ref, small_ref, *, nslab):
    xn = _rms(x_ref[...], g_ref[...]).astype(BF16)
    for s in range(nslab):
        proj_ref[s] = _dot(xn, wbig_ref[:, s * SLAB:(s + 1) * SLAB]).astype(proj_ref.dtype)
    small_ref[...] = _dot(xn, wsmall_ref[...])


def _inproj(h, g, wbig, wsmall, proj_dtype):
    m, d = h.shape
    nslab = wbig.shape[1] // SLAB
    ns = wsmall.shape[1]
    tm = _row_tile(m, 512)
    return pl.pallas_call(
        functools.partial(_inproj_kernel, nslab=nslab),
        out_shape=(jax.ShapeDtypeStruct((nslab, m, SLAB), proj_dtype), jax.ShapeDtypeStruct((m, ns), F32)),
        grid=(m // tm,),
        in_specs=[pl.BlockSpec((tm, d), lambda i: (i, 0)), _resident((1, d)), _resident(wbig.shape),
                  _resident(wsmall.shape)],
        out_specs=(pl.BlockSpec((nslab, tm, SLAB), lambda i: (0, i, 0)), pl.BlockSpec((tm, ns), lambda i: (i, 0))),
        compiler_params=_cparams(("arbitrary",), 52),
        name="in_proj",
    )(h, g, wbig, wsmall)


def _qkv_kernel(s7_ref, small_ref, pos_ref, qg_ref, kvg_ref, wq_ref, wuk_ref, inv_ref, sign_ref,
                ckv_ref, kpe_ref, kcat_ref, q_ref, *, tqb):
    s7 = s7_ref[0].astype(F32)
    tm = s7.shape[0]
    q_lora = qg_ref.shape[1]
    kv_lora = kvg_ref.shape[1]
    nh = MLA_HEADS
    c_kv = _rms(s7[:, q_lora:q_lora + kv_lora], kvg_ref[...])
    ckv_ref[...] = c_kv
    ang = pos_ref[...] * inv_ref[...]
    cos = jnp.cos(ang)
    sin = jnp.sin(ang) * sign_ref[...]
    small = small_ref[...]
    kpe = small[:, 128:256] * cos + small[:, 256:384] * sin
    kpe_ref[...] = kpe
    kcat_ref[:, 0:kv_lora] = c_kv.astype(kcat_ref.dtype)
    kcat_ref[:, kv_lora:kv_lora + 128] = kpe.astype(kcat_ref.dtype)

    qn = _rms(s7[:, 0:q_lora], qg_ref[...] * Q_SCALE).astype(BF16)
    nope_w = nh * QK_NOPE
    pe_w = nh * QK_ROPE
    q_nope = _dot(qn, wq_ref[:, 0:nope_w])
    q_pe = _dot(qn, wq_ref[:, nope_w:nope_w + pe_w])
    q_rot = _dot(qn, wq_ref[:, nope_w + pe_w:nope_w + 2 * pe_w])
    cos_w = jnp.concatenate([cos] * (pe_w // 128), axis=1)
    sin_w = jnp.concatenate([sin] * (pe_w // 128), axis=1)
    q_pe = q_pe * cos_w + q_rot * sin_w
    lane = lax.broadcasted_iota(jnp.int32, (1, 128), 1)
    nblk = tm // tqb
    for p in range(nh // 2):
        qn_pair = q_nope[:, p * 128:(p + 1) * 128].astype(BF16)
        q_lat = _dot(qn_pair, wuk_ref[p])
        for e in range(2):
            hd = 2 * p + e
            pe_blk = q_pe[:, (hd // 4) * 128:(hd // 4 + 1) * 128]
            sel = (lane >= (hd % 4) * QK_ROPE) & (lane < (hd % 4 + 1) * QK_ROPE)
            pe_blk = jnp.where(sel, pe_blk, 0.0)
            lat = q_lat[:, e * kv_lora:(e + 1) * kv_lora]
            for g in range(nblk):
                q_ref[g, hd, :, 0:kv_lora] = lat[g * tqb:(g + 1) * tqb].astype(q_ref.dtype)
                q_ref[g, hd, :, kv_lora:kv_lora + 128] = pe_blk[g * tqb:(g + 1) * tqb].astype(q_ref.dtype)


def _qkv_prep(proj3, small, pos, qg, kvg, wq, wuk, inv128, sign128, tqb, dtype):
    nslab, m, _ = proj3.shape
    kv_lora = kvg.shape[1]
    tm = _row_tile(m, 512, max(16, tqb))
    nblk = tm // tqb
    return pl.pallas_call(
        functools.partial(_qkv_kernel, tqb=tqb),
        out_shape=(jax.ShapeDtypeStruct((m, kv_lora), F32), jax.ShapeDtypeStruct((m, 128), F32),
                   jax.ShapeDtypeStruct((m, QK_DIM), dtype),
                   jax.ShapeDtypeStruct((m // tqb, MLA_HEADS, tqb, QK_DIM), dtype)),
        grid=(m // tm,),
        in_specs=[pl.BlockSpec((1, tm, SLAB), lambda i: (nslab - 1, i, 0)),
                  pl.BlockSpec((tm, small.shape[1]), lambda i: (i, 0)),
                  pl.BlockSpec((tm, 1), lambda i: (i, 0)),
                  _resident(qg.shape), _resident(kvg.shape), _resident(wq.shape), _resident(wuk.shape),
                  _resident((1, 128)), _resident((1, 128))],
        out_specs=(pl.BlockSpec((tm, kv_lora), lambda i: (i, 0)), pl.BlockSpec((tm, 128), lambda i: (i, 0)),
                   pl.BlockSpec((tm, QK_DIM), lambda i: (i, 0)),
                   pl.BlockSpec((nblk, MLA_HEADS, tqb, QK_DIM), lambda i: (i, 0, 0, 0))),
        compiler_params=_cparams(("arbitrary",), 48),
        name="qkv_prep",
    )(proj3, small, pos, qg, kvg, wq, wuk, inv128, sign128)


def _ssd_chunk(xh_ref, dt_raw, state_ref, apad_ref, y_ref, lo, hi, p, bf16_input):
    d_inner = state_ref.shape[1]
    nheads = d_inner // SSM_HEAD_DIM
    gw = d_inner // SSM_GROUPS
    rows = lax.broadcasted_iota(jnp.int32, (CHUNK, 1), 0)
    valid = (rows >= lo) & (rows < hi)
    ntap = CONV_K - 1

    def conv_rows(nrows):
        out = p["conv_b"][...]
        for k in range(CONV_K):
            out = out + p["conv_w"][k:k + 1, :] * xh_ref[pl.ds(8 - ntap + k, nrows), :]
        return out

    if bf16_input:
        cur = xh_ref[8:8 + CHUNK, :]
        shifted = _dot(p["shift"][...], cur.astype(BF16))
        conv = p["conv_b"][...]
        for k in range(ntap):
            conv = conv + p["conv_w"][k:k + 1, :] * shifted[k * CHUNK:(k + 1) * CHUNK]
        conv = conv + p["conv_w"][ntap:CONV_K, :] * cur
        conv = jnp.concatenate([conv_rows(8), conv[8:]], axis=0)
    else:
        conv = conv_rows(CHUNK)
    xc = jnp.where(valid, jax.nn.silu(conv), 0.0)
    xs = xc[:, 0:d_inner]
    bm = xc[:, d_inner:d_inner + SSM_GROUPS * D_STATE]
    cm = xc[:, d_inner + SSM_GROUPS * D_STATE:]

    dt = jnp.where(valid, _softplus(dt_raw + p["dt_bias"][...]), 0.0)
    da = dt * (-jnp.exp(p["a_log"][...]))
    a_cs = _dot_exact(p["tril"][...], da) * math.log2(math.e)
    dt_x = _expand_exact(dt, p["expand"][...])
    acs_x = _expand_exact(a_cs, p["expand"][...])
    alast_x = acs_x[CHUNK - 1:CHUNK, :]
    acs_b = _expand_exact(a_cs, p["expand128"][...])
    apad_ref[...] = a_cs.T
    xdt = xs * dt_x
    xds = (xdt * jnp.exp2(alast_x - acs_x)).astype(BF16)
    exp_acs = jnp.exp2(acs_x)
    causal = rows >= lax.broadcasted_iota(jnp.int32, (1, CHUNK), 1)
    lane = lax.broadcasted_iota(jnp.int32, (1, 128), 1)
    hpg = nheads // SSM_GROUPS

    for g in range(SSM_GROUPS):
        gs = slice(g * gw, (g + 1) * gw)
        bg = bm[:, g * D_STATE:(g + 1) * D_STATE]
        cg = cm[:, g * D_STATE:(g + 1) * D_STATE].astype(BF16)
        cb = _dot_nt(cg, bg.astype(BF16))
        st = state_ref[:, gs]
        y_off = _dot(cg, st.astype(BF16)) * exp_acs[:, gs]
        state_ref[:, gs] = st * jnp.exp2(alast_x[:, gs]) + _dot(bg.T.astype(BF16), xds[:, gs])
        y_ref[:, gs] = y_off + xs[:, gs] * p["dskip_x"][:, gs]
        for pr in range(hpg // 2):
            w2 = []
            for e in range(2):
                hd = g * hpg + 2 * pr + e
                seg = acs_b[:, hd * 128:(hd + 1) * 128] - apad_ref[pl.ds(hd, 1), :]
                decay = jnp.exp2(jnp.where(causal, seg, NEG))
                w2.append((cb * decay).astype(BF16))
            ls = slice(g * gw + pr * 128, g * gw + (pr + 1) * 128)
            xp = xdt[:, ls]
            x2 = jnp.concatenate([jnp.where(lane < SSM_HEAD_DIM, xp, 0.0),
                                  jnp.where(lane >= SSM_HEAD_DIM, xp, 0.0)], axis=0).astype(BF16)
            y_ref[:, ls] = y_ref[:, ls] + _dot(jnp.concatenate(w2, axis=1), x2)


def _gate_norm(y, z, g, d_inner):
    yg = y * jax.nn.silu(z)
    gw = d_inner // SSM_GROUPS
    outs = []
    for k in range(SSM_GROUPS):
        v = yg[:, k * gw:(k + 1) * gw]
        outs.append(_rms(v, g[:, k * gw:(k + 1) * gw]))
    return jnp.concatenate(outs, axis=1)


_SSD_PARAM_NAMES = ("conv_w", "conv_b", "dt_bias", "a_log", "dskip_x", "norm_g", "tril", "expand", "expand128",
                    "shift")


def _ssd_prompt_kernel(x0_ref, x1_ref, bc_ref, z0_ref, z1_ref, small_ref, m0_ref, m1_ref, mbc_ref, msmall_ref,
                       *rest, nc):
    np_ = len(_SSD_PARAM_NAMES)
    p = dict(zip(_SSD_PARAM_NAMES, rest[:np_]))
    yn_ref, hfin_ref, convnew_ref, xh_ref, state_ref, apad_ref, y_ref = rest[np_:]
    c = pl.program_id(1)
    d_inner = state_ref.shape[1]
    assert x0_ref.dtype == BF16

    @pl.when(c == 0)
    def _():
        xh_ref[0:8, :] = jnp.zeros((8, xh_ref.shape[1]), F32)
        state_ref[...] = jnp.zeros(state_ref.shape, F32)
        valid = lax.broadcasted_iota(jnp.int32, (CHUNK, 1), 0) >= LEAD
        for j, ref in enumerate((m0_ref, m1_ref, mbc_ref)):
            xh_ref[8:8 + CHUNK, j * SLAB:(j + 1) * SLAB] = jnp.where(valid, ref[0].astype(BF16).astype(F32), 0.0)

    @pl.when(c > 0)
    def _():
        for j, ref in enumerate((x0_ref, x1_ref, bc_ref)):
            xh_ref[8:8 + CHUNK, j * SLAB:(j + 1) * SLAB] = ref[0].astype(F32)

    lo = jnp.where(c == 0, LEAD, 0)
    dt_raw = jnp.where(c == 0, msmall_ref[:, 0:128], small_ref[:, 0:128])
    _ssd_chunk(xh_ref, dt_raw, state_ref, apad_ref, y_ref, lo, CHUNK, p, True)
    z = jnp.concatenate([z0_ref[0], z1_ref[0]], axis=1).astype(F32)
    yn_ref[...] = _gate_norm(y_ref[...], z, p["norm_g"][...], d_inner).astype(yn_ref.dtype)
    xh_ref[0:8, :] = xh_ref[CHUNK:CHUNK + 8, :]

    @pl.when(c == nc - 1)
    def _():
        convnew_ref[0] = xh_ref[pl.ds(8 + CHUNK - (CONV_K - 1), CONV_K - 1), :]
        for j in range(d_inner // 128):
            hfin_ref[0, j * 128:(j + 1) * 128, :] = state_ref[:, j * 128:(j + 1) * 128].T


def _ssd_prompt(proj3, small, proj3m, smallm, meta_tile, params, bp, nc):
    d_inner = params["expand"].shape[1]
    conv_dim = params["conv_b"].shape[1]
    m = proj3.shape[1]
    ns = small.shape[1]
    ncx = nc - 1
    plist = [params[k] for k in _SSD_PARAM_NAMES]

    def xblk(b, c):
        return b * ncx + jnp.maximum(c - 1, 0)

    def slab(s):
        return pl.BlockSpec((1, CHUNK, SLAB), lambda b, c, s=s: (s, xblk(b, c), 0))

    def mslab(s):
        return pl.BlockSpec((1, CHUNK, SLAB), lambda b, c, s=s: (s, meta_tile, 0))

    return pl.pallas_call(
        functools.partial(_ssd_prompt_kernel, nc=nc),
        out_shape=(jax.ShapeDtypeStruct((m, d_inner), BF16),
                   jax.ShapeDtypeStruct((bp, d_inner, D_STATE), F32),
                   jax.ShapeDtypeStruct((bp, CONV_K - 1, conv_dim), F32)),
        grid=(bp, nc),
        in_specs=[slab(2), slab(3), slab(4), slab(0), slab(1),
                  pl.BlockSpec((CHUNK, ns), lambda b, c: (xblk(b, c), 0)),
                  mslab(2), mslab(3), mslab(4), pl.BlockSpec((CHUNK, ns), lambda b, c: (meta_tile, 0))]
                 + [_resident(a.shape) for a in plist],
        out_specs=(pl.BlockSpec((CHUNK, d_inner), lambda b, c: (xblk(b, c), 0)),
                   pl.BlockSpec((1, d_inner, D_STATE), lambda b, c: (b, 0, 0)),
                   pl.BlockSpec((1, CONV_K - 1, conv_dim), lambda b, c: (b, 0, 0))),
        scratch_shapes=[pltpu.VMEM((8 + CHUNK, conv_dim), F32), pltpu.VMEM((D_STATE, d_inner), F32),
                        pltpu.VMEM((CHUNK, 128), F32), pltpu.VMEM((CHUNK, d_inner), F32)],
        compiler_params=_cparams(("arbitrary", "arbitrary"), 40),
        name="ssd_prompt",
    )(proj3, proj3, proj3, proj3, proj3, small, proj3m, proj3m, proj3m, smallm, *plist)


def _ssd_sample_kernel(x0_ref, x1_ref, bc_ref, z0_ref, z1_ref, small_ref, conv0_ref, h0_ref, *rest, tdec):
    np_ = len(_SSD_PARAM_NAMES)
    p = dict(zip(_SSD_PARAM_NAMES, rest[:np_]))
    yn_ref, hnew_ref, convnew_ref, xh_ref, state_ref, apad_ref, y_ref, dt_ref = rest[np_:]
    d_inner = state_ref.shape[1]
    nk = CONV_K - 1
    xh_ref[...] = jnp.zeros(xh_ref.shape, F32)
    xh_ref[8 - nk:8, :] = conv0_ref[0]
    xh_ref[8:8 + tdec, 0:SLAB] = x0_ref[0]
    xh_ref[8:8 + tdec, SLAB:2 * SLAB] = x1_ref[0]
    xh_ref[8:8 + tdec, 2 * SLAB:3 * SLAB] = bc_ref[0]
    dt_ref[...] = jnp.zeros(dt_ref.shape, F32)
    dt_ref[0:tdec, :] = small_ref[:, 0:128]
    for j in range(d_inner // 128):
        state_ref[:, j * 128:(j + 1) * 128] = h0_ref[0, j * 128:(j + 1) * 128, :].T
    _ssd_chunk(xh_ref, dt_ref[...], state_ref, apad_ref, y_ref, 0, tdec, p, False)
    z = jnp.concatenate([z0_ref[0], z1_ref[0]], axis=1)
    yn_ref[...] = _gate_norm(y_ref[0:tdec, :], z, p["norm_g"][...], d_inner)
    convnew_ref[0] = xh_ref[pl.ds(8 + tdec - nk, nk), :]
    for j in range(d_inner // 128):
        hnew_ref[0, j * 128:(j + 1) * 128, :] = state_ref[:, j * 128:(j + 1) * 128].T


def _ssd_sample(proj3, small, conv0, h0, params, bs, tdec):
    d_inner = params["expand"].shape[1]
    conv_dim = params["conv_b"].shape[1]
    ns = small.shape[1]
    plist = [params[k] for k in _SSD_PARAM_NAMES]

    def slab(s):
        return pl.BlockSpec((1, tdec, SLAB), lambda b, s=s: (s, b, 0))

    return pl.pallas_call(
        functools.partial(_ssd_sample_kernel, tdec=tdec),
        out_shape=(jax.ShapeDtypeStruct((bs * tdec, d_inner), F32),
                   jax.ShapeDtypeStruct((bs, d_inner, D_STATE), F32),
                   jax.ShapeDtypeStruct((bs, CONV_K - 1, conv_dim), F32)),
        grid=(bs,),
        in_specs=[slab(2), slab(3), slab(4), slab(0), slab(1), pl.BlockSpec((tdec, ns), lambda b: (b, 0)),
                  pl.BlockSpec((1, CONV_K - 1, conv_dim), lambda b: (b, 0, 0)),
                  pl.BlockSpec((1, d_inner, D_STATE), lambda b: (b, 0, 0))] + [_resident(a.shape) for a in plist],
        out_specs=(pl.BlockSpec((tdec, d_inner), lambda b: (b, 0)),
                   pl.BlockSpec((1, d_inner, D_STATE), lambda b: (b, 0, 0)),
                   pl.BlockSpec((1, CONV_K - 1, conv_dim), lambda b: (b, 0, 0))),
        scratch_shapes=[pltpu.VMEM((8 + CHUNK, conv_dim), F32), pltpu.VMEM((D_STATE, d_inner), F32),
                        pltpu.VMEM((CHUNK, 128), F32), pltpu.VMEM((CHUNK, d_inner), F32),
                        pltpu.VMEM((CHUNK, 128), F32)],
        compiler_params=_cparams(("arbitrary",), 40),
        name="ssd_sample",
    )(proj3, proj3, proj3, proj3, proj3, small, conv0, h0, *plist)


def _attn_out(acc, l, wuv_ref, nh, tq):
    o_lat = acc / l
    outs = []
    for p in range(nh // 2):
        pair = jnp.concatenate([o_lat[(2 * p) * tq:(2 * p + 1) * tq], o_lat[(2 * p + 1) * tq:(2 * p + 2) * tq]],
                               axis=1).astype(BF16)
        outs.append(_dot(pair, wuv_ref[p]))
    return jnp.concatenate(outs, axis=1)


def _attn_prompt_kernel(q_ref, k_ref, vt_ref, wuvt_ref, o_ref, m_ref, l_ref, acc_ref, p_ref, *, tk, lp, q_off):
    qi = pl.program_id(1) + q_off
    nh, tq = q_ref.shape[1], q_ref.shape[2]
    hpg = q_ref.shape[1] // m_ref.shape[0]
    ngrp = nh // hpg
    gq = hpg * tq
    m_ref[...] = jnp.full(m_ref.shape, NEG, F32)
    l_ref[...] = jnp.zeros(l_ref.shape, F32)
    qidx = qi * tq + (lax.broadcasted_iota(jnp.int32, (1, gq), 1) & (tq - 1))
    nkc = ((qi + 1) * tq + tk - 1) // tk

    def chunk_start(kc):
        return pl.multiple_of(jnp.minimum(kc * tk, lp - tk), 128)

    def stage(kc, scores, values, diagonal):
        if scores:
            k = k_ref[pl.ds(chunk_start(kc), tk), :]
            kidx = chunk_start(kc) + lax.broadcasted_iota(jnp.int32, (tk, 1), 0)
        if values:
            vt = vt_ref[0, :, pl.ds(chunk_start(kc - 1), tk)]
        for r in range(ngrp):
            if values:
                acc = acc_ref[r] + _dot(vt, p_ref[r])
            else:
                acc = jnp.zeros(acc_ref.shape[1:], F32)
            if scores:
                q = q_ref[0, hpg * r:hpg * (r + 1)].reshape(gq, q_ref.shape[3])
                s = _dot_nt(k, q)
                if diagonal:
                    s = jnp.where((kidx <= qidx) & (kidx >= jnp.maximum(kc * tk, LEAD)), s, NEG)
                else:
                    s = jnp.concatenate([jnp.where(kidx[0:CHUNK] >= LEAD, s[0:CHUNK], NEG), s[CHUNK:]], axis=0)
                m_old = m_ref[r:r + 1, :]
                m_new = jnp.maximum(m_old, jnp.max(s, axis=0, keepdims=True))
                pexp = jnp.exp2(s - m_new)
                alpha = jnp.exp2(m_old - m_new)
                l_ref[r:r + 1, :] = alpha * l_ref[r:r + 1, :] + jnp.sum(pexp, axis=0, keepdims=True)
                p_ref[r] = pexp.astype(BF16)
                m_ref[r:r + 1, :] = m_new
                if values:
                    acc = alpha * acc
            acc_ref[r] = acc

    def body(kc, carry):
        stage(kc, True, True, False)
        return carry

    @pl.when(nkc == 1)
    def _():
        stage(0, True, False, True)

    @pl.when(nkc > 1)
    def _():
        stage(0, True, False, False)
        lax.fori_loop(1, nkc - 1, body, 0)
        stage(nkc - 1, True, True, True)

    stage(nkc, False, True, False)
    for r in range(ngrp):
        o_lat_t = (acc_ref[r] * (1.0 / l_ref[r:r + 1, :])).astype(BF16)
        for j in range(hpg // 2):
            h0 = hpg * r + 2 * j
            pair = jnp.concatenate([_dot(wuvt_ref[h0 + e], o_lat_t[:, (2 * j + e) * tq:(2 * j + e + 1) * tq])
                                    for e in range(2)], axis=0)
            o_ref[:, h0 * V_HEAD:(h0 + 2) * V_HEAD] = pair.T.astype(o_ref.dtype)


def _attn_prompt(q4, kcat, vt, wuvt, bp, lp):
    q_off = 1
    nq = lp // CHUNK - q_off
    nh = MLA_HEADS
    kv_lora = vt.shape[1]
    tk = 512
    hpg = 4
    assert lp % 128 == 0 and lp >= tk and 2 * V_HEAD == 128 and nh % hpg == 0
    return pl.pallas_call(
        functools.partial(_attn_prompt_kernel, tk=tk, lp=lp, q_off=q_off),
        out_shape=jax.ShapeDtypeStruct((bp * nq * CHUNK, nh * V_HEAD), BF16),
        grid=(bp, nq),
        in_specs=[pl.BlockSpec((1, nh, CHUNK, QK_DIM), lambda b, i: (b * nq + i, 0, 0, 0)),
                  pl.BlockSpec((lp, QK_DIM), lambda b, i: (b, 0)),
                  pl.BlockSpec((1, kv_lora, lp), lambda b, i: (b, 0, 0)),
                  _resident(wuvt.shape)],
        out_specs=pl.BlockSpec((CHUNK, nh * V_HEAD), lambda b, i: (b * nq + i, 0)),
        scratch_shapes=[pltpu.VMEM((nh // hpg, hpg * CHUNK), F32), pltpu.VMEM((nh // hpg, hpg * CHUNK), F32),
                        pltpu.VMEM((nh // hpg, kv_lora, hpg * CHUNK), F32),
                        pltpu.VMEM((nh // hpg, tk, hpg * CHUNK), BF16)],
        compiler_params=_cparams(("arbitrary", "arbitrary"), 40),
        name="attn_prompt",
    )(q4, kcat, vt, wuvt)


def _attn_sample_kernel(pt_ref, q_ref, knew_ref, vnew_ref, foldt_ref, wuv_ref, ckv_hbm, krt_hbm, o_ref,
                        cbuf, rbuf, sem, m_ref, l_ref, acc_ref, q_scr, qlt_scr, qpt_scr, *, npg, nchunk, tdec):
    b = pl.program_id(0)
    c = pl.program_id(1)
    nb = pl.num_programs(0)
    step = b * nchunk + c
    slot = step % 2
    nh = q_ref.shape[1]
    rows = nh * tdec
    kv_lora = cbuf.shape[-1]
    page = cbuf.shape[2]

    def copies(bb, cc, sl):
        out = []
        for j in range(npg):
            pg = pt_ref[bb, cc * npg + j]
            out.append(pltpu.make_async_copy(ckv_hbm.at[pg], cbuf.at[sl, j], sem.at[0, sl]))
            out.append(pltpu.make_async_copy(krt_hbm.at[pg], rbuf.at[sl, :, pl.ds(j * page, page)], sem.at[1, sl]))
        return out

    @pl.when(step == 0)
    def _():
        for cp in copies(b, c, slot):
            cp.start()

    @pl.when(step + 1 < nb * nchunk)
    def _():
        nxt = step + 1
        for cp in copies(nxt // nchunk, nxt % nchunk, 1 - slot):
            cp.start()

    @pl.when(c == 0)
    def _():
        m_ref[...] = jnp.full(m_ref.shape, NEG, F32)
        l_ref[...] = jnp.zeros(l_ref.shape, F32)
        acc_ref[...] = jnp.zeros(acc_ref.shape, F32)
        q = q_ref[0].reshape(rows, q_ref.shape[3])
        q_scr[...] = q.astype(BF16)
        for j in range(kv_lora // 128):
            qlt_scr[j * 128:(j + 1) * 128, :] = q[:, j * 128:(j + 1) * 128].T.astype(BF16)
        qpt_scr[...] = lax.dot_general(foldt_ref[...], q[:, kv_lora:kv_lora + 128], (((1,), (1,)), ((), ())),
                                       preferred_element_type=F32, precision=HIGHEST).astype(BF16)

    for cp in copies(b, c, slot):
        cp.wait()

    def lanes_to_rows(x):
        return jnp.broadcast_to(x, (rows, rows)).T

    def widen(x):
        return jnp.concatenate([lanes_to_rows(x)] * (kv_lora // 128), axis=1)

    def update(st, s_t, v):
        m_old = m_ref[st:st + 1, :]
        m_new = jnp.maximum(m_old, jnp.max(s_t, axis=0, keepdims=True))
        p_t = jnp.exp2(s_t - m_new)
        alpha = jnp.exp2(m_old - m_new)
        l_ref[st:st + 1, :] = alpha * l_ref[st:st + 1, :] + jnp.sum(p_t, axis=0, keepdims=True)
        m_ref[st:st + 1, :] = m_new
        nk = s_t.shape[0]
        if nk % 128:
            pmat = p_t.T
        else:
            pmat = jnp.concatenate([p_t[j * 128:(j + 1) * 128, :].T for j in range(nk // 128)], axis=1)
        acc_ref[st] = acc_ref[st] * widen(alpha) + _dot(pmat.astype(BF16), v)

    nstream = m_ref.shape[0]
    pps = npg // nstream
    for st in range(nstream):
        kc = cbuf[slot, st * pps:(st + 1) * pps].reshape(pps * page, kv_lora).astype(BF16)
        kr = jnp.concatenate([rbuf[slot, :, j * page:(j + 1) * page].T for j in range(st * pps, (st + 1) * pps)],
                             axis=0).astype(BF16)
        update(st, _dot(kc, qlt_scr[...]) + _dot(kr, qpt_scr[...]), kc)

    @pl.when(c == nchunk - 1)
    def _():
        knew = knew_ref[...].astype(BF16)
        s_t = _dot_nt(knew, q_scr[...])
        nk = knew.shape[0]
        kid = lax.broadcasted_iota(jnp.int32, (nk, 1), 0)
        t = lax.broadcasted_iota(jnp.int32, (1, rows), 1) & (tdec - 1)
        ok = (kid >= b * tdec) & (kid <= b * tdec + t)
        update(0, jnp.where(ok, s_t, NEG), vnew_ref[...].astype(BF16))
        m_all = jnp.max(m_ref[...], axis=0, keepdims=True)
        l_all = jnp.zeros((1, rows), F32)
        acc = jnp.zeros((rows, kv_lora), F32)
        for st in range(nstream):
            w_st = jnp.exp2(m_ref[st:st + 1, :] - m_all)
            l_all = l_all + w_st * l_ref[st:st + 1, :]
            acc = acc + widen(w_st) * acc_ref[st]
        o_ref[...] = _attn_out(acc, widen(l_all), wuv_ref, nh, tdec)


def _attn_sample(page_table, q4, kcat_new, v_new, fold, wuv, cache_ckv, cache_krt, bs, tdec):
    nh = MLA_HEADS
    n_pages = page_table.shape[1]
    page, kv_lora = cache_ckv.shape[1], cache_ckv.shape[2]
    rope = cache_krt.shape[1]
    npg = 64
    while n_pages % npg:
        npg //= 2
    nchunk = n_pages // npg
    nstream = 1
    rows = nh * tdec
    assert tdec & (tdec - 1) == 0 and rows == 128 and page == 128
    grid_spec = pltpu.PrefetchScalarGridSpec(
        num_scalar_prefetch=1,
        grid=(bs, nchunk),
        in_specs=[pl.BlockSpec((1, nh, tdec, QK_DIM), lambda b, c, pt: (b, 0, 0, 0)),
                  pl.BlockSpec(kcat_new.shape, lambda b, c, pt: (0, 0), pipeline_mode=pl.Buffered(1)),
                  pl.BlockSpec(v_new.shape, lambda b, c, pt: (0, 0), pipeline_mode=pl.Buffered(1)),
                  pl.BlockSpec(fold.shape, lambda b, c, pt: (0, 0), pipeline_mode=pl.Buffered(1)),
                  pl.BlockSpec(wuv.shape, lambda b, c, pt: (0, 0, 0), pipeline_mode=pl.Buffered(1)),
                  pl.BlockSpec(memory_space=pl.ANY), pl.BlockSpec(memory_space=pl.ANY)],
        out_specs=pl.BlockSpec((tdec, nh * V_HEAD), lambda b, c, pt: (b, 0)),
        scratch_shapes=[pltpu.VMEM((2, npg, page, kv_lora), F32), pltpu.VMEM((2, rope, npg * page), F32),
                        pltpu.SemaphoreType.DMA((2, 2)),
                        pltpu.VMEM((nstream, rows), F32), pltpu.VMEM((nstream, rows), F32),
                        pltpu.VMEM((nstream, rows, kv_lora), F32),
                        pltpu.VMEM((rows, QK_DIM), BF16), pltpu.VMEM((kv_lora, rows), BF16),
                        pltpu.VMEM((rope, rows), BF16)],
    )
    return pl.pallas_call(
        functools.partial(_attn_sample_kernel, npg=npg, nchunk=nchunk, tdec=tdec),
        out_shape=jax.ShapeDtypeStruct((bs * tdec, nh * V_HEAD), F32),
        grid_spec=grid_spec,
        compiler_params=_cparams(("arbitrary", "arbitrary"), 40),
        name="attn_sample",
    )(page_table, q4, kcat_new, v_new, fold, wuv, cache_ckv, cache_krt)


def _merge_kernel(yn_ref, o_ref, ga_ref, gb_ref, h_ref, wa_ref, wb_ref, wo_ref, g_ref, out_ref):
    a = _dot(yn_ref[...].astype(BF16), wa_ref[...])
    bb = _dot(o_ref[...].astype(BF16), wb_ref[...])
    merged = jax.nn.sigmoid(ga_ref[0].astype(F32)) * a + jax.nn.sigmoid(gb_ref[0].astype(F32)) * bb
    mix = _dot(merged.astype(BF16), wo_ref[...])
    out_ref[...] = h_ref[...] + _rms(mix, g_ref[...])


def _merge(yn, o, proj3, h, wa, wb, wo, g):
    m, d = yn.shape[0], h.shape[1]
    tm = _row_tile(m, 512)
    return pl.pallas_call(
        _merge_kernel,
        out_shape=jax.ShapeDtypeStruct((m, d), F32),
        grid=(m // tm,),
        in_specs=[pl.BlockSpec((tm, yn.shape[1]), lambda i: (i, 0)), pl.BlockSpec((tm, o.shape[1]), lambda i: (i, 0)),
                  pl.BlockSpec((1, tm, SLAB), lambda i: (5, i, 0)), pl.BlockSpec((1, tm, SLAB), lambda i: (6, i, 0)),
                  pl.BlockSpec((tm, d), lambda i: (i, 0)),
                  _resident(wa.shape), _resident(wb.shape), _resident(wo.shape), _resident((1, d))],
        out_specs=pl.BlockSpec((tm, d), lambda i: (i, 0)),
        compiler_params=_cparams(("arbitrary",), 40),
        name="gated_merge",
    )(yn, o, proj3, proj3, h, wa, wb, wo, g)


def _prep_weights(w, d_inner, conv_dim, nheads, q_lora, kv_lora):
    d = w["w_in"].shape[0]
    offs = np.cumsum([0, d_inner, conv_dim, nheads, q_lora, kv_lora, QK_ROPE, d, d])
    seg = {k: w["w_in"][:, offs[i]:offs[i + 1]]
           for i, k in enumerate(("z", "xbc", "dt", "q_a", "kv_a", "k_pe", "ga", "gb"))}
    pad = SLAB - q_lora - kv_lora
    wbig = jnp.concatenate([seg["z"], seg["xbc"], seg["ga"], seg["gb"], seg["q_a"], seg["kv_a"],
                            jnp.zeros((d, pad), F32)], axis=1).astype(BF16)
    swap = np.concatenate([np.arange(QK_ROPE // 2, QK_ROPE), np.arange(QK_ROPE // 2)])
    wsmall = jnp.concatenate([seg["dt"], jnp.zeros((d, 128 - nheads), F32), jnp.tile(seg["k_pe"], (1, 4)),
                              jnp.tile(seg["k_pe"][:, swap], (1, 4))], axis=1).astype(BF16)
    nh = MLA_HEADS
    wq3 = w["w_q_b"].reshape(q_lora, nh, QK_NOPE + QK_ROPE)
    wq = jnp.concatenate([wq3[:, :, :QK_NOPE].reshape(q_lora, nh * QK_NOPE),
                          wq3[:, :, QK_NOPE:].reshape(q_lora, nh * QK_ROPE),
                          wq3[:, :, QK_NOPE:][:, :, swap].reshape(q_lora, nh * QK_ROPE)], axis=1).astype(BF16)
    wk = jnp.transpose(w["w_uk"], (1, 2, 0)).reshape(nh // 2, 2, QK_NOPE, kv_lora)
    zk = jnp.zeros((nh // 2, QK_NOPE, kv_lora), F32)
    wuk = jnp.concatenate([jnp.concatenate([wk[:, 0], zk], axis=2),
                           jnp.concatenate([zk, wk[:, 1]], axis=2)], axis=1).astype(BF16)
    wv = jnp.transpose(w["w_uv"], (1, 0, 2)).reshape(nh // 2, 2, kv_lora, V_HEAD)
    zv = jnp.zeros((nh // 2, kv_lora, V_HEAD), F32)
    wuvt = jnp.transpose(w["w_uv"], (1, 2, 0)).astype(BF16)
    wuv = jnp.concatenate([jnp.concatenate([wv[:, 0], zv], axis=2),
                           jnp.concatenate([zv, wv[:, 1]], axis=2)], axis=1).astype(BF16)
    def expand3(width):
        e = np.kron(np.eye(128, nheads, dtype=np.float32), np.ones((1, width), np.float32))
        return jnp.asarray(np.tile(e, (3, 1)), BF16)

    lane_pad = lambda a: jnp.pad(a, (0, 128 - nheads))[None]
    ssd = dict(conv_w=w["conv_w"], conv_b=w["conv_b"][None], dt_bias=lane_pad(w["dt_bias"]),
               a_log=lane_pad(w["a_log"]),
               dskip_x=jnp.repeat(w["d_skip"], SSM_HEAD_DIM)[None], norm_g=w["ssm_norm_g"][None],
               tril=jnp.asarray(np.tril(np.ones((CHUNK, CHUNK), np.float32))),
               expand=expand3(SSM_HEAD_DIM), expand128=expand3(128),
               shift=jnp.asarray(np.concatenate([np.eye(CHUNK, k=k - (CONV_K - 1), dtype=np.float32)
                                                 for k in range(CONV_K - 1)], axis=0), BF16))
    inv = ROPE_THETA ** (-jnp.arange(0, QK_ROPE, 2, dtype=F32) / QK_ROPE)
    inv128 = jnp.tile(jnp.concatenate([inv, inv]), 4)[None]
    sign128 = jnp.asarray(np.tile(np.concatenate([-np.ones(QK_ROPE // 2), np.ones(QK_ROPE // 2)]), 4)[None], F32)
    foldt = jnp.asarray(np.tile(np.eye(QK_ROPE, dtype=np.float32), (1, 4)))
    bf = lambda a: a.astype(BF16)
    row = lambda a: a[None]
    return dict(
        wbig=wbig, wsmall=wsmall, wq=wq, wuk=wuk, wuv=wuv, wuvt=wuvt, ssd=ssd, inv128=inv128, sign128=sign128, foldt=foldt,
        ffn1=(row(w["ffn1_pre_g"]), bf(w["ffn1_w_gate"]), bf(w["ffn1_w_up"]), bf(w["ffn1_w_down"]),
              row(w["ffn1_post_g"])),
        ffn2=(row(w["ffn2_pre_g"]), bf(w["ffn2_w_gate"]), bf(w["ffn2_w_up"]), bf(w["ffn2_w_down"]),
              row(w["ffn2_post_g"])),
        mix_pre_g=row(w["mix_pre_g"]), mix_post_g=row(w["mix_post_g"]), q_g=row(w["q_a_norm_g"]),
        kv_g=row(w["kv_a_norm_g"]), wa=bf(w["w_a_out"]), wb=bf(w["w_b_out"]), wo=bf(w["w_o"]))


def kernel(x_prompt, x_sample, cache_kv_latent, cache_k_rope, state_ssm, state_conv, page_table, meta_tokens,
           ffn1_pre_g, ffn1_w_gate, ffn1_w_up, ffn1_w_down, ffn1_post_g,
           mix_pre_g, w_in, conv_w, conv_b, dt_bias, a_log, d_skip, ssm_norm_g,
           q_a_norm_g, w_q_b, kv_a_norm_g, w_uk, w_uv, w_a_out, w_b_out, w_o, mix_post_g,
           ffn2_pre_g, ffn2_w_gate, ffn2_w_up, ffn2_w_down, ffn2_post_g):
    names = ("ffn1_pre_g", "ffn1_w_gate", "ffn1_w_up", "ffn1_w_down", "ffn1_post_g", "mix_pre_g", "w_in", "conv_w",
             "conv_b", "dt_bias", "a_log", "d_skip", "ssm_norm_g", "q_a_norm_g", "w_q_b", "kv_a_norm_g", "w_uk",
             "w_uv", "w_a_out", "w_b_out", "w_o", "mix_post_g", "ffn2_pre_g", "ffn2_w_gate", "ffn2_w_up",
             "ffn2_w_down", "ffn2_post_g")
    stacked = dict(zip(names, (ffn1_pre_g, ffn1_w_gate, ffn1_w_up, ffn1_w_down, ffn1_post_g, mix_pre_g, w_in,
                               conv_w, conv_b, dt_bias, a_log, d_skip, ssm_norm_g, q_a_norm_g, w_q_b, kv_a_norm_g,
                               w_uk, w_uv, w_a_out, w_b_out, w_o, mix_post_g, ffn2_pre_g, ffn2_w_gate, ffn2_w_up,
                               ffn2_w_down, ffn2_post_g)))
    depth = w_in.shape[0]
    bp, seq, d = x_prompt.shape
    bs, tdec, _ = x_sample.shape
    nheads = dt_bias.shape[1]
    d_inner = nheads * SSM_HEAD_DIM
    conv_dim = conv_b.shape[1]
    q_lora, kv_lora = q_a_norm_g.shape[1], kv_a_norm_g.shape[1]
    assert seq % CHUNK == 0 and conv_dim == 3 * SLAB and d_inner == 2 * SLAB and d == SLAB
    assert q_lora + kv_lora <= SLAB and CHUNK % tdec == 0
    n_pages, page = page_table.shape[1], cache_kv_latent.shape[2]
    past_len = n_pages * page
    lp = LEAD + N_META + seq
    nc = lp // CHUNK
    ns_rows = bs * tdec
    assert depth == 1 and ns_rows % CHUNK == 0
    meta_tile = ns_rows // CHUNK
    meta0 = ns_rows + LEAD

    hx = x_prompt.reshape(bp * seq, d)
    hs = jnp.concatenate([x_sample.reshape(ns_rows, d), jnp.zeros((LEAD, d), F32), meta_tokens], axis=0)
    pos_x = jnp.tile(N_META + jnp.arange(seq, dtype=jnp.int32), bp).astype(F32)[:, None]
    pos_s = jnp.concatenate([jnp.tile(past_len + jnp.arange(tdec, dtype=jnp.int32), bs),
                             jnp.arange(CHUNK, dtype=jnp.int32) - LEAD]).astype(F32)[:, None]

    outs = [[] for _ in range(8)]
    for l in range(depth):
        w = _prep_weights({k: v[l] for k, v in stacked.items()}, d_inner, conv_dim, nheads, q_lora, kv_lora)

        g1 = _ffn(hs, *w["ffn1"])
        proj3s, smalls = _inproj(g1, w["mix_pre_g"], w["wbig"], w["wsmall"], F32)
        c_kv_s, kpe_s, kcat_s, q4s = _qkv_prep(proj3s, smalls, pos_s, w["q_g"], w["kv_g"], w["wq"], w["wuk"],
                                               w["inv128"], w["sign128"], tdec, F32)

        h1 = _ffn(hx, *w["ffn1"])
        proj3, small = _inproj(h1, w["mix_pre_g"], w["wbig"], w["wsmall"], BF16)
        yn, h_fin, conv_new = _ssd_prompt(proj3, small, proj3s, smalls, meta_tile, w["ssd"], bp, nc)
        c_kv, kpe, kcat, q4 = _qkv_prep(proj3, small, pos_x, w["q_g"], w["kv_g"], w["wq"], w["wuk"], w["inv128"],
                                        w["sign128"], CHUNK, BF16)

        def with_meta(tile, rows):
            f = rows.shape[1]
            return jnp.concatenate([jnp.broadcast_to(tile.astype(rows.dtype)[None], (bp,) + tile.shape),
                                    rows.reshape(bp, seq, f)], axis=1)

        k_all = with_meta(kcat_s[ns_rows:], kcat).reshape(bp * lp, QK_DIM)
        vt = jnp.transpose(with_meta(c_kv_s[ns_rows:].astype(BF16), c_kv.astype(BF16)), (0, 2, 1))
        o = _attn_prompt(q4, k_all, vt, w["wuvt"], bp, lp)
        h2 = _merge(yn, o, proj3, h1, w["wa"], w["wb"], w["wo"], w["mix_post_g"])
        hx = _ffn(h2, *w["ffn2"])
        outs[0].append(with_meta(c_kv_s[meta0:], c_kv))
        outs[1].append(with_meta(kpe_s[meta0:, :QK_ROPE], kpe[:, :QK_ROPE]))
        outs[2].append(h_fin.reshape(bp, nheads, SSM_HEAD_DIM, D_STATE))
        outs[3].append(conv_new)

        yns, h_new, conv_new_s = _ssd_sample(proj3s, smalls, state_conv[l],
                                             state_ssm[l].reshape(bs, d_inner, D_STATE), w["ssd"], bs, tdec)
        os_ = _attn_sample(page_table, q4s, kcat_s[:ns_rows], c_kv_s[:ns_rows], w["foldt"], w["wuv"],
                           cache_kv_latent[l], jnp.transpose(cache_k_rope[l], (0, 2, 1)), bs, tdec)
        g2 = _merge(yns, os_, proj3s, g1, w["wa"], w["wb"], w["wo"], w["mix_post_g"])
        hs = _ffn(g2, *w["ffn2"])
        outs[4].append(c_kv_s[:ns_rows].reshape(bs, tdec, kv_lora))
        outs[5].append(kpe_s[:ns_rows, :QK_ROPE].reshape(bs, tdec, QK_ROPE))
        outs[6].append(h_new.reshape(bs, nheads, SSM_HEAD_DIM, D_STATE))
        outs[7].append(conv_new_s)

    y_prompt = hx.reshape(bp, seq, d)
    y_sample = hs.reshape(bs, tdec, d)
    return (y_prompt, y_sample) + tuple(jnp.stack(o) for o in outs)
```

```python
import functools
import math

import numpy as np
import jax
import jax.numpy as jnp
from jax import lax
from jax.experimental import pallas as pl
from jax.experimental.pallas import tpu as pltpu

F32 = jnp.float32
BF16 = jnp.bfloat16
HIGHEST = lax.Precision.HIGHEST

EPS = 1e-6
N_META = 16
CHUNK = 128
LEAD = (-N_META) % CHUNK
SSM_HEAD_DIM = 64
SSM_GROUPS = 4
D_STATE = 128
CONV_K = 4
MLA_HEADS = 16
QK_NOPE = 64
QK_ROPE = 32
V_HEAD = 64
ROPE_THETA = 10000.0
ATTN_SCALE = (QK_NOPE + QK_ROPE) ** -0.5
Q_SCALE = ATTN_SCALE * math.log2(math.e)
NEG = -1e30
SLAB = 1024
QK_DIM = 384
MIB = 1024 * 1024


def _cparams(sem, vmem_mib):
    return pltpu.CompilerParams(dimension_semantics=sem, vmem_limit_bytes=int(vmem_mib * MIB))


def _resident(shape):
    nd = len(shape)
    return pl.BlockSpec(shape, lambda *_: (0,) * nd, pipeline_mode=pl.Buffered(1))


def _rms(x, g):
    return x * lax.rsqrt(jnp.mean(x * x, axis=-1, keepdims=True) + EPS) * g


def _dot(a, b):
    return jnp.dot(a, b, preferred_element_type=F32)


def _dot_exact(a, b):
    return jnp.dot(a, b, preferred_element_type=F32, precision=HIGHEST)


def _dot_nt(a, b):
    return lax.dot_general(a, b, (((1,), (1,)), ((), ())), preferred_element_type=F32)


def _expand_exact(x, e3):
    hi = x.astype(BF16)
    r1 = x - hi.astype(F32)
    mid = r1.astype(BF16)
    lo = (r1 - mid.astype(F32)).astype(BF16)
    return _dot(jnp.concatenate([hi, mid, lo], axis=1), e3)


def _softplus(x):
    return jnp.maximum(x, 0.0) + jnp.log1p(jnp.exp(-jnp.abs(x)))


def _row_tile(m, cap, mult=16):
    for t in range(min(cap, m), mult - 1, -1):
        if m % t == 0 and t % mult == 0:
            return t
    raise ValueError(f"no row tile for {m}")


def _ffn_kernel(x_ref, pre_ref, wg_ref, wu_ref, wd_ref, post_ref, o_ref, *, chunks):
    x = x_ref[...]
    xn = _rms(x, pre_ref[...]).astype(BF16)
    acc = jnp.zeros(x.shape, F32)
    for lo, sz in chunks:
        g = _dot(xn, wg_ref[:, lo:lo + sz])
        u = _dot(xn, wu_ref[:, lo:lo + sz])
        h = (jax.nn.silu(g) * u).astype(BF16)
        acc = acc + _dot(h, wd_ref[lo:lo + sz, :])
    o_ref[...] = x + 0.5 * _rms(acc, post_ref[...])


def _ffn_chunks(f):
    out, lo = [], 0
    while lo < f:
        sz = min(1024, f - lo)
        out.append((lo, sz))
        lo += sz
    return tuple(out)


def _ffn(x, pre_g, wg, wu, wd, post_g):
    m, d = x.shape
    f = wg.shape[1]
    tm = _row_tile(m, 512)
    return pl.pallas_call(
        functools.partial(_ffn_kernel, chunks=_ffn_chunks(f)),
        out_shape=jax.ShapeDtypeStruct((m, d), F32),
        grid=(m // tm,),
        in_specs=[pl.BlockSpec((tm, d), lambda i: (i, 0)), _resident((1, d)), _resident((d, f)),
                  _resident((d, f)), _resident((f, d)), _resident((1, d))],
        out_specs=pl.BlockSpec((tm, d), lambda i: (i, 0)),
        compiler_params=_cparams(("arbitrary",), 52),
        name="ffn_block",
    )(x, pre_g, wg, wu, wd, post_g)


def _inproj_kernel(x_ref, g_ref, wbig_ref, wsmall_ref, proj_ref, small_ref, *, nslab):
    xn = _rms(x_ref[...], g_ref[...]).astype(BF16)
    for s in range(nslab):
        proj_ref[s] = _dot(xn, wbig_ref[:, s * SLAB:(s + 1) * SLAB]).astype(proj_ref.dtype)
    small_ref[...] = _dot(xn, wsmall_ref[...])


def _inproj(h, g, wbig, wsmall, proj_dtype):
    m, d = h.shape
    nslab = wbig.shape[1] // SLAB
    ns = wsmall.shape[1]
    tm = _row_tile(m, 512)
    return pl.pallas_call(
        functools.partial(_inproj_kernel, nslab=nslab),
        out_shape=(jax.ShapeDtypeStruct((nslab, m, SLAB), proj_dtype), jax.ShapeDtypeStruct((m, ns), F32)),
        grid=(m // tm,),
        in_specs=[pl.BlockSpec((tm, d), lambda i: (i, 0)), _resident((1, d)), _resident(wbig.shape),
                  _resident(wsmall.shape)],
        out_specs=(pl.BlockSpec((nslab, tm, SLAB), lambda i: (0, i, 0)), pl.BlockSpec((tm, ns), lambda i: (i, 0))),
        compiler_params=_cparams(("arbitrary",), 52),
        name="in_proj",
    )(h, g, wbig, wsmall)


def _qkv_kernel(s7_ref, small_ref, pos_ref, qg_ref, kvg_ref, wq_ref, wuk_ref, inv_ref, sign_ref,
                ckv_ref, kpe_ref, kcat_ref, q_ref, *, tqb):
    s7 = s7_ref[0].astype(F32)
    tm = s7.shape[0]
    q_lora = qg_ref.shape[1]
    kv_lora = kvg_ref.shape[1]
    nh = MLA_HEADS
    c_kv = _rms(s7[:, q_lora:q_lora + kv_lora], kvg_ref[...])
    ckv_ref[...] = c_kv
    ang = pos_ref[...] * inv_ref[...]
    cos = jnp.cos(ang)
    sin = jnp.sin(ang) * sign_ref[...]
    small = small_ref[...]
    kpe = small[:, 128:256] * cos + small[:, 256:384] * sin
    kpe_ref[...] = kpe
    kcat_ref[:, 0:kv_lora] = c_kv.astype(kcat_ref.dtype)
    kcat_ref[:, kv_lora:kv_lora + 128] = kpe.astype(kcat_ref.dtype)

    qn = _rms(s7[:, 0:q_lora], qg_ref[...] * Q_SCALE).astype(BF16)
    nope_w = nh * QK_NOPE
    pe_w = nh * QK_ROPE
    q_nope = _dot(qn, wq_ref[:, 0:nope_w])
    q_pe = _dot(qn, wq_ref[:, nope_w:nope_w + pe_w])
    q_rot = _dot(qn, wq_ref[:, nope_w + pe_w:nope_w + 2 * pe_w])
    cos_w = jnp.concatenate([cos] * (pe_w // 128), axis=1)
    sin_w = jnp.concatenate([sin] * (pe_w // 128), axis=1)
    q_pe = q_pe * cos_w + q_rot * sin_w
    lane = lax.broadcasted_iota(jnp.int32, (1, 128), 1)
    nblk = tm // tqb
    for p in range(nh // 2):
        qn_pair = q_nope[:, p * 128:(p + 1) * 128].astype(BF16)
        q_lat = _dot(qn_pair, wuk_ref[p])
        for e in range(2):
            hd = 2 * p + e
            pe_blk = q_pe[:, (hd // 4) * 128:(hd // 4 + 1) * 128]
            sel = (lane >= (hd % 4) * QK_ROPE) & (lane < (hd % 4 + 1) * QK_ROPE)
            pe_blk = jnp.where(sel, pe_blk, 0.0)
            lat = q_lat[:, e * kv_lora:(e + 1) * kv_lora]
            for g in range(nblk):
                q_ref[g, hd, :, 0:kv_lora] = lat[g * tqb:(g + 1) * tqb].astype(q_ref.dtype)
                q_ref[g, hd, :, kv_lora:kv_lora + 128] = pe_blk[g * tqb:(g + 1) * tqb].astype(q_ref.dtype)


def _qkv_prep(proj3, small, pos, qg, kvg, wq, wuk, inv128, sign128, tqb, dtype):
    nslab, m, _ = proj3.shape
    kv_lora = kvg.shape[1]
    tm = _row_tile(m, 512, max(16, tqb))
    nblk = tm // tqb
    return pl.pallas_call(
        functools.partial(_qkv_kernel, tqb=tqb),
        out_shape=(jax.ShapeDtypeStruct((m, kv_lora), F32), jax.ShapeDtypeStruct((m, 128), F32),
                   jax.ShapeDtypeStruct((m, QK_DIM), dtype),
                   jax.ShapeDtypeStruct((m // tqb, MLA_HEADS, tqb, QK_DIM), dtype)),
        grid=(m // tm,),
        in_specs=[pl.BlockSpec((1, tm, SLAB), lambda i: (nslab - 1, i, 0)),
                  pl.BlockSpec((tm, small.shape[1]), lambda i: (i, 0)),
                  pl.BlockSpec((tm, 1), lambda i: (i, 0)),
                  _resident(qg.shape), _resident(kvg.shape), _resident(wq.shape), _resident(wuk.shape),
                  _resident((1, 128)), _resident((1, 128))],
        out_specs=(pl.BlockSpec((tm, kv_lora), lambda i: (i, 0)), pl.BlockSpec((tm, 128), lambda i: (i, 0)),
                   pl.BlockSpec((tm, QK_DIM), lambda i: (i, 0)),
                   pl.BlockSpec((nblk, MLA_HEADS, tqb, QK_DIM), lambda i: (i, 0, 0, 0))),
        compiler_params=_cparams(("arbitrary",), 48),
        name="qkv_prep",
    )(proj3, small, pos, qg, kvg, wq, wuk, inv128, sign128)


def _ssd_chunk(xh_ref, dt_raw, state_ref, apad_ref, y_ref, lo, hi, p, bf16_input):
    d_inner = state_ref.shape[1]
    nheads = d_inner // SSM_HEAD_DIM
    gw = d_inner // SSM_GROUPS
    rows = lax.broadcasted_iota(jnp.int32, (CHUNK, 1), 0)
    valid = (rows >= lo) & (rows < hi)
    ntap = CONV_K - 1

    def conv_rows(nrows):
        out = p["conv_b"][...]
        for k in range(CONV_K):
            out = out + p["conv_w"][k:k + 1, :] * xh_ref[pl.ds(8 - ntap + k, nrows), :]
        return out

    if bf16_input:
        cur = xh_ref[8:8 + CHUNK, :]
        shifted = _dot(p["shift"][...], cur.astype(BF16))
        conv = p["conv_b"][...]
        for k in range(ntap):
            conv = conv + p["conv_w"][k:k + 1, :] * shifted[k * CHUNK:(k + 1) * CHUNK]
        conv = conv + p["conv_w"][ntap:CONV_K, :] * cur
        conv = jnp.concatenate([conv_rows(8), conv[8:]], axis=0)
    else:
        conv = conv_rows(CHUNK)
    xc = jnp.where(valid, jax.nn.silu(conv), 0.0)
    xs = xc[:, 0:d_inner]
    bm = xc[:, d_inner:d_inner + SSM_GROUPS * D_STATE]
    cm = xc[:, d_inner + SSM_GROUPS * D_STATE:]

    dt = jnp.where(valid, _softplus(dt_raw + p["dt_bias"][...]), 0.0)
    da = dt * (-jnp.exp(p["a_log"][...]))
    a_cs = _dot_exact(p["tril"][...], da) * math.log2(math.e)
    dt_x = _expand_exact(dt, p["expand"][...])
    acs_x = _expand_exact(a_cs, p["expand"][...])
    alast_x = acs_x[CHUNK - 1:CHUNK, :]
    acs_b = _expand_exact(a_cs, p["expand128"][...])
    apad_ref[...] = a_cs.T
    xdt = xs * dt_x
    xds = (xdt * jnp.exp2(alast_x - acs_x)).astype(BF16)
    exp_acs = jnp.exp2(acs_x)
    causal = rows >= lax.broadcasted_iota(jnp.int32, (1, CHUNK), 1)
    lane = lax.broadcasted_iota(jnp.int32, (1, 128), 1)
    hpg = nheads // SSM_GROUPS

    for g in range(SSM_GROUPS):
        gs = slice(g * gw, (g + 1) * gw)
        bg = bm[:, g * D_STATE:(g + 1) * D_STATE]
        cg = cm[:, g * D_STATE:(g + 1) * D_STATE].astype(BF16)
        cb = _dot_nt(cg, bg.astype(BF16))
        st = state_ref[:, gs]
        y_off = _dot(cg, st.astype(BF16)) * exp_acs[:, gs]
        state_ref[:, gs] = st * jnp.exp2(alast_x[:, gs]) + _dot(bg.T.astype(BF16), xds[:, gs])
        y_ref[:, gs] = y_off + xs[:, gs] * p["dskip_x"][:, gs]
        for pr in range(hpg // 2):
            w2 = []
            for e in range(2):
                hd = g * hpg + 2 * pr + e
                seg = acs_b[:, hd * 128:(hd + 1) * 128] - apad_ref[pl.ds(hd, 1), :]
                decay = jnp.exp2(jnp.where(causal, seg, NEG))
                w2.append((cb * decay).astype(BF16))
            ls = slice(g * gw + pr * 128, g * gw + (pr + 1) * 128)
            xp = xdt[:, ls]
            x2 = jnp.concatenate([jnp.where(lane < SSM_HEAD_DIM, xp, 0.0),
                                  jnp.where(lane >= SSM_HEAD_DIM, xp, 0.0)], axis=0).astype(BF16)
            y_ref[:, ls] = y_ref[:, ls] + _dot(jnp.concatenate(w2, axis=1), x2)


def _gate_norm(y, z, g, d_inner):
    yg = y * jax.nn.silu(z)
    gw = d_inner // SSM_GROUPS
    outs = []
    for k in range(SSM_GROUPS):
        v = yg[:, k * gw:(k + 1) * gw]
        outs.append(_rms(v, g[:, k * gw:(k + 1) * gw]))
    return jnp.concatenate(outs, axis=1)


_SSD_PARAM_NAMES = ("conv_w", "conv_b", "dt_bias", "a_log", "dskip_x", "norm_g", "tril", "expand", "expand128",
                    "shift")


def _ssd_prompt_kernel(x0_ref, x1_ref, bc_ref, z0_ref, z1_ref, small_ref, m0_ref, m1_ref, mbc_ref, msmall_ref,
                       *rest, nc):
    np_ = len(_SSD_PARAM_NAMES)
    p = dict(zip(_SSD_PARAM_NAMES, rest[:np_]))
    yn_ref, hfin_ref, convnew_ref, xh_ref, state_ref, apad_ref, y_ref, state0_ref, tail0_ref = rest[np_:]
    b = pl.program_id(0)
    c = pl.program_id(1)
    d_inner = state_ref.shape[1]
    assert x0_ref.dtype == BF16
    shared_first = (c == 0) & (b == 0)

    @pl.when(shared_first)
    def _():
        xh_ref[0:8, :] = jnp.zeros((8, xh_ref.shape[1]), F32)
        state_ref[...] = jnp.zeros(state_ref.shape, F32)
        valid = lax.broadcasted_iota(jnp.int32, (CHUNK, 1), 0) >= LEAD
        for j, ref in enumerate((m0_ref, m1_ref, mbc_ref)):
            xh_ref[8:8 + CHUNK, j * SLAB:(j + 1) * SLAB] = jnp.where(valid, ref[0].astype(BF16).astype(F32), 0.0)

    @pl.when((c == 0) & (b > 0))
    def _():
        state_ref[...] = state0_ref[...]
        xh_ref[0:8, :] = tail0_ref[...]

    @pl.when(c > 0)
    def _():
        for j, ref in enumerate((x0_ref, x1_ref, bc_ref)):
            xh_ref[8:8 + CHUNK, j * SLAB:(j + 1) * SLAB] = ref[0].astype(F32)

    @pl.when((c > 0) | (b == 0))
    def _():
        lo = jnp.where(c == 0, LEAD, 0)
        dt_raw = jnp.where(c == 0, msmall_ref[:, 0:128], small_ref[:, 0:128])
        _ssd_chunk(xh_ref, dt_raw, state_ref, apad_ref, y_ref, lo, CHUNK, p, True)
        z = jnp.concatenate([z0_ref[0], z1_ref[0]], axis=1).astype(F32)
        yn_ref[...] = _gate_norm(y_ref[...], z, p["norm_g"][...], d_inner).astype(yn_ref.dtype)
        xh_ref[0:8, :] = xh_ref[CHUNK:CHUNK + 8, :]

    @pl.when(shared_first)
    def _():
        state0_ref[...] = state_ref[...]
        tail0_ref[...] = xh_ref[0:8, :]

    @pl.when(c == nc - 1)
    def _():
        convnew_ref[0] = xh_ref[pl.ds(8 + CHUNK - (CONV_K - 1), CONV_K - 1), :]
        for j in range(d_inner // 128):
            hfin_ref[0, j * 128:(j + 1) * 128, :] = state_ref[:, j * 128:(j + 1) * 128].T


def _ssd_prompt(proj3, small, proj3m, smallm, meta_tile, params, bp, nc):
    d_inner = params["expand"].shape[1]
    conv_dim = params["conv_b"].shape[1]
    m = proj3.shape[1]
    ns = small.shape[1]
    ncx = nc - 1
    plist = [params[k] for k in _SSD_PARAM_NAMES]

    def xblk(b, c):
        return b * ncx + jnp.maximum(c - 1, 0)

    def slab(s):
        return pl.BlockSpec((1, CHUNK, SLAB), lambda b, c, s=s: (s, xblk(b, c), 0))

    def mslab(s):
        return pl.BlockSpec((1, CHUNK, SLAB), lambda b, c, s=s: (s, meta_tile, 0))

    return pl.pallas_call(
        functools.partial(_ssd_prompt_kernel, nc=nc),
        out_shape=(jax.ShapeDtypeStruct((m, d_inner), BF16),
                   jax.ShapeDtypeStruct((bp, d_inner, D_STATE), F32),
                   jax.ShapeDtypeStruct((bp, CONV_K - 1, conv_dim), F32)),
        grid=(bp, nc),
        in_specs=[slab(2), slab(3), slab(4), slab(0), slab(1),
                  pl.BlockSpec((CHUNK, ns), lambda b, c: (xblk(b, c), 0)),
                  mslab(2), mslab(3), mslab(4), pl.BlockSpec((CHUNK, ns), lambda b, c: (meta_tile, 0))]
                 + [_resident(a.shape) for a in plist],
        out_specs=(pl.BlockSpec((CHUNK, d_inner), lambda b, c: (xblk(b, c), 0)),
                   pl.BlockSpec((1, d_inner, D_STATE), lambda b, c: (b, 0, 0)),
                   pl.BlockSpec((1, CONV_K - 1, conv_dim), lambda b, c: (b, 0, 0))),
        scratch_shapes=[pltpu.VMEM((8 + CHUNK, conv_dim), F32), pltpu.VMEM((D_STATE, d_inner), F32),
                        pltpu.VMEM((CHUNK, 128), F32), pltpu.VMEM((CHUNK, d_inner), F32),
                        pltpu.VMEM((D_STATE, d_inner), F32), pltpu.VMEM((8, conv_dim), F32)],
        compiler_params=_cparams(("arbitrary", "arbitrary"), 40),
        name="ssd_prompt",
    )(proj3, proj3, proj3, proj3, proj3, small, proj3m, proj3m, proj3m, smallm, *plist)


def _ssd_sample_kernel(x0_ref, x1_ref, bc_ref, z0_ref, z1_ref, small_ref, conv0_ref, h0_ref, *rest, tdec):
    np_ = len(_SSD_PARAM_NAMES)
    p = dict(zip(_SSD_PARAM_NAMES, rest[:np_]))
    yn_ref, hnew_ref, convnew_ref, xh_ref, state_ref, apad_ref, y_ref, dt_ref = rest[np_:]
    d_inner = state_ref.shape[1]
    nk = CONV_K - 1
    xh_ref[...] = jnp.zeros(xh_ref.shape, F32)
    xh_ref[8 - nk:8, :] = conv0_ref[0]
    xh_ref[8:8 + tdec, 0:SLAB] = x0_ref[0]
    xh_ref[8:8 + tdec, SLAB:2 * SLAB] = x1_ref[0]
    xh_ref[8:8 + tdec, 2 * SLAB:3 * SLAB] = bc_ref[0]
    dt_ref[...] = jnp.zeros(dt_ref.shape, F32)
    dt_ref[0:tdec, :] = small_ref[:, 0:128]
    for j in range(d_inner // 128):
        state_ref[:, j * 128:(j + 1) * 128] = h0_ref[0, j * 128:(j + 1) * 128, :].T
    _ssd_chunk(xh_ref, dt_ref[...], state_ref, apad_ref, y_ref, 0, tdec, p, False)
    z = jnp.concatenate([z0_ref[0], z1_ref[0]], axis=1)
    yn_ref[...] = _gate_norm(y_ref[0:tdec, :], z, p["norm_g"][...], d_inner)
    convnew_ref[0] = xh_ref[pl.ds(8 + tdec - nk, nk), :]
    for j in range(d_inner // 128):
        hnew_ref[0, j * 128:(j + 1) * 128, :] = state_ref[:, j * 128:(j + 1) * 128].T


def _ssd_sample(proj3, small, conv0, h0, params, bs, tdec):
    d_inner = params["expand"].shape[1]
    conv_dim = params["conv_b"].shape[1]
    ns = small.shape[1]
    plist = [params[k] for k in _SSD_PARAM_NAMES]

    def slab(s):
        return pl.BlockSpec((1, tdec, SLAB), lambda b, s=s: (s, b, 0))

    return pl.pallas_call(
        functools.partial(_ssd_sample_kernel, tdec=tdec),
        out_shape=(jax.ShapeDtypeStruct((bs * tdec, d_inner), F32),
                   jax.ShapeDtypeStruct((bs, d_inner, D_STATE), F32),
                   jax.ShapeDtypeStruct((bs, CONV_K - 1, conv_dim), F32)),
        grid=(bs,),
        in_specs=[slab(2), slab(3), slab(4), slab(0), slab(1), pl.BlockSpec((tdec, ns), lambda b: (b, 0)),
                  pl.BlockSpec((1, CONV_K - 1, conv_dim), lambda b: (b, 0, 0)),
                  pl.BlockSpec((1, d_inner, D_STATE), lambda b: (b, 0, 0))] + [_resident(a.shape) for a in plist],
        out_specs=(pl.BlockSpec((tdec, d_inner), lambda b: (b, 0)),
                   pl.BlockSpec((1, d_inner, D_STATE), lambda b: (b, 0, 0)),
                   pl.BlockSpec((1, CONV_K - 1, conv_dim), lambda b: (b, 0, 0))),
        scratch_shapes=[pltpu.VMEM((8 + CHUNK, conv_dim), F32), pltpu.VMEM((D_STATE, d_inner), F32),
                        pltpu.VMEM((CHUNK, 128), F32), pltpu.VMEM((CHUNK, d_inner), F32),
                        pltpu.VMEM((CHUNK, 128), F32)],
        compiler_params=_cparams(("arbitrary",), 40),
        name="ssd_sample",
    )(proj3, proj3, proj3, proj3, proj3, small, conv0, h0, *plist)


def _attn_out(acc, l, wuv_ref, nh, tq):
    o_lat = acc / l
    outs = []
    for p in range(nh // 2):
        pair = jnp.concatenate([o_lat[(2 * p) * tq:(2 * p + 1) * tq], o_lat[(2 * p + 1) * tq:(2 * p + 2) * tq]],
                               axis=1).astype(BF16)
        outs.append(_dot(pair, wuv_ref[p]))
    return jnp.concatenate(outs, axis=1)


def _attn_prompt_kernel(q_ref, k_ref, vt_ref, wuvt_ref, o_ref, m_ref, l_ref, acc_ref, p_ref, *, tk, lp, q_off):
    qi = pl.program_id(1) + q_off
    nh, tq = q_ref.shape[1], q_ref.shape[2]
    hpg = q_ref.shape[1] // m_ref.shape[0]
    ngrp = nh // hpg
    gq = hpg * tq
    m_ref[...] = jnp.full(m_ref.shape, NEG, F32)
    l_ref[...] = jnp.zeros(l_ref.shape, F32)
    qidx = qi * tq + (lax.broadcasted_iota(jnp.int32, (1, gq), 1) & (tq - 1))
    nkc = ((qi + 1) * tq + tk - 1) // tk

    def chunk_start(kc):
        return pl.multiple_of(jnp.minimum(kc * tk, lp - tk), 128)

    def stage(kc, scores, values, diagonal):
        if scores:
            k = k_ref[pl.ds(chunk_start(kc), tk), :]
            kidx = chunk_start(kc) + lax.broadcasted_iota(jnp.int32, (tk, 1), 0)
        if values:
            vt = vt_ref[0, :, pl.ds(chunk_start(kc - 1), tk)]
        for r in range(ngrp):
            if values:
                acc = acc_ref[r] + _dot(vt, p_ref[r])
            else:
                acc = jnp.zeros(acc_ref.shape[1:], F32)
            if scores:
                q = q_ref[0, hpg * r:hpg * (r + 1)].reshape(gq, q_ref.shape[3])
                s = _dot_nt(k, q)
                if diagonal:
                    s = jnp.where((kidx <= qidx) & (kidx >= jnp.maximum(kc * tk, LEAD)), s, NEG)
                else:
                    s = jnp.concatenate([jnp.where(kidx[0:CHUNK] >= LEAD, s[0:CHUNK], NEG), s[CHUNK:]], axis=0)
                m_old = m_ref[r:r + 1, :]
                m_new = jnp.maximum(m_old, jnp.max(s, axis=0, keepdims=True))
                pexp = jnp.exp2(s - m_new)
                alpha = jnp.exp2(m_old - m_new)
                l_ref[r:r + 1, :] = alpha * l_ref[r:r + 1, :] + jnp.sum(pexp, axis=0, keepdims=True)
                p_ref[r] = pexp.astype(BF16)
                m_ref[r:r + 1, :] = m_new
                if values:
                    acc = alpha * acc
            acc_ref[r] = acc

    def body(kc, carry):
        stage(kc, True, True, False)
        return carry

    @pl.when(nkc == 1)
    def _():
        stage(0, True, False, True)

    @pl.when(nkc > 1)
    def _():
        stage(0, True, False, False)
        lax.fori_loop(1, nkc - 1, body, 0)
        stage(nkc - 1, True, True, True)

    stage(nkc, False, True, False)
    for r in range(ngrp):
        o_lat_t = (acc_ref[r] * (1.0 / l_ref[r:r + 1, :])).astype(BF16)
        for j in range(hpg // 2):
            h0 = hpg * r + 2 * j
            pair = jnp.concatenate([_dot(wuvt_ref[h0 + e], o_lat_t[:, (2 * j + e) * tq:(2 * j + e + 1) * tq])
                                    for e in range(2)], axis=0)
            o_ref[:, h0 * V_HEAD:(h0 + 2) * V_HEAD] = pair.T.astype(o_ref.dtype)


def _attn_prompt(q4, kcat, vt, wuvt, bp, lp):
    q_off = 1
    nq = lp // CHUNK - q_off
    nh = MLA_HEADS
    kv_lora = vt.shape[1]
    tk = 512
    hpg = 4
    assert lp % 128 == 0 and lp >= tk and 2 * V_HEAD == 128 and nh % hpg == 0
    return pl.pallas_call(
        functools.partial(_attn_prompt_kernel, tk=tk, lp=lp, q_off=q_off),
        out_shape=jax.ShapeDtypeStruct((bp * nq * CHUNK, nh * V_HEAD), BF16),
        grid=(bp, nq),
        in_specs=[pl.BlockSpec((1, nh, CHUNK, QK_DIM), lambda b, i: (b * nq + i, 0, 0, 0)),
                  pl.BlockSpec((lp, QK_DIM), lambda b, i: (b, 0)),
                  pl.BlockSpec((1, kv_lora, lp), lambda b, i: (b, 0, 0)),
                  _resident(wuvt.shape)],
        out_specs=pl.BlockSpec((CHUNK, nh * V_HEAD), lambda b, i: (b * nq + i, 0)),
        scratch_shapes=[pltpu.VMEM((nh // hpg, hpg * CHUNK), F32), pltpu.VMEM((nh // hpg, hpg * CHUNK), F32),
                        pltpu.VMEM((nh // hpg, kv_lora, hpg * CHUNK), F32),
                        pltpu.VMEM((nh // hpg, tk, hpg * CHUNK), BF16)],
        compiler_params=_cparams(("arbitrary", "arbitrary"), 40),
        name="attn_prompt",
    )(q4, kcat, vt, wuvt)


def _attn_sample_kernel(pt_ref, q_ref, knew_ref, vnew_ref, foldt_ref, wuv_ref, ckv_hbm, krt_hbm, o_ref,
                        cbuf, rbuf, sem, m_ref, l_ref, acc_ref, q_scr, qlt_scr, qpt_scr, *, npg, nchunk, tdec):
    b = pl.program_id(0)
    c = pl.program_id(1)
    nb = pl.num_programs(0)
    step = b * nchunk + c
    slot = step % 2
    nh = q_ref.shape[1]
    rows = nh * tdec
    kv_lora = cbuf.shape[-1]
    page = cbuf.shape[2]

    def copies(bb, cc, sl):
        out = []
        for j in range(npg):
            pg = pt_ref[bb, cc * npg + j]
            out.append(pltpu.make_async_copy(ckv_hbm.at[pg], cbuf.at[sl, j], sem.at[0, sl]))
            out.append(pltpu.make_async_copy(krt_hbm.at[pg], rbuf.at[sl, :, pl.ds(j * page, page)], sem.at[1, sl]))
        return out

    @pl.when(step == 0)
    def _():
        for cp in copies(b, c, slot):
            cp.start()

    @pl.when(step + 1 < nb * nchunk)
    def _():
        nxt = step + 1
        for cp in copies(nxt // nchunk, nxt % nchunk, 1 - slot):
            cp.start()

    @pl.when(c == 0)
    def _():
        m_ref[...] = jnp.full(m_ref.shape, NEG, F32)
        l_ref[...] = jnp.zeros(l_ref.shape, F32)
        acc_ref[...] = jnp.zeros(acc_ref.shape, F32)
        q = q_ref[0].reshape(rows, q_ref.shape[3])
        q_scr[...] = q.astype(BF16)
        for j in range(kv_lora // 128):
            qlt_scr[j * 128:(j + 1) * 128, :] = q[:, j * 128:(j + 1) * 128].T.astype(BF16)
        qpt_scr[...] = lax.dot_general(foldt_ref[...], q[:, kv_lora:kv_lora + 128], (((1,), (1,)), ((), ())),
                                       preferred_element_type=F32, precision=HIGHEST).astype(BF16)

    for cp in copies(b, c, slot):
        cp.wait()

    def lanes_to_rows(x):
        return jnp.broadcast_to(x, (rows, rows)).T

    def widen(x):
        return jnp.concatenate([lanes_to_rows(x)] * (kv_lora // 128), axis=1)

    def update(s_t, v):
        m_old = m_ref[...]
        m_new = jnp.maximum(m_old, jnp.max(s_t, axis=0, keepdims=True))
        p_t = jnp.exp2(s_t - m_new)
        alpha = jnp.exp2(m_old - m_new)
        l_ref[...] = alpha * l_ref[...] + jnp.sum(p_t, axis=0, keepdims=True)
        m_ref[...] = m_new
        nk = s_t.shape[0]
        if nk % 128:
            pmat = p_t.T
        else:
            pmat = jnp.concatenate([p_t[j * 128:(j + 1) * 128, :].T for j in range(nk // 128)], axis=1)
        acc_ref[...] = acc_ref[...] * widen(alpha) + _dot(pmat.astype(BF16), v)

    kc = cbuf[slot].reshape(npg * page, kv_lora).astype(BF16)
    kr = jnp.concatenate([rbuf[slot, :, j * page:(j + 1) * page].T for j in range(npg)],
                         axis=0).astype(BF16)
    update(_dot(kc, qlt_scr[...]) + _dot(kr, qpt_scr[...]), kc)

    @pl.when(c == nchunk - 1)
    def _():
        knew = knew_ref[...].astype(BF16)
        s_t = _dot_nt(knew, q_scr[...])
        nk = knew.shape[0]
        kid = lax.broadcasted_iota(jnp.int32, (nk, 1), 0)
        t = lax.broadcasted_iota(jnp.int32, (1, rows), 1) & (tdec - 1)
        ok = (kid >= b * tdec) & (kid <= b * tdec + t)
        update(jnp.where(ok, s_t, NEG), vnew_ref[...].astype(BF16))
        o_ref[...] = _attn_out(acc_ref[...], widen(l_ref[...]), wuv_ref, nh, tdec)


def _attn_sample(page_table, q4, kcat_new, v_new, fold, wuv, cache_ckv, cache_krt, bs, tdec):
    nh = MLA_HEADS
    n_pages = page_table.shape[1]
    page, kv_lora = cache_ckv.shape[1], cache_ckv.shape[2]
    rope = cache_krt.shape[1]
    npg = 64
    while n_pages % npg:
        npg //= 2
    nchunk = n_pages // npg
    rows = nh * tdec
    assert tdec & (tdec - 1) == 0 and rows == 128 and page == 128
    grid_spec = pltpu.PrefetchScalarGridSpec(
        num_scalar_prefetch=1,
        grid=(bs, nchunk),
        in_specs=[pl.BlockSpec((1, nh, tdec, QK_DIM), lambda b, c, pt: (b, 0, 0, 0)),
                  pl.BlockSpec(kcat_new.shape, lambda b, c, pt: (0, 0), pipeline_mode=pl.Buffered(1)),
                  pl.BlockSpec(v_new.shape, lambda b, c, pt: (0, 0), pipeline_mode=pl.Buffered(1)),
                  pl.BlockSpec(fold.shape, lambda b, c, pt: (0, 0), pipeline_mode=pl.Buffered(1)),
                  pl.BlockSpec(wuv.shape, lambda b, c, pt: (0, 0, 0), pipeline_mode=pl.Buffered(1)),
                  pl.BlockSpec(memory_space=pl.ANY), pl.BlockSpec(memory_space=pl.ANY)],
        out_specs=pl.BlockSpec((tdec, nh * V_HEAD), lambda b, c, pt: (b, 0)),
        scratch_shapes=[pltpu.VMEM((2, npg, page, kv_lora), F32), pltpu.VMEM((2, rope, npg * page), F32),
                        pltpu.SemaphoreType.DMA((2, 2)),
                        pltpu.VMEM((1, rows), F32), pltpu.VMEM((1, rows), F32), pltpu.VMEM((rows, kv_lora), F32),
                        pltpu.VMEM((rows, QK_DIM), BF16), pltpu.VMEM((kv_lora, rows), BF16),
                        pltpu.VMEM((rope, rows), BF16)],
    )
    return pl.pallas_call(
        functools.partial(_attn_sample_kernel, npg=npg, nchunk=nchunk, tdec=tdec),
        out_shape=jax.ShapeDtypeStruct((bs * tdec, nh * V_HEAD), F32),
        grid_spec=grid_spec,
        compiler_params=_cparams(("arbitrary", "arbitrary"), 40),
        name="attn_sample",
    )(page_table, q4, kcat_new, v_new, fold, wuv, cache_ckv, cache_krt)


def _merge_kernel(yn_ref, o_ref, ga_ref, gb_ref, h_ref, wa_ref, wb_ref, wo_ref, g_ref, out_ref):
    a = _dot(yn_ref[...].astype(BF16), wa_ref[...])
    bb = _dot(o_ref[...].astype(BF16), wb_ref[...])
    merged = jax.nn.sigmoid(ga_ref[0].astype(F32)) * a + jax.nn.sigmoid(gb_ref[0].astype(F32)) * bb
    mix = _dot(merged.astype(BF16), wo_ref[...])
    out_ref[...] = h_ref[...] + _rms(mix, g_ref[...])


def _merge(yn, o, proj3, h, wa, wb, wo, g):
    m, d = yn.shape[0], h.shape[1]
    tm = _row_tile(m, 512)
    return pl.pallas_call(
        _merge_kernel,
        out_shape=jax.ShapeDtypeStruct((m, d), F32),
        grid=(m // tm,),
        in_specs=[pl.BlockSpec((tm, yn.shape[1]), lambda i: (i, 0)), pl.BlockSpec((tm, o.shape[1]), lambda i: (i, 0)),
                  pl.BlockSpec((1, tm, SLAB), lambda i: (5, i, 0)), pl.BlockSpec((1, tm, SLAB), lambda i: (6, i, 0)),
                  pl.BlockSpec((tm, d), lambda i: (i, 0)),
                  _resident(wa.shape), _resident(wb.shape), _resident(wo.shape), _resident((1, d))],
        out_specs=pl.BlockSpec((tm, d), lambda i: (i, 0)),
        compiler_params=_cparams(("arbitrary",), 40),
        name="gated_merge",
    )(yn, o, proj3, proj3, h, wa, wb, wo, g)


def _prep_weights(w, d_inner, conv_dim, nheads, q_lora, kv_lora):
    d = w["w_in"].shape[0]
    offs = np.cumsum([0, d_inner, conv_dim, nheads, q_lora, kv_lora, QK_ROPE, d, d])
    seg = {k: w["w_in"][:, offs[i]:offs[i + 1]]
           for i, k in enumerate(("z", "xbc", "dt", "q_a", "kv_a", "k_pe", "ga", "gb"))}
    pad = SLAB - q_lora - kv_lora
    wbig = jnp.concatenate([seg["z"], seg["xbc"], seg["ga"], seg["gb"], seg["q_a"], seg["kv_a"],
                            jnp.zeros((d, pad), F32)], axis=1).astype(BF16)
    swap = np.concatenate([np.arange(QK_ROPE // 2, QK_ROPE), np.arange(QK_ROPE // 2)])
    wsmall = jnp.concatenate([seg["dt"], jnp.zeros((d, 128 - nheads), F32), jnp.tile(seg["k_pe"], (1, 4)),
                              jnp.tile(seg["k_pe"][:, swap], (1, 4))], axis=1).astype(BF16)
    nh = MLA_HEADS
    wq3 = w["w_q_b"].reshape(q_lora, nh, QK_NOPE + QK_ROPE)
    wq = jnp.concatenate([wq3[:, :, :QK_NOPE].reshape(q_lora, nh * QK_NOPE),
                          wq3[:, :, QK_NOPE:].reshape(q_lora, nh * QK_ROPE),
                          wq3[:, :, QK_NOPE:][:, :, swap].reshape(q_lora, nh * QK_ROPE)], axis=1).astype(BF16)
    wk = jnp.transpose(w["w_uk"], (1, 2, 0)).reshape(nh // 2, 2, QK_NOPE, kv_lora)
    zk = jnp.zeros((nh // 2, QK_NOPE, kv_lora), F32)
    wuk = jnp.concatenate([jnp.concatenate([wk[:, 0], zk], axis=2),
                           jnp.concatenate([zk, wk[:, 1]], axis=2)], axis=1).astype(BF16)
    wv = jnp.transpose(w["w_uv"], (1, 0, 2)).reshape(nh // 2, 2, kv_lora, V_HEAD)
    zv = jnp.zeros((nh // 2, kv_lora, V_HEAD), F32)
    wuvt = jnp.transpose(w["w_uv"], (1, 2, 0)).astype(BF16)
    wuv = jnp.concatenate([jnp.concatenate([wv[:, 0], zv], axis=2),
                           jnp.concatenate([zv, wv[:, 1]], axis=2)], axis=1).astype(BF16)
    def expand3(width):
        e = np.kron(np.eye(128, nheads, dtype=np.float32), np.ones((1, width), np.float32))
        return jnp.asarray(np.tile(e, (3, 1)), BF16)

    lane_pad = lambda a: jnp.pad(a, (0, 128 - nheads))[None]
    ssd = dict(conv_w=w["conv_w"], conv_b=w["conv_b"][None], dt_bias=lane_pad(w["dt_bias"]),
               a_log=lane_pad(w["a_log"]),
               dskip_x=jnp.repeat(w["d_skip"], SSM_HEAD_DIM)[None], norm_g=w["ssm_norm_g"][None],
               tril=jnp.asarray(np.tril(np.ones((CHUNK, CHUNK), np.float32))),
               expand=expand3(SSM_HEAD_DIM), expand128=expand3(128),
               shift=jnp.asarray(np.concatenate([np.eye(CHUNK, k=k - (CONV_K - 1), dtype=np.float32)
                                                 for k in range(CONV_K - 1)], axis=0), BF16))
    inv = ROPE_THETA ** (-jnp.arange(0, QK_ROPE, 2, dtype=F32) / QK_ROPE)
    inv128 = jnp.tile(jnp.concatenate([inv, inv]), 4)[None]
    sign128 = jnp.asarray(np.tile(np.concatenate([-np.ones(QK_ROPE // 2), np.ones(QK_ROPE // 2)]), 4)[None], F32)
    foldt = jnp.asarray(np.tile(np.eye(QK_ROPE, dtype=np.float32), (1, 4)))
    bf = lambda a: a.astype(BF16)
    row = lambda a: a[None]
    return dict(
        wbig=wbig, wsmall=wsmall, wq=wq, wuk=wuk, wuv=wuv, wuvt=wuvt, ssd=ssd, inv128=inv128, sign128=sign128, foldt=foldt,
        ffn1=(row(w["ffn1_pre_g"]), bf(w["ffn1_w_gate"]), bf(w["ffn1_w_up"]), bf(w["ffn1_w_down"]),
              row(w["ffn1_post_g"])),
        ffn2=(row(w["ffn2_pre_g"]), bf(w["ffn2_w_gate"]), bf(w["ffn2_w_up"]), bf(w["ffn2_w_down"]),
              row(w["ffn2_post_g"])),
        mix_pre_g=row(w["mix_pre_g"]), mix_post_g=row(w["mix_post_g"]), q_g=row(w["q_a_norm_g"]),
        kv_g=row(w["kv_a_norm_g"]), wa=bf(w["w_a_out"]), wb=bf(w["w_b_out"]), wo=bf(w["w_o"]))


def kernel(x_prompt, x_sample, cache_kv_latent, cache_k_rope, state_ssm, state_conv, page_table, meta_tokens,
           ffn1_pre_g, ffn1_w_gate, ffn1_w_up, ffn1_w_down, ffn1_post_g,
           mix_pre_g, w_in, conv_w, conv_b, dt_bias, a_log, d_skip, ssm_norm_g,
           q_a_norm_g, w_q_b, kv_a_norm_g, w_uk, w_uv, w_a_out, w_b_out, w_o, mix_post_g,
           ffn2_pre_g, ffn2_w_gate, ffn2_w_up, ffn2_w_down, ffn2_post_g):
    names = ("ffn1_pre_g", "ffn1_w_gate", "ffn1_w_up", "ffn1_w_down", "ffn1_post_g", "mix_pre_g", "w_in", "conv_w",
             "conv_b", "dt_bias", "a_log", "d_skip", "ssm_norm_g", "q_a_norm_g", "w_q_b", "kv_a_norm_g", "w_uk",
             "w_uv", "w_a_out", "w_b_out", "w_o", "mix_post_g", "ffn2_pre_g", "ffn2_w_gate", "ffn2_w_up",
             "ffn2_w_down", "ffn2_post_g")
    stacked = dict(zip(names, (ffn1_pre_g, ffn1_w_gate, ffn1_w_up, ffn1_w_down, ffn1_post_g, mix_pre_g, w_in,
                               conv_w, conv_b, dt_bias, a_log, d_skip, ssm_norm_g, q_a_norm_g, w_q_b, kv_a_norm_g,
                               w_uk, w_uv, w_a_out, w_b_out, w_o, mix_post_g, ffn2_pre_g, ffn2_w_gate, ffn2_w_up,
                               ffn2_w_down, ffn2_post_g)))
    depth = w_in.shape[0]
    bp, seq, d = x_prompt.shape
    bs, tdec, _ = x_sample.shape
    nheads = dt_bias.shape[1]
    d_inner = nheads * SSM_HEAD_DIM
    conv_dim = conv_b.shape[1]
    q_lora, kv_lora = q_a_norm_g.shape[1], kv_a_norm_g.shape[1]
    assert seq % CHUNK == 0 and conv_dim == 3 * SLAB and d_inner == 2 * SLAB and d == SLAB
    assert q_lora + kv_lora <= SLAB and CHUNK % tdec == 0
    n_pages, page = page_table.shape[1], cache_kv_latent.shape[2]
    past_len = n_pages * page
    lp = LEAD + N_META + seq
    nc = lp // CHUNK
    ns_rows = bs * tdec
    assert depth == 1 and ns_rows % CHUNK == 0
    meta_tile = ns_rows // CHUNK
    meta0 = ns_rows + LEAD

    hx = x_prompt.reshape(bp * seq, d)
    hs = jnp.concatenate([x_sample.reshape(ns_rows, d), jnp.zeros((LEAD, d), F32), meta_tokens], axis=0)
    pos_x = jnp.tile(N_META + jnp.arange(seq, dtype=jnp.int32), bp).astype(F32)[:, None]
    pos_s = jnp.concatenate([jnp.tile(past_len + jnp.arange(tdec, dtype=jnp.int32), bs),
                             jnp.arange(CHUNK, dtype=jnp.int32) - LEAD]).astype(F32)[:, None]

    outs = [[] for _ in range(8)]
    for l in range(depth):
        w = _prep_weights({k: v[l] for k, v in stacked.items()}, d_inner, conv_dim, nheads, q_lora, kv_lora)

        g1 = _ffn(hs, *w["ffn1"])
        proj3s, smalls = _inproj(g1, w["mix_pre_g"], w["wbig"], w["wsmall"], F32)
        c_kv_s, kpe_s, kcat_s, q4s = _qkv_prep(proj3s, smalls, pos_s, w["q_g"], w["kv_g"], w["wq"], w["wuk"],
                                               w["inv128"], w["sign128"], tdec, F32)

        h1 = _ffn(hx, *w["ffn1"])
        proj3, small = _inproj(h1, w["mix_pre_g"], w["wbig"], w["wsmall"], BF16)
        yn, h_fin, conv_new = _ssd_prompt(proj3, small, proj3s, smalls, meta_tile, w["ssd"], bp, nc)
        c_kv, kpe, kcat, q4 = _qkv_prep(proj3, small, pos_x, w["q_g"], w["kv_g"], w["wq"], w["wuk"], w["inv128"],
                                        w["sign128"], CHUNK, BF16)

        def with_meta(tile, rows):
            f = rows.shape[1]
            return jnp.concatenate([jnp.broadcast_to(tile.astype(rows.dtype)[None], (bp,) + tile.shape),
                                    rows.reshape(bp, seq, f)], axis=1)

        k_all = with_meta(kcat_s[ns_rows:], kcat).reshape(bp * lp, QK_DIM)
        vt = jnp.transpose(k_all.reshape(bp, lp, QK_DIM)[:, :, :kv_lora], (0, 2, 1))
        o = _attn_prompt(q4, k_all, vt, w["wuvt"], bp, lp)
        h2 = _merge(yn, o, proj3, h1, w["wa"], w["wb"], w["wo"], w["mix_post_g"])
        hx = _ffn(h2, *w["ffn2"])
        outs[0].append(with_meta(c_kv_s[meta0:], c_kv))
        outs[1].append(with_meta(kpe_s[meta0:, :QK_ROPE], kpe[:, :QK_ROPE]))
        outs[2].append(h_fin.reshape(bp, nheads, SSM_HEAD_DIM, D_STATE))
        outs[3].append(conv_new)

        yns, h_new, conv_new_s = _ssd_sample(proj3s, smalls, state_conv[l],
                                             state_ssm[l].reshape(bs, d_inner, D_STATE), w["ssd"], bs, tdec)
        os_ = _attn_sample(page_table, q4s, kcat_s[:ns_rows], c_kv_s[:ns_rows], w["foldt"], w["wuv"],
                           cache_kv_latent[l], jnp.transpose(cache_k_rope[l], (0, 2, 1)), bs, tdec)
        g2 = _merge(yns, os_, proj3s, g1, w["wa"], w["wb"], w["wo"], w["mix_post_g"])
        hs = _ffn(g2, *w["ffn2"])
        outs[4].append(c_kv_s[:ns_rows].reshape(bs, tdec, kv_lora))
        outs[5].append(kpe_s[:ns_rows, :QK_ROPE].reshape(bs, tdec, QK_ROPE))
        outs[6].append(h_new.reshape(bs, nheads, SSM_HEAD_DIM, D_STATE))
        outs[7].append(conv_new_s)

    y_prompt = hx.reshape(bp, seq, d)
    y_sample = hs.reshape(bs, tdec, d)
    return (y_prompt, y_sample) + tuple(jnp.stack(o) for o in outs)
```

```python
import functools
import math

import numpy as np
import jax
import jax.numpy as jnp
from jax import lax
from jax.experimental import pallas as pl
from jax.experimental.pallas import tpu as pltpu

F32 = jnp.float32
BF16 = jnp.bfloat16
HIGHEST = lax.Precision.HIGHEST

EPS = 1e-6
N_META = 16
CHUNK = 128
LEAD = (-N_META) % CHUNK
SSM_HEAD_DIM = 64
SSM_GROUPS = 4
D_STATE = 128
CONV_K = 4
MLA_HEADS = 16
QK_NOPE = 64
QK_ROPE = 32
V_HEAD = 64
ROPE_THETA = 10000.0
ATTN_SCALE = (QK_NOPE + QK_ROPE) ** -0.5
Q_SCALE = ATTN_SCALE * math.log2(math.e)
NEG = -1e30
SLAB = 1024
QK_DIM = 384
MIB = 1024 * 1024


def _cparams(sem, vmem_mib):
    return pltpu.CompilerParams(dimension_semantics=sem, vmem_limit_bytes=int(vmem_mib * MIB))


def _resident(shape):
    nd = len(shape)
    return pl.BlockSpec(shape, lambda *_: (0,) * nd, pipeline_mode=pl.Buffered(1))


def _rms(x, g):
    return x * lax.rsqrt(jnp.mean(x * x, axis=-1, keepdims=True) + EPS) * g


def _dot(a, b):
    return jnp.dot(a, b, preferred_element_type=F32)


def _dot_exact(a, b):
    return jnp.dot(a, b, preferred_element_type=F32, precision=HIGHEST)


def _dot_nt(a, b):
    return lax.dot_general(a, b, (((1,), (1,)), ((), ())), preferred_element_type=F32)


def _expand_exact(x, e3):
    hi = x.astype(BF16)
    r1 = x - hi.astype(F32)
    mid = r1.astype(BF16)
    lo = (r1 - mid.astype(F32)).astype(BF16)
    return _dot(jnp.concatenate([hi, mid, lo], axis=1), e3)


def _softplus(x):
    return jnp.maximum(x, 0.0) + jnp.log1p(jnp.exp(-jnp.abs(x)))


def _row_tile(m, cap, mult=16):
    for t in range(min(cap, m), mult - 1, -1):
        if m % t == 0 and t % mult == 0:
            return t
    raise ValueError(f"no row tile for {m}")


def _ffn_kernel(x_ref, pre_ref, wg_ref, wu_ref, wd_ref, post_ref, o_ref, *, chunks):
    x = x_ref[...]
    xn = _rms(x, pre_ref[...]).astype(BF16)
    acc = jnp.zeros(x.shape, F32)
    for lo, sz in chunks:
        g = _dot(xn, wg_ref[:, lo:lo + sz])
        u = _dot(xn, wu_ref[:, lo:lo + sz])
        h = (jax.nn.silu(g) * u).astype(BF16)
        acc = acc + _dot(h, wd_ref[lo:lo + sz, :])
    o_ref[...] = x + 0.5 * _rms(acc, post_ref[...])


def _ffn_chunks(f):
    out, lo = [], 0
    while lo < f:
        sz = min(1024, f - lo)
        out.append((lo, sz))
        lo += sz
    return tuple(out)


def _ffn_stream_kernel(x_ref, pre_ref, wg_ref, wu_ref, wd_ref, post_ref, o_ref, xn_ref, acc_ref):
    j = pl.program_id(0)

    @pl.when(j == 0)
    def _():
        xn_ref[...] = _rms(x_ref[...], pre_ref[...]).astype(BF16)
        acc_ref[...] = jnp.zeros(acc_ref.shape, F32)

    xn = xn_ref[...]
    h = (jax.nn.silu(_dot(xn, wg_ref[...])) * _dot(xn, wu_ref[...])).astype(BF16)
    acc_ref[...] += _dot(h, wd_ref[...])

    @pl.when(j == pl.num_programs(0) - 1)
    def _():
        o_ref[...] = x_ref[...] + 0.5 * _rms(acc_ref[...], post_ref[...])


def _ffn(x, pre_g, wg, wu, wd, post_g):
    m, d = x.shape
    f = wg.shape[1]
    tm = _row_tile(m, 512)
    ft = 256
    if tm == m and f % ft == 0:
        return pl.pallas_call(
            _ffn_stream_kernel,
            out_shape=jax.ShapeDtypeStruct((m, d), F32),
            grid=(f // ft,),
            in_specs=[pl.BlockSpec((m, d), lambda j: (0, 0)), pl.BlockSpec((1, d), lambda j: (0, 0)),
                      pl.BlockSpec((d, ft), lambda j: (0, j)), pl.BlockSpec((d, ft), lambda j: (0, j)),
                      pl.BlockSpec((ft, d), lambda j: (j, 0)), pl.BlockSpec((1, d), lambda j: (0, 0))],
            out_specs=pl.BlockSpec((m, d), lambda j: (0, 0)),
            scratch_shapes=[pltpu.VMEM((m, d), BF16), pltpu.VMEM((m, d), F32)],
            compiler_params=_cparams(("arbitrary",), 32),
            name="ffn_block_stream",
        )(x, pre_g, wg, wu, wd, post_g)
    return pl.pallas_call(
        functools.partial(_ffn_kernel, chunks=_ffn_chunks(f)),
        out_shape=jax.ShapeDtypeStruct((m, d), F32),
        grid=(m // tm,),
        in_specs=[pl.BlockSpec((tm, d), lambda i: (i, 0)), _resident((1, d)), _resident((d, f)),
                  _resident((d, f)), _resident((f, d)), _resident((1, d))],
        out_specs=pl.BlockSpec((tm, d), lambda i: (i, 0)),
        compiler_params=_cparams(("arbitrary",), 52),
        name="ffn_block",
    )(x, pre_g, wg, wu, wd, post_g)


def _inproj_kernel(x_ref, g_ref, wbig_ref, wsmall_ref, proj_ref, small_ref, *, nslab):
    xn = _rms(x_ref[...], g_ref[...]).astype(BF16)
    for s in range(nslab):
        proj_ref[s] = _dot(xn, wbig_ref[:, s * SLAB:(s + 1) * SLAB]).astype(proj_ref.dtype)
    small_ref[...] = _dot(xn, wsmall_ref[...])


def _inproj_stream_kernel(x_ref, g_ref, wbig_ref, wsmall_ref, proj_ref, small_ref, xn_ref):
    @pl.when(pl.program_id(0) == 0)
    def _():
        xn_ref[...] = _rms(x_ref[...], g_ref[...]).astype(BF16)
        small_ref[...] = _dot(xn_ref[...], wsmall_ref[...])

    proj_ref[0] = _dot(xn_ref[...], wbig_ref[...]).astype(proj_ref.dtype)


def _inproj(h, g, wbig, wsmall, proj_dtype):
    m, d = h.shape
    nslab = wbig.shape[1] // SLAB
    ns = wsmall.shape[1]
    tm = _row_tile(m, 512)
    if tm == m:
        return pl.pallas_call(
            _inproj_stream_kernel,
            out_shape=(jax.ShapeDtypeStruct((nslab, m, SLAB), proj_dtype), jax.ShapeDtypeStruct((m, ns), F32)),
            grid=(nslab,),
            in_specs=[pl.BlockSpec((m, d), lambda s: (0, 0)), pl.BlockSpec((1, d), lambda s: (0, 0)),
                      pl.BlockSpec((d, SLAB), lambda s: (0, s)), pl.BlockSpec(wsmall.shape, lambda s: (0, 0))],
            out_specs=(pl.BlockSpec((1, m, SLAB), lambda s: (s, 0, 0)), pl.BlockSpec((m, ns), lambda s: (0, 0))),
            scratch_shapes=[pltpu.VMEM((m, d), BF16)],
            compiler_params=_cparams(("arbitrary",), 32),
            name="in_proj_stream",
        )(h, g, wbig, wsmall)
    return pl.pallas_call(
        functools.partial(_inproj_kernel, nslab=nslab),
        out_shape=(jax.ShapeDtypeStruct((nslab, m, SLAB), proj_dtype), jax.ShapeDtypeStruct((m, ns), F32)),
        grid=(m // tm,),
        in_specs=[pl.BlockSpec((tm, d), lambda i: (i, 0)), _resident((1, d)), _resident(wbig.shape),
                  _resident(wsmall.shape)],
        out_specs=(pl.BlockSpec((nslab, tm, SLAB), lambda i: (0, i, 0)), pl.BlockSpec((tm, ns), lambda i: (i, 0))),
        compiler_params=_cparams(("arbitrary",), 52),
        name="in_proj",
    )(h, g, wbig, wsmall)


def _qkv_kernel(s7_ref, small_ref, pos_ref, qg_ref, kvg_ref, wq_ref, wuk_ref, inv_ref, sign_ref,
                ckv_ref, kpe_ref, kcat_ref, q_ref, *, tqb):
    s7 = s7_ref[0].astype(F32)
    tm = s7.shape[0]
    q_lora = qg_ref.shape[1]
    kv_lora = kvg_ref.shape[1]
    nh = MLA_HEADS
    c_kv = _rms(s7[:, q_lora:q_lora + kv_lora], kvg_ref[...])
    ckv_ref[...] = c_kv
    ang = pos_ref[...] * inv_ref[...]
    cos = jnp.cos(ang)
    sin = jnp.sin(ang) * sign_ref[...]
    small = small_ref[...]
    kpe = small[:, 128:256] * cos + small[:, 256:384] * sin
    kpe_ref[...] = kpe
    kcat_ref[:, 0:kv_lora] = c_kv.astype(kcat_ref.dtype)
    kcat_ref[:, kv_lora:kv_lora + 128] = kpe.astype(kcat_ref.dtype)

    qn = _rms(s7[:, 0:q_lora], qg_ref[...] * Q_SCALE).astype(BF16)
    nope_w = nh * QK_NOPE
    pe_w = nh * QK_ROPE
    q_nope = _dot(qn, wq_ref[:, 0:nope_w])
    q_pe = _dot(qn, wq_ref[:, nope_w:nope_w + pe_w])
    q_rot = _dot(qn, wq_ref[:, nope_w + pe_w:nope_w + 2 * pe_w])
    cos_w = jnp.concatenate([cos] * (pe_w // 128), axis=1)
    sin_w = jnp.concatenate([sin] * (pe_w // 128), axis=1)
    q_pe = q_pe * cos_w + q_rot * sin_w
    lane = lax.broadcasted_iota(jnp.int32, (1, 128), 1)
    nblk = tm // tqb
    for p in range(nh // 2):
        qn_pair = q_nope[:, p * 128:(p + 1) * 128].astype(BF16)
        q_lat = _dot(qn_pair, wuk_ref[p])
        for e in range(2):
            hd = 2 * p + e
            pe_blk = q_pe[:, (hd // 4) * 128:(hd // 4 + 1) * 128]
            sel = (lane >= (hd % 4) * QK_ROPE) & (lane < (hd % 4 + 1) * QK_ROPE)
            pe_blk = jnp.where(sel, pe_blk, 0.0)
            lat = q_lat[:, e * kv_lora:(e + 1) * kv_lora]
            for g in range(nblk):
                q_ref[g, hd, :, 0:kv_lora] = lat[g * tqb:(g + 1) * tqb].astype(q_ref.dtype)
                q_ref[g, hd, :, kv_lora:kv_lora + 128] = pe_blk[g * tqb:(g + 1) * tqb].astype(q_ref.dtype)


def _qkv_prep(proj3, small, pos, qg, kvg, wq, wuk, inv128, sign128, tqb, dtype):
    nslab, m, _ = proj3.shape
    kv_lora = kvg.shape[1]
    tm = _row_tile(m, 512, max(16, tqb))
    nblk = tm // tqb
    return pl.pallas_call(
        functools.partial(_qkv_kernel, tqb=tqb),
        out_shape=(jax.ShapeDtypeStruct((m, kv_lora), F32), jax.ShapeDtypeStruct((m, 128), F32),
                   jax.ShapeDtypeStruct((m, QK_DIM), dtype),
                   jax.ShapeDtypeStruct((m // tqb, MLA_HEADS, tqb, QK_DIM), dtype)),
        grid=(m // tm,),
        in_specs=[pl.BlockSpec((1, tm, SLAB), lambda i: (nslab - 1, i, 0)),
                  pl.BlockSpec((tm, small.shape[1]), lambda i: (i, 0)),
                  pl.BlockSpec((tm, 1), lambda i: (i, 0)),
                  _resident(qg.shape), _resident(kvg.shape), _resident(wq.shape), _resident(wuk.shape),
                  _resident((1, 128)), _resident((1, 128))],
        out_specs=(pl.BlockSpec((tm, kv_lora), lambda i: (i, 0)), pl.BlockSpec((tm, 128), lambda i: (i, 0)),
                   pl.BlockSpec((tm, QK_DIM), lambda i: (i, 0)),
                   pl.BlockSpec((nblk, MLA_HEADS, tqb, QK_DIM), lambda i: (i, 0, 0, 0))),
        compiler_params=_cparams(("arbitrary",), 48),
        name="qkv_prep",
    )(proj3, small, pos, qg, kvg, wq, wuk, inv128, sign128)


def _ssd_chunk(xh_ref, dt_raw, state_ref, apad_ref, y_ref, lo, hi, p, bf16_input, rows_out=CHUNK):
    d_inner = state_ref.shape[1]
    nheads = d_inner // SSM_HEAD_DIM
    gw = d_inner // SSM_GROUPS
    rows = lax.broadcasted_iota(jnp.int32, (CHUNK, 1), 0)
    valid = (rows >= lo) & (rows < hi)
    ntap = CONV_K - 1

    def conv_rows(nrows):
        out = p["conv_b"][...]
        for k in range(CONV_K):
            out = out + p["conv_w"][k:k + 1, :] * xh_ref[pl.ds(8 - ntap + k, nrows), :]
        return out

    if bf16_input:
        cur = xh_ref[8:8 + CHUNK, :]
        shifted = _dot(p["shift"][...], cur.astype(BF16))
        conv = p["conv_b"][...]
        for k in range(ntap):
            conv = conv + p["conv_w"][k:k + 1, :] * shifted[k * CHUNK:(k + 1) * CHUNK]
        conv = conv + p["conv_w"][ntap:CONV_K, :] * cur
        conv = jnp.concatenate([conv_rows(8), conv[8:]], axis=0)
    else:
        conv = conv_rows(CHUNK)
    xc = jnp.where(valid, jax.nn.silu(conv), 0.0)
    xs = xc[:, 0:d_inner]
    bm = xc[:, d_inner:d_inner + SSM_GROUPS * D_STATE]
    cm = xc[:, d_inner + SSM_GROUPS * D_STATE:]

    dt = jnp.where(valid, _softplus(dt_raw + p["dt_bias"][...]), 0.0)
    da = dt * (-jnp.exp(p["a_log"][...]))
    a_cs = _dot_exact(p["tril"][...], da) * math.log2(math.e)
    dt_x = _expand_exact(dt, p["expand"][...])
    acs_x = _expand_exact(a_cs, p["expand"][...])
    alast_x = acs_x[CHUNK - 1:CHUNK, :]
    acs_b = _expand_exact(a_cs, p["expand128"][...])
    apad_ref[...] = a_cs.T
    xdt = xs * dt_x
    xds = (xdt * jnp.exp2(alast_x - acs_x)).astype(BF16)
    exp_acs = jnp.exp2(acs_x)
    causal = rows >= lax.broadcasted_iota(jnp.int32, (1, CHUNK), 1)
    lane = lax.broadcasted_iota(jnp.int32, (1, 128), 1)
    hpg = nheads // SSM_GROUPS

    ro = rows_out
    for g in range(SSM_GROUPS):
        gs = slice(g * gw, (g + 1) * gw)
        bg = bm[:, g * D_STATE:(g + 1) * D_STATE]
        cg = cm[0:ro, g * D_STATE:(g + 1) * D_STATE].astype(BF16)
        cb = _dot_nt(cg, bg.astype(BF16))
        st = state_ref[:, gs]
        y_off = _dot(cg, st.astype(BF16)) * exp_acs[0:ro, gs]
        state_ref[:, gs] = st * jnp.exp2(alast_x[:, gs]) + _dot(bg.T.astype(BF16), xds[:, gs])
        y_ref[0:ro, gs] = y_off + xs[0:ro, gs] * p["dskip_x"][:, gs]
        for pr in range(hpg // 2):
            w2 = []
            for e in range(2):
                hd = g * hpg + 2 * pr + e
                seg = acs_b[0:ro, hd * 128:(hd + 1) * 128] - apad_ref[pl.ds(hd, 1), :]
                decay = jnp.exp2(jnp.where(causal[0:ro], seg, NEG))
                w2.append((cb * decay).astype(BF16))
            ls = slice(g * gw + pr * 128, g * gw + (pr + 1) * 128)
            xp = xdt[:, ls]
            x2 = jnp.concatenate([jnp.where(lane < SSM_HEAD_DIM, xp, 0.0),
                                  jnp.where(lane >= SSM_HEAD_DIM, xp, 0.0)], axis=0).astype(BF16)
            y_ref[0:ro, ls] = y_ref[0:ro, ls] + _dot(jnp.concatenate(w2, axis=1), x2)


def _gate_norm(y, z, g, d_inner):
    yg = y * jax.nn.silu(z)
    gw = d_inner // SSM_GROUPS
    outs = []
    for k in range(SSM_GROUPS):
        v = yg[:, k * gw:(k + 1) * gw]
        outs.append(_rms(v, g[:, k * gw:(k + 1) * gw]))
    return jnp.concatenate(outs, axis=1)


_SSD_PARAM_NAMES = ("conv_w", "conv_b", "dt_bias", "a_log", "dskip_x", "norm_g", "tril", "expand", "expand128",
                    "shift")


def _ssd_prompt_kernel(x0_ref, x1_ref, bc_ref, z0_ref, z1_ref, small_ref, m0_ref, m1_ref, mbc_ref, msmall_ref,
                       *rest, nc):
    np_ = len(_SSD_PARAM_NAMES)
    p = dict(zip(_SSD_PARAM_NAMES, rest[:np_]))
    yn_ref, hfin_ref, convnew_ref, xh_ref, state_ref, apad_ref, y_ref, state0_ref, tail0_ref = rest[np_:]
    b = pl.program_id(0)
    c = pl.program_id(1)
    d_inner = state_ref.shape[1]
    assert x0_ref.dtype == BF16
    shared_first = (c == 0) & (b == 0)

    @pl.when(shared_first)
    def _():
        xh_ref[0:8, :] = jnp.zeros((8, xh_ref.shape[1]), F32)
        state_ref[...] = jnp.zeros(state_ref.shape, F32)
        valid = lax.broadcasted_iota(jnp.int32, (CHUNK, 1), 0) >= LEAD
        for j, ref in enumerate((m0_ref, m1_ref, mbc_ref)):
            xh_ref[8:8 + CHUNK, j * SLAB:(j + 1) * SLAB] = jnp.where(valid, ref[0].astype(BF16).astype(F32), 0.0)

    @pl.when((c == 0) & (b > 0))
    def _():
        state_ref[...] = state0_ref[...]
        xh_ref[0:8, :] = tail0_ref[...]

    @pl.when(c > 0)
    def _():
        for j, ref in enumerate((x0_ref, x1_ref, bc_ref)):
            xh_ref[8:8 + CHUNK, j * SLAB:(j + 1) * SLAB] = ref[0].astype(F32)

    @pl.when((c > 0) | (b == 0))
    def _():
        lo = jnp.where(c == 0, LEAD, 0)
        dt_raw = jnp.where(c == 0, msmall_ref[:, 0:128], small_ref[:, 0:128])
        _ssd_chunk(xh_ref, dt_raw, state_ref, apad_ref, y_ref, lo, CHUNK, p, True)
        z = jnp.concatenate([z0_ref[0], z1_ref[0]], axis=1).astype(F32)
        yn_ref[...] = _gate_norm(y_ref[...], z, p["norm_g"][...], d_inner).astype(yn_ref.dtype)
        xh_ref[0:8, :] = xh_ref[CHUNK:CHUNK + 8, :]

    @pl.when(shared_first)
    def _():
        state0_ref[...] = state_ref[...]
        tail0_ref[...] = xh_ref[0:8, :]

    @pl.when(c == nc - 1)
    def _():
        convnew_ref[0] = xh_ref[pl.ds(8 + CHUNK - (CONV_K - 1), CONV_K - 1), :]
        for j in range(d_inner // 128):
            hfin_ref[0, j * 128:(j + 1) * 128, :] = state_ref[:, j * 128:(j + 1) * 128].T


def _ssd_prompt(proj3, small, proj3m, smallm, meta_tile, params, bp, nc):
    d_inner = params["expand"].shape[1]
    conv_dim = params["conv_b"].shape[1]
    m = proj3.shape[1]
    ns = small.shape[1]
    ncx = nc - 1
    plist = [params[k] for k in _SSD_PARAM_NAMES]

    def xblk(b, c):
        return b * ncx + jnp.maximum(c - 1, 0)

    def slab(s):
        return pl.BlockSpec((1, CHUNK, SLAB), lambda b, c, s=s: (s, xblk(b, c), 0))

    def mslab(s):
        return pl.BlockSpec((1, CHUNK, SLAB), lambda b, c, s=s: (s, meta_tile, 0))

    return pl.pallas_call(
        functools.partial(_ssd_prompt_kernel, nc=nc),
        out_shape=(jax.ShapeDtypeStruct((m, d_inner), BF16),
                   jax.ShapeDtypeStruct((bp, d_inner, D_STATE), F32),
                   jax.ShapeDtypeStruct((bp, CONV_K - 1, conv_dim), F32)),
        grid=(bp, nc),
        in_specs=[slab(2), slab(3), slab(4), slab(0), slab(1),
                  pl.BlockSpec((CHUNK, ns), lambda b, c: (xblk(b, c), 0)),
                  mslab(2), mslab(3), mslab(4), pl.BlockSpec((CHUNK, ns), lambda b, c: (meta_tile, 0))]
                 + [_resident(a.shape) for a in plist],
        out_specs=(pl.BlockSpec((CHUNK, d_inner), lambda b, c: (xblk(b, c), 0)),
                   pl.BlockSpec((1, d_inner, D_STATE), lambda b, c: (b, 0, 0)),
                   pl.BlockSpec((1, CONV_K - 1, conv_dim), lambda b, c: (b, 0, 0))),
        scratch_shapes=[pltpu.VMEM((8 + CHUNK, conv_dim), F32), pltpu.VMEM((D_STATE, d_inner), F32),
                        pltpu.VMEM((CHUNK, 128), F32), pltpu.VMEM((CHUNK, d_inner), F32),
                        pltpu.VMEM((D_STATE, d_inner), F32), pltpu.VMEM((8, conv_dim), F32)],
        compiler_params=_cparams(("arbitrary", "arbitrary"), 40),
        name="ssd_prompt",
    )(proj3, proj3, proj3, proj3, proj3, small, proj3m, proj3m, proj3m, smallm, *plist)


def _ssd_sample_kernel(x0_ref, x1_ref, bc_ref, z0_ref, z1_ref, small_ref, conv0_ref, h0_ref, *rest, tdec):
    np_ = len(_SSD_PARAM_NAMES)
    p = dict(zip(_SSD_PARAM_NAMES, rest[:np_]))
    yn_ref, hnew_ref, convnew_ref, xh_ref, state_ref, apad_ref, y_ref, dt_ref = rest[np_:]
    d_inner = state_ref.shape[1]
    nk = CONV_K - 1
    xh_ref[...] = jnp.zeros(xh_ref.shape, F32)
    xh_ref[8 - nk:8, :] = conv0_ref[0]
    xh_ref[8:8 + tdec, 0:SLAB] = x0_ref[0]
    xh_ref[8:8 + tdec, SLAB:2 * SLAB] = x1_ref[0]
    xh_ref[8:8 + tdec, 2 * SLAB:3 * SLAB] = bc_ref[0]
    dt_ref[...] = jnp.zeros(dt_ref.shape, F32)
    dt_ref[0:tdec, :] = small_ref[:, 0:128]
    for j in range(d_inner // 128):
        state_ref[:, j * 128:(j + 1) * 128] = h0_ref[0, j * 128:(j + 1) * 128, :].T
    _ssd_chunk(xh_ref, dt_ref[...], state_ref, apad_ref, y_ref, 0, tdec, p, False, rows_out=tdec)
    z = jnp.concatenate([z0_ref[0], z1_ref[0]], axis=1)
    yn_ref[...] = _gate_norm(y_ref[0:tdec, :], z, p["norm_g"][...], d_inner)
    convnew_ref[0] = xh_ref[pl.ds(8 + tdec - nk, nk), :]
    for j in range(d_inner // 128):
        hnew_ref[0, j * 128:(j + 1) * 128, :] = state_ref[:, j * 128:(j + 1) * 128].T


def _ssd_sample(proj3, small, conv0, h0, params, bs, tdec):
    d_inner = params["expand"].shape[1]
    conv_dim = params["conv_b"].shape[1]
    ns = small.shape[1]
    plist = [params[k] for k in _SSD_PARAM_NAMES]

    def slab(s):
        return pl.BlockSpec((1, tdec, SLAB), lambda b, s=s: (s, b, 0))

    return pl.pallas_call(
        functools.partial(_ssd_sample_kernel, tdec=tdec),
        out_shape=(jax.ShapeDtypeStruct((bs * tdec, d_inner), F32),
                   jax.ShapeDtypeStruct((bs, d_inner, D_STATE), F32),
                   jax.ShapeDtypeStruct((bs, CONV_K - 1, conv_dim), F32)),
        grid=(bs,),
        in_specs=[slab(2), slab(3), slab(4), slab(0), slab(1), pl.BlockSpec((tdec, ns), lambda b: (b, 0)),
                  pl.BlockSpec((1, CONV_K - 1, conv_dim), lambda b: (b, 0, 0)),
                  pl.BlockSpec((1, d_inner, D_STATE), lambda b: (b, 0, 0))] + [_resident(a.shape) for a in plist],
        out_specs=(pl.BlockSpec((tdec, d_inner), lambda b: (b, 0)),
                   pl.BlockSpec((1, d_inner, D_STATE), lambda b: (b, 0, 0)),
                   pl.BlockSpec((1, CONV_K - 1, conv_dim), lambda b: (b, 0, 0))),
        scratch_shapes=[pltpu.VMEM((8 + CHUNK, conv_dim), F32), pltpu.VMEM((D_STATE, d_inner), F32),
                        pltpu.VMEM((CHUNK, 128), F32), pltpu.VMEM((CHUNK, d_inner), F32),
                        pltpu.VMEM((CHUNK, 128), F32)],
        compiler_params=_cparams(("arbitrary",), 40),
        name="ssd_sample",
    )(proj3, proj3, proj3, proj3, proj3, small, conv0, h0, *plist)


def _attn_out(acc, l, wuv_ref, nh, tq):
    o_lat = acc / l
    outs = []
    for p in range(nh // 2):
        pair = jnp.concatenate([o_lat[(2 * p) * tq:(2 * p + 1) * tq], o_lat[(2 * p + 1) * tq:(2 * p + 2) * tq]],
                               axis=1).astype(BF16)
        outs.append(_dot(pair, wuv_ref[p]))
    return jnp.concatenate(outs, axis=1)


def _attn_prompt_kernel(q_ref, k_ref, vt_ref, wuvt_ref, o_ref, m_ref, l_ref, acc_ref, p_ref, *, tk, lp, q_off):
    qi = pl.program_id(1) + q_off
    nh, tq = q_ref.shape[1], q_ref.shape[2]
    hpg = q_ref.shape[1] // m_ref.shape[0]
    ngrp = nh // hpg
    gq = hpg * tq
    m_ref[...] = jnp.full(m_ref.shape, NEG, F32)
    l_ref[...] = jnp.zeros(l_ref.shape, F32)
    qidx = qi * tq + (lax.broadcasted_iota(jnp.int32, (1, gq), 1) & (tq - 1))
    nkc = ((qi + 1) * tq + tk - 1) // tk

    def chunk_start(kc):
        return pl.multiple_of(jnp.minimum(kc * tk, lp - tk), 128)

    def stage(kc, scores, values, diagonal):
        if scores:
            k = k_ref[pl.ds(chunk_start(kc), tk), :]
            kidx = chunk_start(kc) + lax.broadcasted_iota(jnp.int32, (tk, 1), 0)
        if values:
            vt = vt_ref[0, :, pl.ds(chunk_start(kc - 1), tk)]
        for r in range(ngrp):
            if values:
                acc = acc_ref[r] + _dot(vt, p_ref[r])
            else:
                acc = jnp.zeros(acc_ref.shape[1:], F32)
            if scores:
                q = q_ref[0, hpg * r:hpg * (r + 1)].reshape(gq, q_ref.shape[3])
                s = _dot_nt(k, q)
                if diagonal:
                    s = jnp.where((kidx <= qidx) & (kidx >= jnp.maximum(kc * tk, LEAD)), s, NEG)
                else:
                    s = jnp.concatenate([jnp.where(kidx[0:CHUNK] >= LEAD, s[0:CHUNK], NEG), s[CHUNK:]], axis=0)
                m_old = m_ref[r:r + 1, :]
                m_new = jnp.maximum(m_old, jnp.max(s, axis=0, keepdims=True))
                pexp = jnp.exp2(s - m_new)
                alpha = jnp.exp2(m_old - m_new)
                l_ref[r:r + 1, :] = alpha * l_ref[r:r + 1, :] + jnp.sum(pexp, axis=0, keepdims=True)
                p_ref[r] = pexp.astype(BF16)
                m_ref[r:r + 1, :] = m_new
                if values:
                    acc = alpha * acc
            acc_ref[r] = acc

    def body(kc, carry):
        stage(kc, True, True, False)
        return carry

    @pl.when(nkc == 1)
    def _():
        stage(0, True, False, True)

    @pl.when(nkc > 1)
    def _():
        stage(0, True, False, False)
        lax.fori_loop(1, nkc - 1, body, 0)
        stage(nkc - 1, True, True, True)

    stage(nkc, False, True, False)
    for r in range(ngrp):
        o_lat_t = (acc_ref[r] * (1.0 / l_ref[r:r + 1, :])).astype(BF16)
        for j in range(hpg // 2):
            h0 = hpg * r + 2 * j
            pair = jnp.concatenate([_dot(wuvt_ref[h0 + e], o_lat_t[:, (2 * j + e) * tq:(2 * j + e + 1) * tq])
                                    for e in range(2)], axis=0)
            o_ref[:, h0 * V_HEAD:(h0 + 2) * V_HEAD] = pair.T.astype(o_ref.dtype)


def _attn_prompt(q4, kcat, vt, wuvt, bp, lp):
    q_off = 1
    nq = lp // CHUNK - q_off
    nh = MLA_HEADS
    kv_lora = vt.shape[1]
    tk = 512
    hpg = 4
    assert lp % 128 == 0 and lp >= tk and 2 * V_HEAD == 128 and nh % hpg == 0
    return pl.pallas_call(
        functools.partial(_attn_prompt_kernel, tk=tk, lp=lp, q_off=q_off),
        out_shape=jax.ShapeDtypeStruct((bp * nq * CHUNK, nh * V_HEAD), BF16),
        grid=(bp, nq),
        in_specs=[pl.BlockSpec((1, nh, CHUNK, QK_DIM), lambda b, i: (b * nq + i, 0, 0, 0)),
                  pl.BlockSpec((lp, QK_DIM), lambda b, i: (b, 0)),
                  pl.BlockSpec((1, kv_lora, lp), lambda b, i: (b, 0, 0)),
                  _resident(wuvt.shape)],
        out_specs=pl.BlockSpec((CHUNK, nh * V_HEAD), lambda b, i: (b * nq + i, 0)),
        scratch_shapes=[pltpu.VMEM((nh // hpg, hpg * CHUNK), F32), pltpu.VMEM((nh // hpg, hpg * CHUNK), F32),
                        pltpu.VMEM((nh // hpg, kv_lora, hpg * CHUNK), F32),
                        pltpu.VMEM((nh // hpg, tk, hpg * CHUNK), BF16)],
        compiler_params=_cparams(("arbitrary", "arbitrary"), 40),
        name="attn_prompt",
    )(q4, kcat, vt, wuvt)


def _attn_sample_kernel(pt_ref, q_ref, knew_ref, vnew_ref, foldt_ref, wuv_ref, ckv_hbm, krt_hbm, o_ref,
                        cbuf, rbuf, sem, m_ref, l_ref, acc_ref, q_scr, qlt_scr, qpt_scr, *, npg, nchunk, tdec):
    b = pl.program_id(0)
    c = pl.program_id(1)
    nb = pl.num_programs(0)
    step = b * nchunk + c
    slot = step % 2
    nh = q_ref.shape[1]
    rows = nh * tdec
    kv_lora = cbuf.shape[-1]
    page = cbuf.shape[2]

    def copies(bb, cc, sl):
        out = []
        for j in range(npg):
            pg = pt_ref[bb, cc * npg + j]
            out.append(pltpu.make_async_copy(ckv_hbm.at[pg], cbuf.at[sl, j], sem.at[0, sl]))
            out.append(pltpu.make_async_copy(krt_hbm.at[pg], rbuf.at[sl, :, pl.ds(j * page, page)], sem.at[1, sl]))
        return out

    @pl.when(step == 0)
    def _():
        for cp in copies(b, c, slot):
            cp.start()

    @pl.when(step + 1 < nb * nchunk)
    def _():
        nxt = step + 1
        for cp in copies(nxt // nchunk, nxt % nchunk, 1 - slot):
            cp.start()

    @pl.when(c == 0)
    def _():
        m_ref[...] = jnp.full(m_ref.shape, NEG, F32)
        l_ref[...] = jnp.zeros(l_ref.shape, F32)
        acc_ref[...] = jnp.zeros(acc_ref.shape, F32)
        q = q_ref[0].reshape(rows, q_ref.shape[3])
        q_scr[...] = q.astype(BF16)
        for j in range(kv_lora // 128):
            qlt_scr[j * 128:(j + 1) * 128, :] = q[:, j * 128:(j + 1) * 128].T.astype(BF16)
        qpt_scr[...] = lax.dot_general(foldt_ref[...], q[:, kv_lora:kv_lora + 128], (((1,), (1,)), ((), ())),
                                       preferred_element_type=F32, precision=HIGHEST).astype(BF16)

    for cp in copies(b, c, slot):
        cp.wait()

    def lanes_to_rows(x):
        return jnp.broadcast_to(x, (rows, rows)).T

    def widen(x):
        return jnp.concatenate([lanes_to_rows(x)] * (kv_lora // 128), axis=1)

    def update(s_t, v):
        m_old = m_ref[...]
        m_new = jnp.maximum(m_old, jnp.max(s_t, axis=0, keepdims=True))
        p_t = jnp.exp2(s_t - m_new)
        alpha = jnp.exp2(m_old - m_new)
        l_ref[...] = alpha * l_ref[...] + jnp.sum(p_t, axis=0, keepdims=True)
        m_ref[...] = m_new
        nk = s_t.shape[0]
        if nk % 128:
            pmat = p_t.T
        else:
            pmat = jnp.concatenate([p_t[j * 128:(j + 1) * 128, :].T for j in range(nk // 128)], axis=1)
        acc_ref[...] = acc_ref[...] * widen(alpha) + _dot(pmat.astype(BF16), v)

    kc = cbuf[slot].reshape(npg * page, kv_lora).astype(BF16)
    kr = jnp.concatenate([rbuf[slot, :, j * page:(j + 1) * page].T for j in range(npg)],
                         axis=0).astype(BF16)
    update(_dot(kc, qlt_scr[...]) + _dot(kr, qpt_scr[...]), kc)

    @pl.when(c == nchunk - 1)
    def _():
        knew = knew_ref[...].astype(BF16)
        s_t = _dot_nt(knew, q_scr[...])
        nk = knew.shape[0]
        kid = lax.broadcasted_iota(jnp.int32, (nk, 1), 0)
        t = lax.broadcasted_iota(jnp.int32, (1, rows), 1) & (tdec - 1)
        ok = (kid >= b * tdec) & (kid <= b * tdec + t)
        update(jnp.where(ok, s_t, NEG), vnew_ref[...].astype(BF16))
        o_ref[...] = _attn_out(acc_ref[...], widen(l_ref[...]), wuv_ref, nh, tdec)


def _attn_sample(page_table, q4, kcat_new, v_new, fold, wuv, cache_ckv, cache_krt, bs, tdec):
    nh = MLA_HEADS
    n_pages = page_table.shape[1]
    page, kv_lora = cache_ckv.shape[1], cache_ckv.shape[2]
    rope = cache_krt.shape[1]
    npg = 64
    while n_pages % npg:
        npg //= 2
    nchunk = n_pages // npg
    rows = nh * tdec
    assert tdec & (tdec - 1) == 0 and rows == 128 and page == 128
    grid_spec = pltpu.PrefetchScalarGridSpec(
        num_scalar_prefetch=1,
        grid=(bs, nchunk),
        in_specs=[pl.BlockSpec((1, nh, tdec, QK_DIM), lambda b, c, pt: (b, 0, 0, 0)),
                  pl.BlockSpec(kcat_new.shape, lambda b, c, pt: (0, 0), pipeline_mode=pl.Buffered(1)),
                  pl.BlockSpec(v_new.shape, lambda b, c, pt: (0, 0), pipeline_mode=pl.Buffered(1)),
                  pl.BlockSpec(fold.shape, lambda b, c, pt: (0, 0), pipeline_mode=pl.Buffered(1)),
                  pl.BlockSpec(wuv.shape, lambda b, c, pt: (0, 0, 0), pipeline_mode=pl.Buffered(1)),
                  pl.BlockSpec(memory_space=pl.ANY), pl.BlockSpec(memory_space=pl.ANY)],
        out_specs=pl.BlockSpec((tdec, nh * V_HEAD), lambda b, c, pt: (b, 0)),
        scratch_shapes=[pltpu.VMEM((2, npg, page, kv_lora), F32), pltpu.VMEM((2, rope, npg * page), F32),
                        pltpu.SemaphoreType.DMA((2, 2)),
                        pltpu.VMEM((1, rows), F32), pltpu.VMEM((1, rows), F32), pltpu.VMEM((rows, kv_lora), F32),
                        pltpu.VMEM((rows, QK_DIM), BF16), pltpu.VMEM((kv_lora, rows), BF16),
                        pltpu.VMEM((rope, rows), BF16)],
    )
    return pl.pallas_call(
        functools.partial(_attn_sample_kernel, npg=npg, nchunk=nchunk, tdec=tdec),
        out_shape=jax.ShapeDtypeStruct((bs * tdec, nh * V_HEAD), F32),
        grid_spec=grid_spec,
        compiler_params=_cparams(("arbitrary", "arbitrary"), 40),
        name="attn_sample",
    )(page_table, q4, kcat_new, v_new, fold, wuv, cache_ckv, cache_krt)


def _merge_kernel(yn_ref, o_ref, ga_ref, gb_ref, h_ref, wa_ref, wb_ref, wo_ref, g_ref, out_ref):
    a = _dot(yn_ref[...].astype(BF16), wa_ref[...])
    bb = _dot(o_ref[...].astype(BF16), wb_ref[...])
    merged = jax.nn.sigmoid(ga_ref[0].astype(F32)) * a + jax.nn.sigmoid(gb_ref[0].astype(F32)) * bb
    mix = _dot(merged.astype(BF16), wo_ref[...])
    out_ref[...] = h_ref[...] + _rms(mix, g_ref[...])


def _merge(yn, o, proj3, h, wa, wb, wo, g):
    m, d = yn.shape[0], h.shape[1]
    tm = _row_tile(m, 512)
    return pl.pallas_call(
        _merge_kernel,
        out_shape=jax.ShapeDtypeStruct((m, d), F32),
        grid=(m // tm,),
        in_specs=[pl.BlockSpec((tm, yn.shape[1]), lambda i: (i, 0)), pl.BlockSpec((tm, o.shape[1]), lambda i: (i, 0)),
                  pl.BlockSpec((1, tm, SLAB), lambda i: (5, i, 0)), pl.BlockSpec((1, tm, SLAB), lambda i: (6, i, 0)),
                  pl.BlockSpec((tm, d), lambda i: (i, 0)),
                  _resident(wa.shape), _resident(wb.shape), _resident(wo.shape), _resident((1, d))],
        out_specs=pl.BlockSpec((tm, d), lambda i: (i, 0)),
        compiler_params=_cparams(("arbitrary",), 40),
        name="gated_merge",
    )(yn, o, proj3, proj3, h, wa, wb, wo, g)


def _prep_weights(w, d_inner, conv_dim, nheads, q_lora, kv_lora):
    d = w["w_in"].shape[0]
    offs = np.cumsum([0, d_inner, conv_dim, nheads, q_lora, kv_lora, QK_ROPE, d, d])
    seg = {k: w["w_in"][:, offs[i]:offs[i + 1]]
           for i, k in enumerate(("z", "xbc", "dt", "q_a", "kv_a", "k_pe", "ga", "gb"))}
    pad = SLAB - q_lora - kv_lora
    wbig = jnp.concatenate([seg["z"], seg["xbc"], seg["ga"], seg["gb"], seg["q_a"], seg["kv_a"],
                            jnp.zeros((d, pad), F32)], axis=1).astype(BF16)
    swap = np.concatenate([np.arange(QK_ROPE // 2, QK_ROPE), np.arange(QK_ROPE // 2)])
    wsmall = jnp.concatenate([seg["dt"], jnp.zeros((d, 128 - nheads), F32), jnp.tile(seg["k_pe"], (1, 4)),
                              jnp.tile(seg["k_pe"][:, swap], (1, 4))], axis=1).astype(BF16)
    nh = MLA_HEADS
    wq3 = w["w_q_b"].reshape(q_lora, nh, QK_NOPE + QK_ROPE)
    wq = jnp.concatenate([wq3[:, :, :QK_NOPE].reshape(q_lora, nh * QK_NOPE),
                          wq3[:, :, QK_NOPE:].reshape(q_lora, nh * QK_ROPE),
                          wq3[:, :, QK_NOPE:][:, :, swap].reshape(q_lora, nh * QK_ROPE)], axis=1).astype(BF16)
    wk = jnp.transpose(w["w_uk"], (1, 2, 0)).reshape(nh // 2, 2, QK_NOPE, kv_lora)
    zk = jnp.zeros((nh // 2, QK_NOPE, kv_lora), F32)
    wuk = jnp.concatenate([jnp.concatenate([wk[:, 0], zk], axis=2),
                           jnp.concatenate([zk, wk[:, 1]], axis=2)], axis=1).astype(BF16)
    wv = jnp.transpose(w["w_uv"], (1, 0, 2)).reshape(nh // 2, 2, kv_lora, V_HEAD)
    zv = jnp.zeros((nh // 2, kv_lora, V_HEAD), F32)
    wuvt = jnp.transpose(w["w_uv"], (1, 2, 0)).astype(BF16)
    wuv = jnp.concatenate([jnp.concatenate([wv[:, 0], zv], axis=2),
                           jnp.concatenate([zv, wv[:, 1]], axis=2)], axis=1).astype(BF16)
    def expand3(width):
        e = np.kron(np.eye(128, nheads, dtype=np.float32), np.ones((1, width), np.float32))
        return jnp.asarray(np.tile(e, (3, 1)), BF16)

    lane_pad = lambda a: jnp.pad(a, (0, 128 - nheads))[None]
    ssd = dict(conv_w=w["conv_w"], conv_b=w["conv_b"][None], dt_bias=lane_pad(w["dt_bias"]),
               a_log=lane_pad(w["a_log"]),
               dskip_x=jnp.repeat(w["d_skip"], SSM_HEAD_DIM)[None], norm_g=w["ssm_norm_g"][None],
               tril=jnp.asarray(np.tril(np.ones((CHUNK, CHUNK), np.float32))),
               expand=expand3(SSM_HEAD_DIM), expand128=expand3(128),
               shift=jnp.asarray(np.concatenate([np.eye(CHUNK, k=k - (CONV_K - 1), dtype=np.float32)
                                                 for k in range(CONV_K - 1)], axis=0), BF16))
    inv = ROPE_THETA ** (-jnp.arange(0, QK_ROPE, 2, dtype=F32) / QK_ROPE)
    inv128 = jnp.tile(jnp.concatenate([inv, inv]), 4)[None]
    sign128 = jnp.asarray(np.tile(np.concatenate([-np.ones(QK_ROPE // 2), np.ones(QK_ROPE // 2)]), 4)[None], F32)
    foldt = jnp.asarray(np.tile(np.eye(QK_ROPE, dtype=np.float32), (1, 4)))
    bf = lambda a: a.astype(BF16)
    row = lambda a: a[None]
    return dict(
        wbig=wbig, wsmall=wsmall, wq=wq, wuk=wuk, wuv=wuv, wuvt=wuvt, ssd=ssd, inv128=inv128, sign128=sign128, foldt=foldt,
        ffn1=(row(w["ffn1_pre_g"]), bf(w["ffn1_w_gate"]), bf(w["ffn1_w_up"]), bf(w["ffn1_w_down"]),
              row(w["ffn1_post_g"])),
        ffn2=(row(w["ffn2_pre_g"]), bf(w["ffn2_w_gate"]), bf(w["ffn2_w_up"]), bf(w["ffn2_w_down"]),
              row(w["ffn2_post_g"])),
        mix_pre_g=row(w["mix_pre_g"]), mix_post_g=row(w["mix_post_g"]), q_g=row(w["q_a_norm_g"]),
        kv_g=row(w["kv_a_norm_g"]), wa=bf(w["w_a_out"]), wb=bf(w["w_b_out"]), wo=bf(w["w_o"]))


def kernel(x_prompt, x_sample, cache_kv_latent, cache_k_rope, state_ssm, state_conv, page_table, meta_tokens,
           ffn1_pre_g, ffn1_w_gate, ffn1_w_up, ffn1_w_down, ffn1_post_g,
           mix_pre_g, w_in, conv_w, conv_b, dt_bias, a_log, d_skip, ssm_norm_g,
           q_a_norm_g, w_q_b, kv_a_norm_g, w_uk, w_uv, w_a_out, w_b_out, w_o, mix_post_g,
           ffn2_pre_g, ffn2_w_gate, ffn2_w_up, ffn2_w_down, ffn2_post_g):
    names = ("ffn1_pre_g", "ffn1_w_gate", "ffn1_w_up", "ffn1_w_down", "ffn1_post_g", "mix_pre_g", "w_in", "conv_w",
             "conv_b", "dt_bias", "a_log", "d_skip", "ssm_norm_g", "q_a_norm_g", "w_q_b", "kv_a_norm_g", "w_uk",
             "w_uv", "w_a_out", "w_b_out", "w_o", "mix_post_g", "ffn2_pre_g", "ffn2_w_gate", "ffn2_w_up",
             "ffn2_w_down", "ffn2_post_g")
    stacked = dict(zip(names, (ffn1_pre_g, ffn1_w_gate, ffn1_w_up, ffn1_w_down, ffn1_post_g, mix_pre_g, w_in,
                               conv_w, conv_b, dt_bias, a_log, d_skip, ssm_norm_g, q_a_norm_g, w_q_b, kv_a_norm_g,
                               w_uk, w_uv, w_a_out, w_b_out, w_o, mix_post_g, ffn2_pre_g, ffn2_w_gate, ffn2_w_up,
                               ffn2_w_down, ffn2_post_g)))
    depth = w_in.shape[0]
    bp, seq, d = x_prompt.shape
    bs, tdec, _ = x_sample.shape
    nheads = dt_bias.shape[1]
    d_inner = nheads * SSM_HEAD_DIM
    conv_dim = conv_b.shape[1]
    q_lora, kv_lora = q_a_norm_g.shape[1], kv_a_norm_g.shape[1]
    assert seq % CHUNK == 0 and conv_dim == 3 * SLAB and d_inner == 2 * SLAB and d == SLAB
    assert q_lora + kv_lora <= SLAB and CHUNK % tdec == 0
    n_pages, page = page_table.shape[1], cache_kv_latent.shape[2]
    past_len = n_pages * page
    lp = LEAD + N_META + seq
    nc = lp // CHUNK
    ns_rows = bs * tdec
    assert depth == 1 and ns_rows % CHUNK == 0
    meta_tile = ns_rows // CHUNK
    meta0 = ns_rows + LEAD

    hx = x_prompt.reshape(bp * seq, d)
    hs = jnp.concatenate([x_sample.reshape(ns_rows, d), jnp.zeros((LEAD, d), F32), meta_tokens], axis=0)
    pos_x = jnp.tile(N_META + jnp.arange(seq, dtype=jnp.int32), bp).astype(F32)[:, None]
    pos_s = jnp.concatenate([jnp.tile(past_len + jnp.arange(tdec, dtype=jnp.int32), bs),
                             jnp.arange(CHUNK, dtype=jnp.int32) - LEAD]).astype(F32)[:, None]

    outs = [[] for _ in range(8)]
    for l in range(depth):
        w = _prep_weights({k: v[l] for k, v in stacked.items()}, d_inner, conv_dim, nheads, q_lora, kv_lora)

        g1 = _ffn(hs, *w["ffn1"])
        proj3s, smalls = _inproj(g1, w["mix_pre_g"], w["wbig"], w["wsmall"], F32)
        c_kv_s, kpe_s, kcat_s, q4s = _qkv_prep(proj3s, smalls, pos_s, w["q_g"], w["kv_g"], w["wq"], w["wuk"],
                                               w["inv128"], w["sign128"], tdec, F32)

        h1 = _ffn(hx, *w["ffn1"])
        proj3, small = _inproj(h1, w["mix_pre_g"], w["wbig"], w["wsmall"], BF16)
        yn, h_fin, conv_new = _ssd_prompt(proj3, small, proj3s, smalls, meta_tile, w["ssd"], bp, nc)
        c_kv, kpe, kcat, q4 = _qkv_prep(proj3, small, pos_x, w["q_g"], w["kv_g"], w["wq"], w["wuk"], w["inv128"],
                                        w["sign128"], CHUNK, BF16)

        def with_meta(tile, rows):
            f = rows.shape[1]
            return jnp.concatenate([jnp.broadcast_to(tile.astype(rows.dtype)[None], (bp,) + tile.shape),
                                    rows.reshape(bp, seq, f)], axis=1)

        k_all = with_meta(kcat_s[ns_rows:], kcat).reshape(bp * lp, QK_DIM)
        vt = jnp.transpose(k_all.reshape(bp, lp, QK_DIM)[:, :, :kv_lora], (0, 2, 1))
        o = _attn_prompt(q4, k_all, vt, w["wuvt"], bp, lp)
        h2 = _merge(yn, o, proj3, h1, w["wa"], w["wb"], w["wo"], w["mix_post_g"])
        hx = _ffn(h2, *w["ffn2"])
        outs[0].append(with_meta(c_kv_s[meta0:], c_kv))
        outs[1].append(with_meta(kpe_s[meta0:, :QK_ROPE], kpe[:, :QK_ROPE]))
        outs[2].append(h_fin.reshape(bp, nheads, SSM_HEAD_DIM, D_STATE))
        outs[3].append(conv_new)

        yns, h_new, conv_new_s = _ssd_sample(proj3s, smalls, state_conv[l],
                                             state_ssm[l].reshape(bs, d_inner, D_STATE), w["ssd"], bs, tdec)
        os_ = _attn_sample(page_table, q4s, kcat_s[:ns_rows], c_kv_s[:ns_rows], w["foldt"], w["wuv"],
                           cache_kv_latent[l], jnp.transpose(cache_k_rope[l], (0, 2, 1)), bs, tdec)
        g2 = _merge(yns, os_, proj3s, g1, w["wa"], w["wb"], w["wo"], w["mix_post_g"])
        hs = _ffn(g2, *w["ffn2"])
        outs[4].append(c_kv_s[:ns_rows].reshape(bs, tdec, kv_lora))
        outs[5].append(kpe_s[:ns_rows, :QK_ROPE].reshape(bs, tdec, QK_ROPE))
        outs[6].append(h_new.reshape(bs, nheads, SSM_HEAD_DIM, D_STATE))
        outs[7].append(conv_new_s)

    y_prompt = hx.reshape(bp, seq, d)
    y_sample = hs.reshape(bs, tdec, d)
    return (y_prompt, y_sample) + tuple(jnp.stack(o) for o in outs)
```

```python
import functools
import math

import numpy as np
import jax
import jax.numpy as jnp
from jax import lax
from jax.experimental import pallas as pl
from jax.experimental.pallas import tpu as pltpu

F32 = jnp.float32
BF16 = jnp.bfloat16
HIGHEST = lax.Precision.HIGHEST

EPS = 1e-6
N_META = 16
CHUNK = 128
LEAD = (-N_META) % CHUNK
SSM_HEAD_DIM = 64
SSM_GROUPS = 4
D_STATE = 128
CONV_K = 4
MLA_HEADS = 16
QK_NOPE = 64
QK_ROPE = 32
V_HEAD = 64
ROPE_THETA = 10000.0
ATTN_SCALE = (QK_NOPE + QK_ROPE) ** -0.5
Q_SCALE = ATTN_SCALE * math.log2(math.e)
NEG = -1e30
SLAB = 1024
QK_DIM = 384
MIB = 1024 * 1024


def _cparams(sem, vmem_mib):
    return pltpu.CompilerParams(dimension_semantics=sem, vmem_limit_bytes=int(vmem_mib * MIB))


def _resident(shape):
    nd = len(shape)
    return pl.BlockSpec(shape, lambda *_: (0,) * nd, pipeline_mode=pl.Buffered(1))


def _rms(x, g):
    return x * lax.rsqrt(jnp.mean(x * x, axis=-1, keepdims=True) + EPS) * g


def _dot(a, b):
    return jnp.dot(a, b, preferred_element_type=F32)


def _dot_exact(a, b):
    return jnp.dot(a, b, preferred_element_type=F32, precision=HIGHEST)


def _dot_nt(a, b):
    return lax.dot_general(a, b, (((1,), (1,)), ((), ())), preferred_element_type=F32)


def _expand_exact(x, e3):
    hi = x.astype(BF16)
    r1 = x - hi.astype(F32)
    mid = r1.astype(BF16)
    lo = (r1 - mid.astype(F32)).astype(BF16)
    return _dot(jnp.concatenate([hi, mid, lo], axis=1), e3)


def _softplus(x):
    return jnp.maximum(x, 0.0) + jnp.log1p(jnp.exp(-jnp.abs(x)))


def _row_tile(m, cap, mult=16):
    for t in range(min(cap, m), mult - 1, -1):
        if m % t == 0 and t % mult == 0:
            return t
    raise ValueError(f"no row tile for {m}")


def _ffn_kernel(x_ref, pre_ref, wg_ref, wu_ref, wd_ref, post_ref, o_ref, *, chunks):
    x = x_ref[...]
    xn = _rms(x, pre_ref[...]).astype(BF16)
    acc = jnp.zeros(x.shape, F32)
    for lo, sz in chunks:
        g = _dot(xn, wg_ref[:, lo:lo + sz])
        u = _dot(xn, wu_ref[:, lo:lo + sz])
        h = (jax.nn.silu(g) * u).astype(BF16)
        acc = acc + _dot(h, wd_ref[lo:lo + sz, :])
    o_ref[...] = x + 0.5 * _rms(acc, post_ref[...])


def _ffn_chunks(f):
    out, lo = [], 0
    while lo < f:
        sz = min(1024, f - lo)
        out.append((lo, sz))
        lo += sz
    return tuple(out)


def _ffn_stream_kernel(x_ref, pre_ref, wg_ref, wu_ref, wd_ref, post_ref, o_ref, xn_ref, acc_ref):
    j = pl.program_id(0)

    @pl.when(j == 0)
    def _():
        xn_ref[...] = _rms(x_ref[...], pre_ref[...]).astype(BF16)
        acc_ref[...] = jnp.zeros(acc_ref.shape, F32)

    xn = xn_ref[...]
    h = (jax.nn.silu(_dot(xn, wg_ref[...])) * _dot(xn, wu_ref[...])).astype(BF16)
    acc_ref[...] += _dot(h, wd_ref[...])

    @pl.when(j == pl.num_programs(0) - 1)
    def _():
        o_ref[...] = x_ref[...] + 0.5 * _rms(acc_ref[...], post_ref[...])


def _ffn(x, pre_g, wg, wu, wd, post_g):
    m, d = x.shape
    f = wg.shape[1]
    tm = _row_tile(m, 512)
    ft = 256
    if tm == m and f % ft == 0:
        return pl.pallas_call(
            _ffn_stream_kernel,
            out_shape=jax.ShapeDtypeStruct((m, d), F32),
            grid=(f // ft,),
            in_specs=[pl.BlockSpec((m, d), lambda j: (0, 0)), pl.BlockSpec((1, d), lambda j: (0, 0)),
                      pl.BlockSpec((d, ft), lambda j: (0, j)), pl.BlockSpec((d, ft), lambda j: (0, j)),
                      pl.BlockSpec((ft, d), lambda j: (j, 0)), pl.BlockSpec((1, d), lambda j: (0, 0))],
            out_specs=pl.BlockSpec((m, d), lambda j: (0, 0)),
            scratch_shapes=[pltpu.VMEM((m, d), BF16), pltpu.VMEM((m, d), F32)],
            compiler_params=_cparams(("arbitrary",), 32),
            name="ffn_block_stream",
        )(x, pre_g, wg, wu, wd, post_g)
    return pl.pallas_call(
        functools.partial(_ffn_kernel, chunks=_ffn_chunks(f)),
        out_shape=jax.ShapeDtypeStruct((m, d), F32),
        grid=(m // tm,),
        in_specs=[pl.BlockSpec((tm, d), lambda i: (i, 0)), _resident((1, d)), _resident((d, f)),
                  _resident((d, f)), _resident((f, d)), _resident((1, d))],
        out_specs=pl.BlockSpec((tm, d), lambda i: (i, 0)),
        compiler_params=_cparams(("arbitrary",), 52),
        name="ffn_block",
    )(x, pre_g, wg, wu, wd, post_g)


def _inproj_kernel(x_ref, g_ref, wbig_ref, wsmall_ref, proj_ref, small_ref, *, nslab):
    xn = _rms(x_ref[...], g_ref[...]).astype(BF16)
    for s in range(nslab):
        proj_ref[s] = _dot(xn, wbig_ref[:, s * SLAB:(s + 1) * SLAB]).astype(proj_ref.dtype)
    small_ref[...] = _dot(xn, wsmall_ref[...])


def _inproj_stream_kernel(x_ref, g_ref, wbig_ref, wsmall_ref, proj_ref, small_ref, xn_ref):
    @pl.when(pl.program_id(0) == 0)
    def _():
        xn_ref[...] = _rms(x_ref[...], g_ref[...]).astype(BF16)
        small_ref[...] = _dot(xn_ref[...], wsmall_ref[...])

    proj_ref[0] = _dot(xn_ref[...], wbig_ref[...]).astype(proj_ref.dtype)


def _inproj(h, g, wbig, wsmall, proj_dtype):
    m, d = h.shape
    nslab = wbig.shape[1] // SLAB
    ns = wsmall.shape[1]
    tm = _row_tile(m, 512)
    if tm == m:
        return pl.pallas_call(
            _inproj_stream_kernel,
            out_shape=(jax.ShapeDtypeStruct((nslab, m, SLAB), proj_dtype), jax.ShapeDtypeStruct((m, ns), F32)),
            grid=(nslab,),
            in_specs=[pl.BlockSpec((m, d), lambda s: (0, 0)), pl.BlockSpec((1, d), lambda s: (0, 0)),
                      pl.BlockSpec((d, SLAB), lambda s: (0, s)), pl.BlockSpec(wsmall.shape, lambda s: (0, 0))],
            out_specs=(pl.BlockSpec((1, m, SLAB), lambda s: (s, 0, 0)), pl.BlockSpec((m, ns), lambda s: (0, 0))),
            scratch_shapes=[pltpu.VMEM((m, d), BF16)],
            compiler_params=_cparams(("arbitrary",), 32),
            name="in_proj_stream",
        )(h, g, wbig, wsmall)
    return pl.pallas_call(
        functools.partial(_inproj_kernel, nslab=nslab),
        out_shape=(jax.ShapeDtypeStruct((nslab, m, SLAB), proj_dtype), jax.ShapeDtypeStruct((m, ns), F32)),
        grid=(m // tm,),
        in_specs=[pl.BlockSpec((tm, d), lambda i: (i, 0)), _resident((1, d)), _resident(wbig.shape),
                  _resident(wsmall.shape)],
        out_specs=(pl.BlockSpec((nslab, tm, SLAB), lambda i: (0, i, 0)), pl.BlockSpec((tm, ns), lambda i: (i, 0))),
        compiler_params=_cparams(("arbitrary",), 52),
        name="in_proj",
    )(h, g, wbig, wsmall)


def _qkv_kernel(s7_ref, small_ref, pos_ref, qg_ref, kvg_ref, wq_ref, wuk_ref, inv_ref, sign_ref,
                ckv_ref, kpe_ref, kcat_ref, q_ref, *, tqb):
    s7 = s7_ref[0].astype(F32)
    tm = s7.shape[0]
    q_lora = qg_ref.shape[1]
    kv_lora = kvg_ref.shape[1]
    nh = MLA_HEADS
    c_kv = _rms(s7[:, q_lora:q_lora + kv_lora], kvg_ref[...])
    ckv_ref[...] = c_kv
    ang = pos_ref[...] * inv_ref[...]
    cos = jnp.cos(ang)
    sin = jnp.sin(ang) * sign_ref[...]
    small = small_ref[...]
    kpe = small[:, 128:256] * cos + small[:, 256:384] * sin
    kpe_ref[...] = kpe
    kcat_ref[:, 0:kv_lora] = c_kv.astype(kcat_ref.dtype)
    kcat_ref[:, kv_lora:kv_lora + 128] = kpe.astype(kcat_ref.dtype)

    qn = _rms(s7[:, 0:q_lora], qg_ref[...] * Q_SCALE).astype(BF16)
    nope_w = nh * QK_NOPE
    pe_w = nh * QK_ROPE
    q_nope = _dot(qn, wq_ref[:, 0:nope_w])
    q_pe = _dot(qn, wq_ref[:, nope_w:nope_w + pe_w])
    q_rot = _dot(qn, wq_ref[:, nope_w + pe_w:nope_w + 2 * pe_w])
    cos_w = jnp.concatenate([cos] * (pe_w // 128), axis=1)
    sin_w = jnp.concatenate([sin] * (pe_w // 128), axis=1)
    q_pe = q_pe * cos_w + q_rot * sin_w
    lane = lax.broadcasted_iota(jnp.int32, (1, 128), 1)
    nblk = tm // tqb
    for p in range(nh // 2):
        qn_pair = q_nope[:, p * 128:(p + 1) * 128].astype(BF16)
        q_lat = _dot(qn_pair, wuk_ref[p])
        for e in range(2):
            hd = 2 * p + e
            pe_blk = q_pe[:, (hd // 4) * 128:(hd // 4 + 1) * 128]
            sel = (lane >= (hd % 4) * QK_ROPE) & (lane < (hd % 4 + 1) * QK_ROPE)
            pe_blk = jnp.where(sel, pe_blk, 0.0)
            lat = q_lat[:, e * kv_lora:(e + 1) * kv_lora]
            for g in range(nblk):
                q_ref[g, hd, :, 0:kv_lora] = lat[g * tqb:(g + 1) * tqb].astype(q_ref.dtype)
                q_ref[g, hd, :, kv_lora:kv_lora + 128] = pe_blk[g * tqb:(g + 1) * tqb].astype(q_ref.dtype)


def _qkv_prep(proj3, small, pos, qg, kvg, wq, wuk, inv128, sign128, tqb, dtype):
    nslab, m, _ = proj3.shape
    kv_lora = kvg.shape[1]
    tm = _row_tile(m, 512, max(16, tqb))
    nblk = tm // tqb
    return pl.pallas_call(
        functools.partial(_qkv_kernel, tqb=tqb),
        out_shape=(jax.ShapeDtypeStruct((m, kv_lora), F32), jax.ShapeDtypeStruct((m, 128), F32),
                   jax.ShapeDtypeStruct((m, QK_DIM), dtype),
                   jax.ShapeDtypeStruct((m // tqb, MLA_HEADS, tqb, QK_DIM), dtype)),
        grid=(m // tm,),
        in_specs=[pl.BlockSpec((1, tm, SLAB), lambda i: (nslab - 1, i, 0)),
                  pl.BlockSpec((tm, small.shape[1]), lambda i: (i, 0)),
                  pl.BlockSpec((tm, 1), lambda i: (i, 0)),
                  _resident(qg.shape), _resident(kvg.shape), _resident(wq.shape), _resident(wuk.shape),
                  _resident((1, 128)), _resident((1, 128))],
        out_specs=(pl.BlockSpec((tm, kv_lora), lambda i: (i, 0)), pl.BlockSpec((tm, 128), lambda i: (i, 0)),
                   pl.BlockSpec((tm, QK_DIM), lambda i: (i, 0)),
                   pl.BlockSpec((nblk, MLA_HEADS, tqb, QK_DIM), lambda i: (i, 0, 0, 0))),
        compiler_params=_cparams(("arbitrary",), 48),
        name="qkv_prep",
    )(proj3, small, pos, qg, kvg, wq, wuk, inv128, sign128)


def _ssd_chunk(xh_ref, dt_raw, state_ref, apad_ref, y_ref, lo, hi, p, bf16_input, rows_out=CHUNK):
    d_inner = state_ref.shape[1]
    nheads = d_inner // SSM_HEAD_DIM
    gw = d_inner // SSM_GROUPS
    rows = lax.broadcasted_iota(jnp.int32, (CHUNK, 1), 0)
    valid = (rows >= lo) & (rows < hi)
    ntap = CONV_K - 1

    def conv_rows(nrows):
        out = p["conv_b"][...]
        for k in range(CONV_K):
            out = out + p["conv_w"][k:k + 1, :] * xh_ref[pl.ds(8 - ntap + k, nrows), :]
        return out

    if bf16_input:
        cur = xh_ref[8:8 + CHUNK, :]
        shifted = _dot(p["shift"][...], cur.astype(BF16))
        conv = p["conv_b"][...]
        for k in range(ntap):
            conv = conv + p["conv_w"][k:k + 1, :] * shifted[k * CHUNK:(k + 1) * CHUNK]
        conv = conv + p["conv_w"][ntap:CONV_K, :] * cur
        conv = jnp.concatenate([conv_rows(8), conv[8:]], axis=0)
    else:
        conv = conv_rows(CHUNK)
    xc = jnp.where(valid, jax.nn.silu(conv), 0.0)
    xs = xc[:, 0:d_inner]
    bm = xc[:, d_inner:d_inner + SSM_GROUPS * D_STATE]
    cm = xc[:, d_inner + SSM_GROUPS * D_STATE:]

    dt = jnp.where(valid, _softplus(dt_raw + p["dt_bias"][...]), 0.0)
    da = dt * (-jnp.exp(p["a_log"][...]))
    a_cs = _dot_exact(p["tril"][...], da) * math.log2(math.e)
    dt_x = _expand_exact(dt, p["expand"][...])
    acs_x = _expand_exact(a_cs, p["expand"][...])
    alast_x = acs_x[CHUNK - 1:CHUNK, :]
    apad_ref[...] = a_cs.T
    xdt = xs * dt_x
    xds = (xdt * jnp.exp2(alast_x - acs_x)).astype(BF16)
    exp_acs = jnp.exp2(acs_x)
    causal = rows >= lax.broadcasted_iota(jnp.int32, (1, CHUNK), 1)
    lane = lax.broadcasted_iota(jnp.int32, (1, 128), 1)
    hpg = nheads // SSM_GROUPS

    ro = rows_out
    for g in range(SSM_GROUPS):
        gs = slice(g * gw, (g + 1) * gw)
        bg = bm[:, g * D_STATE:(g + 1) * D_STATE]
        cg = cm[0:ro, g * D_STATE:(g + 1) * D_STATE].astype(BF16)
        cb = _dot_nt(cg, bg.astype(BF16))
        st = state_ref[:, gs]
        y_off = _dot(cg, st.astype(BF16)) * exp_acs[0:ro, gs]
        state_ref[:, gs] = st * jnp.exp2(alast_x[:, gs]) + _dot(bg.T.astype(BF16), xds[:, gs])
        y_ref[0:ro, gs] = y_off + xs[0:ro, gs] * p["dskip_x"][:, gs]
        for pr in range(hpg // 2):
            w2 = []
            for e in range(2):
                hd = g * hpg + 2 * pr + e
                seg = jnp.broadcast_to(a_cs[0:ro, hd:hd + 1], (ro, 128)) - apad_ref[pl.ds(hd, 1), :]
                decay = jnp.exp2(jnp.where(causal[0:ro], seg, NEG))
                w2.append((cb * decay).astype(BF16))
            ls = slice(g * gw + pr * 128, g * gw + (pr + 1) * 128)
            xp = xdt[:, ls]
            x2 = jnp.concatenate([jnp.where(lane < SSM_HEAD_DIM, xp, 0.0),
                                  jnp.where(lane >= SSM_HEAD_DIM, xp, 0.0)], axis=0).astype(BF16)
            y_ref[0:ro, ls] = y_ref[0:ro, ls] + _dot(jnp.concatenate(w2, axis=1), x2)


def _gate_norm(y, z, g, d_inner):
    yg = y * jax.nn.silu(z)
    gw = d_inner // SSM_GROUPS
    outs = []
    for k in range(SSM_GROUPS):
        v = yg[:, k * gw:(k + 1) * gw]
        outs.append(_rms(v, g[:, k * gw:(k + 1) * gw]))
    return jnp.concatenate(outs, axis=1)


_SSD_PARAM_NAMES = ("conv_w", "conv_b", "dt_bias", "a_log", "dskip_x", "norm_g", "tril", "expand", "shift")


def _ssd_prompt_kernel(x0_ref, x1_ref, bc_ref, z0_ref, z1_ref, small_ref, m0_ref, m1_ref, mbc_ref, msmall_ref,
                       *rest, nc):
    np_ = len(_SSD_PARAM_NAMES)
    p = dict(zip(_SSD_PARAM_NAMES, rest[:np_]))
    yn_ref, hfin_ref, convnew_ref, xh_ref, state_ref, apad_ref, y_ref, state0_ref, tail0_ref = rest[np_:]
    b = pl.program_id(0)
    c = pl.program_id(1)
    d_inner = state_ref.shape[1]
    assert x0_ref.dtype == BF16
    shared_first = (c == 0) & (b == 0)

    @pl.when(shared_first)
    def _():
        xh_ref[0:8, :] = jnp.zeros((8, xh_ref.shape[1]), F32)
        state_ref[...] = jnp.zeros(state_ref.shape, F32)
        valid = lax.broadcasted_iota(jnp.int32, (CHUNK, 1), 0) >= LEAD
        for j, ref in enumerate((m0_ref, m1_ref, mbc_ref)):
            xh_ref[8:8 + CHUNK, j * SLAB:(j + 1) * SLAB] = jnp.where(valid, ref[0].astype(BF16).astype(F32), 0.0)

    @pl.when((c == 0) & (b > 0))
    def _():
        state_ref[...] = state0_ref[...]
        xh_ref[0:8, :] = tail0_ref[...]

    @pl.when(c > 0)
    def _():
        for j, ref in enumerate((x0_ref, x1_ref, bc_ref)):
            xh_ref[8:8 + CHUNK, j * SLAB:(j + 1) * SLAB] = ref[0].astype(F32)

    @pl.when((c > 0) | (b == 0))
    def _():
        lo = jnp.where(c == 0, LEAD, 0)
        dt_raw = jnp.where(c == 0, msmall_ref[:, 0:128], small_ref[:, 0:128])
        _ssd_chunk(xh_ref, dt_raw, state_ref, apad_ref, y_ref, lo, CHUNK, p, True)
        z = jnp.concatenate([z0_ref[0], z1_ref[0]], axis=1).astype(F32)
        yn_ref[...] = _gate_norm(y_ref[...], z, p["norm_g"][...], d_inner).astype(yn_ref.dtype)
        xh_ref[0:8, :] = xh_ref[CHUNK:CHUNK + 8, :]

    @pl.when(shared_first)
    def _():
        state0_ref[...] = state_ref[...]
        tail0_ref[...] = xh_ref[0:8, :]

    @pl.when(c == nc - 1)
    def _():
        convnew_ref[0] = xh_ref[pl.ds(8 + CHUNK - (CONV_K - 1), CONV_K - 1), :]
        for j in range(d_inner // 128):
            hfin_ref[0, j * 128:(j + 1) * 128, :] = state_ref[:, j * 128:(j + 1) * 128].T


def _ssd_prompt(proj3, small, proj3m, smallm, meta_tile, params, bp, nc):
    d_inner = params["expand"].shape[1]
    conv_dim = params["conv_b"].shape[1]
    m = proj3.shape[1]
    ns = small.shape[1]
    ncx = nc - 1
    plist = [params[k] for k in _SSD_PARAM_NAMES]

    def xblk(b, c):
        return b * ncx + jnp.maximum(c - 1, 0)

    def slab(s):
        return pl.BlockSpec((1, CHUNK, SLAB), lambda b, c, s=s: (s, xblk(b, c), 0))

    def mslab(s):
        return pl.BlockSpec((1, CHUNK, SLAB), lambda b, c, s=s: (s, meta_tile, 0))

    return pl.pallas_call(
        functools.partial(_ssd_prompt_kernel, nc=nc),
        out_shape=(jax.ShapeDtypeStruct((m, d_inner), BF16),
                   jax.ShapeDtypeStruct((bp, d_inner, D_STATE), F32),
                   jax.ShapeDtypeStruct((bp, CONV_K - 1, conv_dim), F32)),
        grid=(bp, nc),
        in_specs=[slab(2), slab(3), slab(4), slab(0), slab(1),
                  pl.BlockSpec((CHUNK, ns), lambda b, c: (xblk(b, c), 0)),
                  mslab(2), mslab(3), mslab(4), pl.BlockSpec((CHUNK, ns), lambda b, c: (meta_tile, 0))]
                 + [_resident(a.shape) for a in plist],
        out_specs=(pl.BlockSpec((CHUNK, d_inner), lambda b, c: (xblk(b, c), 0)),
                   pl.BlockSpec((1, d_inner, D_STATE), lambda b, c: (b, 0, 0)),
                   pl.BlockSpec((1, CONV_K - 1, conv_dim), lambda b, c: (b, 0, 0))),
        scratch_shapes=[pltpu.VMEM((8 + CHUNK, conv_dim), F32), pltpu.VMEM((D_STATE, d_inner), F32),
                        pltpu.VMEM((CHUNK, 128), F32), pltpu.VMEM((CHUNK, d_inner), F32),
                        pltpu.VMEM((D_STATE, d_inner), F32), pltpu.VMEM((8, conv_dim), F32)],
        compiler_params=_cparams(("arbitrary", "arbitrary"), 40),
        name="ssd_prompt",
    )(proj3, proj3, proj3, proj3, proj3, small, proj3m, proj3m, proj3m, smallm, *plist)


def _ssd_sample_kernel(x0_ref, x1_ref, bc_ref, z0_ref, z1_ref, small_ref, conv0_ref, h0_ref, *rest, tdec):
    np_ = len(_SSD_PARAM_NAMES)
    p = dict(zip(_SSD_PARAM_NAMES, rest[:np_]))
    yn_ref, hnew_ref, convnew_ref, xh_ref, state_ref, apad_ref, y_ref, dt_ref = rest[np_:]
    d_inner = state_ref.shape[1]
    nk = CONV_K - 1
    xh_ref[...] = jnp.zeros(xh_ref.shape, F32)
    xh_ref[8 - nk:8, :] = conv0_ref[0]
    xh_ref[8:8 + tdec, 0:SLAB] = x0_ref[0]
    xh_ref[8:8 + tdec, SLAB:2 * SLAB] = x1_ref[0]
    xh_ref[8:8 + tdec, 2 * SLAB:3 * SLAB] = bc_ref[0]
    dt_ref[...] = jnp.zeros(dt_ref.shape, F32)
    dt_ref[0:tdec, :] = small_ref[:, 0:128]
    for j in range(d_inner // 128):
        state_ref[:, j * 128:(j + 1) * 128] = h0_ref[0, j * 128:(j + 1) * 128, :].T
    _ssd_chunk(xh_ref, dt_ref[...], state_ref, apad_ref, y_ref, 0, tdec, p, False, rows_out=tdec)
    z = jnp.concatenate([z0_ref[0], z1_ref[0]], axis=1)
    yn_ref[...] = _gate_norm(y_ref[0:tdec, :], z, p["norm_g"][...], d_inner)
    convnew_ref[0] = xh_ref[pl.ds(8 + tdec - nk, nk), :]
    for j in range(d_inner // 128):
        hnew_ref[0, j * 128:(j + 1) * 128, :] = state_ref[:, j * 128:(j + 1) * 128].T


def _ssd_sample(proj3, small, conv0, h0, params, bs, tdec):
    d_inner = params["expand"].shape[1]
    conv_dim = params["conv_b"].shape[1]
    ns = small.shape[1]
    plist = [params[k] for k in _SSD_PARAM_NAMES]

    def slab(s):
        return pl.BlockSpec((1, tdec, SLAB), lambda b, s=s: (s, b, 0))

    return pl.pallas_call(
        functools.partial(_ssd_sample_kernel, tdec=tdec),
        out_shape=(jax.ShapeDtypeStruct((bs * tdec, d_inner), F32),
                   jax.ShapeDtypeStruct((bs, d_inner, D_STATE), F32),
                   jax.ShapeDtypeStruct((bs, CONV_K - 1, conv_dim), F32)),
        grid=(bs,),
        in_specs=[slab(2), slab(3), slab(4), slab(0), slab(1), pl.BlockSpec((tdec, ns), lambda b: (b, 0)),
                  pl.BlockSpec((1, CONV_K - 1, conv_dim), lambda b: (b, 0, 0)),
                  pl.BlockSpec((1, d_inner, D_STATE), lambda b: (b, 0, 0))] + [_resident(a.shape) for a in plist],
        out_specs=(pl.BlockSpec((tdec, d_inner), lambda b: (b, 0)),
                   pl.BlockSpec((1, d_inner, D_STATE), lambda b: (b, 0, 0)),
                   pl.BlockSpec((1, CONV_K - 1, conv_dim), lambda b: (b, 0, 0))),
        scratch_shapes=[pltpu.VMEM((8 + CHUNK, conv_dim), F32), pltpu.VMEM((D_STATE, d_inner), F32),
                        pltpu.VMEM((CHUNK, 128), F32), pltpu.VMEM((CHUNK, d_inner), F32),
                        pltpu.VMEM((CHUNK, 128), F32)],
        compiler_params=_cparams(("arbitrary",), 40),
        name="ssd_sample",
    )(proj3, proj3, proj3, proj3, proj3, small, conv0, h0, *plist)


def _attn_out(acc, l, wuv_ref, nh, tq):
    o_lat = acc / l
    outs = []
    for p in range(nh // 2):
        pair = jnp.concatenate([o_lat[(2 * p) * tq:(2 * p + 1) * tq], o_lat[(2 * p + 1) * tq:(2 * p + 2) * tq]],
                               axis=1).astype(BF16)
        outs.append(_dot(pair, wuv_ref[p]))
    return jnp.concatenate(outs, axis=1)


def _attn_prompt_kernel(q_ref, k_ref, vt_ref, wuvt_ref, o_ref, m_ref, l_ref, acc_ref, p_ref, *, tk, lp, q_off):
    qi = pl.program_id(1) + q_off
    nh, tq = q_ref.shape[1], q_ref.shape[2]
    hpg = q_ref.shape[1] // m_ref.shape[0]
    ngrp = nh // hpg
    gq = hpg * tq
    m_ref[...] = jnp.full(m_ref.shape, NEG, F32)
    l_ref[...] = jnp.zeros(l_ref.shape, F32)
    qidx = qi * tq + (lax.broadcasted_iota(jnp.int32, (1, gq), 1) & (tq - 1))
    nkc = ((qi + 1) * tq + tk - 1) // tk

    def chunk_start(kc):
        return pl.multiple_of(jnp.minimum(kc * tk, lp - tk), 128)

    def stage(kc, scores, values, diagonal):
        if scores:
            k = k_ref[pl.ds(chunk_start(kc), tk), :]
            kidx = chunk_start(kc) + lax.broadcasted_iota(jnp.int32, (tk, 1), 0)
        if values:
            vt = vt_ref[0, :, pl.ds(chunk_start(kc - 1), tk)]
        for r in range(ngrp):
            if values:
                acc = acc_ref[r] + _dot(vt, p_ref[r])
            else:
                acc = jnp.zeros(acc_ref.shape[1:], F32)
            if scores:
                q = q_ref[0, hpg * r:hpg * (r + 1)].reshape(gq, q_ref.shape[3])
                s = _dot_nt(k, q)
                if diagonal:
                    s = jnp.where((kidx <= qidx) & (kidx >= jnp.maximum(kc * tk, LEAD)), s, NEG)
                else:
                    s = jnp.concatenate([jnp.where(kidx[0:CHUNK] >= LEAD, s[0:CHUNK], NEG), s[CHUNK:]], axis=0)
                m_old = m_ref[r:r + 1, :]
                m_new = jnp.maximum(m_old, jnp.max(s, axis=0, keepdims=True))
                pexp = jnp.exp2(s - m_new)
                alpha = jnp.exp2(m_old - m_new)
                l_ref[r:r + 1, :] = alpha * l_ref[r:r + 1, :] + jnp.sum(pexp, axis=0, keepdims=True)
                p_ref[r] = pexp.astype(BF16)
                m_ref[r:r + 1, :] = m_new
                if values:
                    acc = alpha * acc
            acc_ref[r] = acc

    def body(kc, carry):
        stage(kc, True, True, False)
        return carry

    @pl.when(nkc == 1)
    def _():
        stage(0, True, False, True)

    @pl.when(nkc > 1)
    def _():
        stage(0, True, False, False)
        lax.fori_loop(1, nkc - 1, body, 0)
        stage(nkc - 1, True, True, True)

    stage(nkc, False, True, False)
    for r in range(ngrp):
        o_lat_t = (acc_ref[r] * (1.0 / l_ref[r:r + 1, :])).astype(BF16)
        for j in range(hpg // 2):
            h0 = hpg * r + 2 * j
            pair = jnp.concatenate([_dot(wuvt_ref[h0 + e], o_lat_t[:, (2 * j + e) * tq:(2 * j + e + 1) * tq])
                                    for e in range(2)], axis=0)
            o_ref[:, h0 * V_HEAD:(h0 + 2) * V_HEAD] = pair.T.astype(o_ref.dtype)


def _attn_prompt(q4, kcat, vt, wuvt, bp, lp):
    q_off = 1
    nq = lp // CHUNK - q_off
    nh = MLA_HEADS
    kv_lora = vt.shape[1]
    tk = 512
    hpg = 4
    assert lp % 128 == 0 and lp >= tk and 2 * V_HEAD == 128 and nh % hpg == 0
    return pl.pallas_call(
        functools.partial(_attn_prompt_kernel, tk=tk, lp=lp, q_off=q_off),
        out_shape=jax.ShapeDtypeStruct((bp * nq * CHUNK, nh * V_HEAD), BF16),
        grid=(bp, nq),
        in_specs=[pl.BlockSpec((1, nh, CHUNK, QK_DIM), lambda b, i: (b * nq + i, 0, 0, 0)),
                  pl.BlockSpec((lp, QK_DIM), lambda b, i: (b, 0)),
                  pl.BlockSpec((1, kv_lora, lp), lambda b, i: (b, 0, 0)),
                  _resident(wuvt.shape)],
        out_specs=pl.BlockSpec((CHUNK, nh * V_HEAD), lambda b, i: (b * nq + i, 0)),
        scratch_shapes=[pltpu.VMEM((nh // hpg, hpg * CHUNK), F32), pltpu.VMEM((nh // hpg, hpg * CHUNK), F32),
                        pltpu.VMEM((nh // hpg, kv_lora, hpg * CHUNK), F32),
                        pltpu.VMEM((nh // hpg, tk, hpg * CHUNK), BF16)],
        compiler_params=_cparams(("arbitrary", "arbitrary"), 40),
        name="attn_prompt",
    )(q4, kcat, vt, wuvt)


def _attn_sample_kernel(pt_ref, q_ref, knew_ref, vnew_ref, foldt_ref, wuv_ref, ckv_hbm, krt_hbm, o_ref,
                        cbuf, rbuf, sem, m_ref, l_ref, acc_ref, q_scr, qlt_scr, qpt_scr, *, npg, nchunk, tdec):
    b = pl.program_id(0)
    c = pl.program_id(1)
    nb = pl.num_programs(0)
    step = b * nchunk + c
    slot = step % 2
    nh = q_ref.shape[1]
    rows = nh * tdec
    kv_lora = cbuf.shape[-1]
    page = cbuf.shape[2]

    def copies(bb, cc, sl):
        out = []
        for j in range(npg):
            pg = pt_ref[bb, cc * npg + j]
            out.append(pltpu.make_async_copy(ckv_hbm.at[pg], cbuf.at[sl, j], sem.at[0, sl]))
            out.append(pltpu.make_async_copy(krt_hbm.at[pg], rbuf.at[sl, :, pl.ds(j * page, page)], sem.at[1, sl]))
        return out

    @pl.when(step == 0)
    def _():
        for cp in copies(b, c, slot):
            cp.start()

    @pl.when(step + 1 < nb * nchunk)
    def _():
        nxt = step + 1
        for cp in copies(nxt // nchunk, nxt % nchunk, 1 - slot):
            cp.start()

    @pl.when(c == 0)
    def _():
        m_ref[...] = jnp.full(m_ref.shape, NEG, F32)
        l_ref[...] = jnp.zeros(l_ref.shape, F32)
        acc_ref[...] = jnp.zeros(acc_ref.shape, F32)
        q = q_ref[0].reshape(rows, q_ref.shape[3])
        q_scr[...] = q.astype(BF16)
        for j in range(kv_lora // 128):
            qlt_scr[j * 128:(j + 1) * 128, :] = q[:, j * 128:(j + 1) * 128].T.astype(BF16)
        qpt_scr[...] = lax.dot_general(foldt_ref[...], q[:, kv_lora:kv_lora + 128], (((1,), (1,)), ((), ())),
                                       preferred_element_type=F32, precision=HIGHEST).astype(BF16)

    for cp in copies(b, c, slot):
        cp.wait()

    def lanes_to_rows(x):
        return jnp.broadcast_to(x, (rows, rows)).T

    def widen(x):
        return jnp.concatenate([lanes_to_rows(x)] * (kv_lora // 128), axis=1)

    def update(s_t, v):
        m_old = m_ref[...]
        m_new = jnp.maximum(m_old, jnp.max(s_t, axis=0, keepdims=True))
        p_t = jnp.exp2(s_t - m_new)
        alpha = jnp.exp2(m_old - m_new)
        l_ref[...] = alpha * l_ref[...] + jnp.sum(p_t, axis=0, keepdims=True)
        m_ref[...] = m_new
        nk = s_t.shape[0]
        if nk % 128:
            pmat = p_t.T
        else:
            pmat = jnp.concatenate([p_t[j * 128:(j + 1) * 128, :].T for j in range(nk // 128)], axis=1)
        acc_ref[...] = acc_ref[...] * widen(alpha) + _dot(pmat.astype(BF16), v)

    kc = cbuf[slot].reshape(npg * page, kv_lora).astype(BF16)
    kr = jnp.concatenate([rbuf[slot, :, j * page:(j + 1) * page].T for j in range(npg)],
                         axis=0).astype(BF16)
    update(_dot(kc, qlt_scr[...]) + _dot(kr, qpt_scr[...]), kc)

    @pl.when(c == nchunk - 1)
    def _():
        knew = knew_ref[...].astype(BF16)
        s_t = _dot_nt(knew, q_scr[...])
        nk = knew.shape[0]
        kid = lax.broadcasted_iota(jnp.int32, (nk, 1), 0)
        t = lax.broadcasted_iota(jnp.int32, (1, rows), 1) & (tdec - 1)
        ok = (kid >= b * tdec) & (kid <= b * tdec + t)
        update(jnp.where(ok, s_t, NEG), vnew_ref[...].astype(BF16))
        o_ref[...] = _attn_out(acc_ref[...], widen(l_ref[...]), wuv_ref, nh, tdec)


def _attn_sample(page_table, q4, kcat_new, v_new, fold, wuv, cache_ckv, cache_krt, bs, tdec):
    nh = MLA_HEADS
    n_pages = page_table.shape[1]
    page, kv_lora = cache_ckv.shape[1], cache_ckv.shape[2]
    rope = cache_krt.shape[1]
    npg = 64
    while n_pages % npg:
        npg //= 2
    nchunk = n_pages // npg
    rows = nh * tdec
    assert tdec & (tdec - 1) == 0 and rows == 128 and page == 128
    grid_spec = pltpu.PrefetchScalarGridSpec(
        num_scalar_prefetch=1,
        grid=(bs, nchunk),
        in_specs=[pl.BlockSpec((1, nh, tdec, QK_DIM), lambda b, c, pt: (b, 0, 0, 0)),
                  pl.BlockSpec(kcat_new.shape, lambda b, c, pt: (0, 0), pipeline_mode=pl.Buffered(1)),
                  pl.BlockSpec(v_new.shape, lambda b, c, pt: (0, 0), pipeline_mode=pl.Buffered(1)),
                  pl.BlockSpec(fold.shape, lambda b, c, pt: (0, 0), pipeline_mode=pl.Buffered(1)),
                  pl.BlockSpec(wuv.shape, lambda b, c, pt: (0, 0, 0), pipeline_mode=pl.Buffered(1)),
                  pl.BlockSpec(memory_space=pl.ANY), pl.BlockSpec(memory_space=pl.ANY)],
        out_specs=pl.BlockSpec((tdec, nh * V_HEAD), lambda b, c, pt: (b, 0)),
        scratch_shapes=[pltpu.VMEM((2, npg, page, kv_lora), F32), pltpu.VMEM((2, rope, npg * page), F32),
                        pltpu.SemaphoreType.DMA((2, 2)),
                        pltpu.VMEM((1, rows), F32), pltpu.VMEM((1, rows), F32), pltpu.VMEM((rows, kv_lora), F32),
                        pltpu.VMEM((rows, QK_DIM), BF16), pltpu.VMEM((kv_lora, rows), BF16),
                        pltpu.VMEM((rope, rows), BF16)],
    )
    return pl.pallas_call(
        functools.partial(_attn_sample_kernel, npg=npg, nchunk=nchunk, tdec=tdec),
        out_shape=jax.ShapeDtypeStruct((bs * tdec, nh * V_HEAD), F32),
        grid_spec=grid_spec,
        compiler_params=_cparams(("arbitrary", "arbitrary"), 40),
        name="attn_sample",
    )(page_table, q4, kcat_new, v_new, fold, wuv, cache_ckv, cache_krt)


def _merge_kernel(yn_ref, o_ref, ga_ref, gb_ref, h_ref, wa_ref, wb_ref, wo_ref, g_ref, out_ref):
    a = _dot(yn_ref[...].astype(BF16), wa_ref[...])
    bb = _dot(o_ref[...].astype(BF16), wb_ref[...])
    merged = jax.nn.sigmoid(ga_ref[0].astype(F32)) * a + jax.nn.sigmoid(gb_ref[0].astype(F32)) * bb
    mix = _dot(merged.astype(BF16), wo_ref[...])
    out_ref[...] = h_ref[...] + _rms(mix, g_ref[...])


def _merge(yn, o, proj3, h, wa, wb, wo, g):
    m, d = yn.shape[0], h.shape[1]
    tm = _row_tile(m, 512)
    return pl.pallas_call(
        _merge_kernel,
        out_shape=jax.ShapeDtypeStruct((m, d), F32),
        grid=(m // tm,),
        in_specs=[pl.BlockSpec((tm, yn.shape[1]), lambda i: (i, 0)), pl.BlockSpec((tm, o.shape[1]), lambda i: (i, 0)),
                  pl.BlockSpec((1, tm, SLAB), lambda i: (5, i, 0)), pl.BlockSpec((1, tm, SLAB), lambda i: (6, i, 0)),
                  pl.BlockSpec((tm, d), lambda i: (i, 0)),
                  _resident(wa.shape), _resident(wb.shape), _resident(wo.shape), _resident((1, d))],
        out_specs=pl.BlockSpec((tm, d), lambda i: (i, 0)),
        compiler_params=_cparams(("arbitrary",), 40),
        name="gated_merge",
    )(yn, o, proj3, proj3, h, wa, wb, wo, g)


def _prep_weights(w, d_inner, conv_dim, nheads, q_lora, kv_lora):
    d = w["w_in"].shape[0]
    offs = np.cumsum([0, d_inner, conv_dim, nheads, q_lora, kv_lora, QK_ROPE, d, d])
    seg = {k: w["w_in"][:, offs[i]:offs[i + 1]]
           for i, k in enumerate(("z", "xbc", "dt", "q_a", "kv_a", "k_pe", "ga", "gb"))}
    pad = SLAB - q_lora - kv_lora
    wbig = jnp.concatenate([seg["z"], seg["xbc"], seg["ga"], seg["gb"], seg["q_a"], seg["kv_a"],
                            jnp.zeros((d, pad), F32)], axis=1).astype(BF16)
    swap = np.concatenate([np.arange(QK_ROPE // 2, QK_ROPE), np.arange(QK_ROPE // 2)])
    wsmall = jnp.concatenate([seg["dt"], jnp.zeros((d, 128 - nheads), F32), jnp.tile(seg["k_pe"], (1, 4)),
                              jnp.tile(seg["k_pe"][:, swap], (1, 4))], axis=1).astype(BF16)
    nh = MLA_HEADS
    wq3 = w["w_q_b"].reshape(q_lora, nh, QK_NOPE + QK_ROPE)
    wq = jnp.concatenate([wq3[:, :, :QK_NOPE].reshape(q_lora, nh * QK_NOPE),
                          wq3[:, :, QK_NOPE:].reshape(q_lora, nh * QK_ROPE),
                          wq3[:, :, QK_NOPE:][:, :, swap].reshape(q_lora, nh * QK_ROPE)], axis=1).astype(BF16)
    wk = jnp.transpose(w["w_uk"], (1, 2, 0)).reshape(nh // 2, 2, QK_NOPE, kv_lora)
    zk = jnp.zeros((nh // 2, QK_NOPE, kv_lora), F32)
    wuk = jnp.concatenate([jnp.concatenate([wk[:, 0], zk], axis=2),
                           jnp.concatenate([zk, wk[:, 1]], axis=2)], axis=1).astype(BF16)
    wv = jnp.transpose(w["w_uv"], (1, 0, 2)).reshape(nh // 2, 2, kv_lora, V_HEAD)
    zv = jnp.zeros((nh // 2, kv_lora, V_HEAD), F32)
    wuvt = jnp.transpose(w["w_uv"], (1, 2, 0)).astype(BF16)
    wuv = jnp.concatenate([jnp.concatenate([wv[:, 0], zv], axis=2),
                           jnp.concatenate([zv, wv[:, 1]], axis=2)], axis=1).astype(BF16)
    def expand3(width):
        e = np.kron(np.eye(128, nheads, dtype=np.float32), np.ones((1, width), np.float32))
        return jnp.asarray(np.tile(e, (3, 1)), BF16)

    lane_pad = lambda a: jnp.pad(a, (0, 128 - nheads))[None]
    ssd = dict(conv_w=w["conv_w"], conv_b=w["conv_b"][None], dt_bias=lane_pad(w["dt_bias"]),
               a_log=lane_pad(w["a_log"]),
               dskip_x=jnp.repeat(w["d_skip"], SSM_HEAD_DIM)[None], norm_g=w["ssm_norm_g"][None],
               tril=jnp.asarray(np.tril(np.ones((CHUNK, CHUNK), np.float32))),
               expand=expand3(SSM_HEAD_DIM),
               shift=jnp.asarray(np.concatenate([np.eye(CHUNK, k=k - (CONV_K - 1), dtype=np.float32)
                                                 for k in range(CONV_K - 1)], axis=0), BF16))
    inv = ROPE_THETA ** (-jnp.arange(0, QK_ROPE, 2, dtype=F32) / QK_ROPE)
    inv128 = jnp.tile(jnp.concatenate([inv, inv]), 4)[None]
    sign128 = jnp.asarray(np.tile(np.concatenate([-np.ones(QK_ROPE // 2), np.ones(QK_ROPE // 2)]), 4)[None], F32)
    foldt = jnp.asarray(np.tile(np.eye(QK_ROPE, dtype=np.float32), (1, 4)))
    bf = lambda a: a.astype(BF16)
    row = lambda a: a[None]
    return dict(
        wbig=wbig, wsmall=wsmall, wq=wq, wuk=wuk, wuv=wuv, wuvt=wuvt, ssd=ssd, inv128=inv128, sign128=sign128, foldt=foldt,
        ffn1=(row(w["ffn1_pre_g"]), bf(w["ffn1_w_gate"]), bf(w["ffn1_w_up"]), bf(w["ffn1_w_down"]),
              row(w["ffn1_post_g"])),
        ffn2=(row(w["ffn2_pre_g"]), bf(w["ffn2_w_gate"]), bf(w["ffn2_w_up"]), bf(w["ffn2_w_down"]),
              row(w["ffn2_post_g"])),
        mix_pre_g=row(w["mix_pre_g"]), mix_post_g=row(w["mix_post_g"]), q_g=row(w["q_a_norm_g"]),
        kv_g=row(w["kv_a_norm_g"]), wa=bf(w["w_a_out"]), wb=bf(w["w_b_out"]), wo=bf(w["w_o"]))


def kernel(x_prompt, x_sample, cache_kv_latent, cache_k_rope, state_ssm, state_conv, page_table, meta_tokens,
           ffn1_pre_g, ffn1_w_gate, ffn1_w_up, ffn1_w_down, ffn1_post_g,
           mix_pre_g, w_in, conv_w, conv_b, dt_bias, a_log, d_skip, ssm_norm_g,
           q_a_norm_g, w_q_b, kv_a_norm_g, w_uk, w_uv, w_a_out, w_b_out, w_o, mix_post_g,
           ffn2_pre_g, ffn2_w_gate, ffn2_w_up, ffn2_w_down, ffn2_post_g):
    names = ("ffn1_pre_g", "ffn1_w_gate", "ffn1_w_up", "ffn1_w_down", "ffn1_post_g", "mix_pre_g", "w_in", "conv_w",
             "conv_b", "dt_bias", "a_log", "d_skip", "ssm_norm_g", "q_a_norm_g", "w_q_b", "kv_a_norm_g", "w_uk",
             "w_uv", "w_a_out", "w_b_out", "w_o", "mix_post_g", "ffn2_pre_g", "ffn2_w_gate", "ffn2_w_up",
             "ffn2_w_down", "ffn2_post_g")
    stacked = dict(zip(names, (ffn1_pre_g, ffn1_w_gate, ffn1_w_up, ffn1_w_down, ffn1_post_g, mix_pre_g, w_in,
                               conv_w, conv_b, dt_bias, a_log, d_skip, ssm_norm_g, q_a_norm_g, w_q_b, kv_a_norm_g,
                               w_uk, w_uv, w_a_out, w_b_out, w_o, mix_post_g, ffn2_pre_g, ffn2_w_gate, ffn2_w_up,
                               ffn2_w_down, ffn2_post_g)))
    depth = w_in.shape[0]
    bp, seq, d = x_prompt.shape
    bs, tdec, _ = x_sample.shape
    nheads = dt_bias.shape[1]
    d_inner = nheads * SSM_HEAD_DIM
    conv_dim = conv_b.shape[1]
    q_lora, kv_lora = q_a_norm_g.shape[1], kv_a_norm_g.shape[1]
    assert seq % CHUNK == 0 and conv_dim == 3 * SLAB and d_inner == 2 * SLAB and d == SLAB
    assert q_lora + kv_lora <= SLAB and CHUNK % tdec == 0
    n_pages, page = page_table.shape[1], cache_kv_latent.shape[2]
    past_len = n_pages * page
    lp = LEAD + N_META + seq
    nc = lp // CHUNK
    ns_rows = bs * tdec
    assert depth == 1 and ns_rows % CHUNK == 0
    meta_tile = ns_rows // CHUNK
    meta0 = ns_rows + LEAD

    hx = x_prompt.reshape(bp * seq, d)
    hs = jnp.concatenate([x_sample.reshape(ns_rows, d), jnp.zeros((LEAD, d), F32), meta_tokens], axis=0)
    pos_x = jnp.tile(N_META + jnp.arange(seq, dtype=jnp.int32), bp).astype(F32)[:, None]
    pos_s = jnp.concatenate([jnp.tile(past_len + jnp.arange(tdec, dtype=jnp.int32), bs),
                             jnp.arange(CHUNK, dtype=jnp.int32) - LEAD]).astype(F32)[:, None]

    outs = [[] for _ in range(8)]
    for l in range(depth):
        w = _prep_weights({k: v[l] for k, v in stacked.items()}, d_inner, conv_dim, nheads, q_lora, kv_lora)

        g1 = _ffn(hs, *w["ffn1"])
        proj3s, smalls = _inproj(g1, w["mix_pre_g"], w["wbig"], w["wsmall"], F32)
        c_kv_s, kpe_s, kcat_s, q4s = _qkv_prep(proj3s, smalls, pos_s, w["q_g"], w["kv_g"], w["wq"], w["wuk"],
                                               w["inv128"], w["sign128"], tdec, F32)

        h1 = _ffn(hx, *w["ffn1"])
        proj3, small = _inproj(h1, w["mix_pre_g"], w["wbig"], w["wsmall"], BF16)
        yn, h_fin, conv_new = _ssd_prompt(proj3, small, proj3s, smalls, meta_tile, w["ssd"], bp, nc)
        c_kv, kpe, kcat, q4 = _qkv_prep(proj3, small, pos_x, w["q_g"], w["kv_g"], w["wq"], w["wuk"], w["inv128"],
                                        w["sign128"], CHUNK, BF16)

        def with_meta(tile, rows):
            f = rows.shape[1]
            return jnp.concatenate([jnp.broadcast_to(tile.astype(rows.dtype)[None], (bp,) + tile.shape),
                                    rows.reshape(bp, seq, f)], axis=1)

        k_all = with_meta(kcat_s[ns_rows:], kcat).reshape(bp * lp, QK_DIM)
        vt = jnp.transpose(k_all.reshape(bp, lp, QK_DIM)[:, :, :kv_lora], (0, 2, 1))
        o = _attn_prompt(q4, k_all, vt, w["wuvt"], bp, lp)
        h2 = _merge(yn, o, proj3, h1, w["wa"], w["wb"], w["wo"], w["mix_post_g"])
        hx = _ffn(h2, *w["ffn2"])
        outs[0].append(with_meta(c_kv_s[meta0:], c_kv))
        outs[1].append(with_meta(kpe_s[meta0:, :QK_ROPE], kpe[:, :QK_ROPE]))
        outs[2].append(h_fin.reshape(bp, nheads, SSM_HEAD_DIM, D_STATE))
        outs[3].append(conv_new)

        yns, h_new, conv_new_s = _ssd_sample(proj3s, smalls, state_conv[l],
                                             state_ssm[l].reshape(bs, d_inner, D_STATE), w["ssd"], bs, tdec)
        os_ = _attn_sample(page_table, q4s, kcat_s[:ns_rows], c_kv_s[:ns_rows], w["foldt"], w["wuv"],
                           cache_kv_latent[l], jnp.transpose(cache_k_rope[l], (0, 2, 1)), bs, tdec)
        g2 = _merge(yns, os_, proj3s, g1, w["wa"], w["wb"], w["wo"], w["mix_post_g"])
        hs = _ffn(g2, *w["ffn2"])
        outs[4].append(c_kv_s[:ns_rows].reshape(bs, tdec, kv_lora))
        outs[5].append(kpe_s[:ns_rows, :QK_ROPE].reshape(bs, tdec, QK_ROPE))
        outs[6].append(h_new.reshape(bs, nheads, SSM_HEAD_DIM, D_STATE))
        outs[7].append(conv_new_s)

    y_prompt = hx.reshape(bp, seq, d)
    y_sample = hs.reshape(bs, tdec, d)
    return (y_prompt, y_sample) + tuple(jnp.stack(o) for o in outs)
```

```python
import functools
import math

import numpy as np
import jax
import jax.numpy as jnp
from jax import lax
from jax.experimental import pallas as pl
from jax.experimental.pallas import tpu as pltpu

F32 = jnp.float32
BF16 = jnp.bfloat16
HIGHEST = lax.Precision.HIGHEST

EPS = 1e-6
N_META = 16
CHUNK = 128
LEAD = (-N_META) % CHUNK
SSM_HEAD_DIM = 64
SSM_GROUPS = 4
D_STATE = 128
CONV_K = 4
MLA_HEADS = 16
QK_NOPE = 64
QK_ROPE = 32
V_HEAD = 64
ROPE_THETA = 10000.0
ATTN_SCALE = (QK_NOPE + QK_ROPE) ** -0.5
Q_SCALE = ATTN_SCALE * math.log2(math.e)
NEG = -1e30
SLAB = 1024
QK_DIM = 384
MIB = 1024 * 1024


def _cparams(sem, vmem_mib):
    return pltpu.CompilerParams(dimension_semantics=sem, vmem_limit_bytes=int(vmem_mib * MIB))


def _resident(shape):
    nd = len(shape)
    return pl.BlockSpec(shape, lambda *_: (0,) * nd, pipeline_mode=pl.Buffered(1))


def _rms(x, g):
    return x * lax.rsqrt(jnp.mean(x * x, axis=-1, keepdims=True) + EPS) * g


def _dot(a, b):
    return jnp.dot(a, b, preferred_element_type=F32)


def _dot_exact(a, b):
    return jnp.dot(a, b, preferred_element_type=F32, precision=HIGHEST)


def _dot_nt(a, b):
    return lax.dot_general(a, b, (((1,), (1,)), ((), ())), preferred_element_type=F32)


def _expand_exact(x, e3):
    hi = x.astype(BF16)
    r1 = x - hi.astype(F32)
    mid = r1.astype(BF16)
    lo = (r1 - mid.astype(F32)).astype(BF16)
    return _dot(jnp.concatenate([hi, mid, lo], axis=1), e3)


def _softplus(x):
    return jnp.maximum(x, 0.0) + jnp.log1p(jnp.exp(-jnp.abs(x)))


def _row_tile(m, cap, mult=16):
    for t in range(min(cap, m), mult - 1, -1):
        if m % t == 0 and t % mult == 0:
            return t
    raise ValueError(f"no row tile for {m}")


def _ffn_kernel(x_ref, pre_ref, wg_ref, wu_ref, wd_ref, post_ref, o_ref, *, chunks):
    x = x_ref[...]
    xn = _rms(x, pre_ref[...]).astype(BF16)
    acc = jnp.zeros(x.shape, F32)
    for lo, sz in chunks:
        g = _dot(xn, wg_ref[:, lo:lo + sz])
        u = _dot(xn, wu_ref[:, lo:lo + sz])
        h = (jax.nn.silu(g) * u).astype(BF16)
        acc = acc + _dot(h, wd_ref[lo:lo + sz, :])
    o_ref[...] = x + 0.5 * _rms(acc, post_ref[...])


def _ffn_chunks(f):
    out, lo = [], 0
    while lo < f:
        sz = min(1024, f - lo)
        out.append((lo, sz))
        lo += sz
    return tuple(out)


def _ffn_stream_kernel(x_ref, pre_ref, wg_ref, wu_ref, wd_ref, post_ref, o_ref, xn_ref, acc_ref):
    j = pl.program_id(0)

    @pl.when(j == 0)
    def _():
        xn_ref[...] = _rms(x_ref[...], pre_ref[...]).astype(BF16)
        acc_ref[...] = jnp.zeros(acc_ref.shape, F32)

    xn = xn_ref[...]
    h = (jax.nn.silu(_dot(xn, wg_ref[...])) * _dot(xn, wu_ref[...])).astype(BF16)
    acc_ref[...] += _dot(h, wd_ref[...])

    @pl.when(j == pl.num_programs(0) - 1)
    def _():
        o_ref[...] = x_ref[...] + 0.5 * _rms(acc_ref[...], post_ref[...])


def _ffn(x, pre_g, wg, wu, wd, post_g):
    m, d = x.shape
    f = wg.shape[1]
    tm = _row_tile(m, 512)
    ft = 256
    if tm == m and f % ft == 0:
        return pl.pallas_call(
            _ffn_stream_kernel,
            out_shape=jax.ShapeDtypeStruct((m, d), F32),
            grid=(f // ft,),
            in_specs=[pl.BlockSpec((m, d), lambda j: (0, 0)), pl.BlockSpec((1, d), lambda j: (0, 0)),
                      pl.BlockSpec((d, ft), lambda j: (0, j)), pl.BlockSpec((d, ft), lambda j: (0, j)),
                      pl.BlockSpec((ft, d), lambda j: (j, 0)), pl.BlockSpec((1, d), lambda j: (0, 0))],
            out_specs=pl.BlockSpec((m, d), lambda j: (0, 0)),
            scratch_shapes=[pltpu.VMEM((m, d), BF16), pltpu.VMEM((m, d), F32)],
            compiler_params=_cparams(("arbitrary",), 32),
            name="ffn_block_stream",
        )(x, pre_g, wg, wu, wd, post_g)
    return pl.pallas_call(
        functools.partial(_ffn_kernel, chunks=_ffn_chunks(f)),
        out_shape=jax.ShapeDtypeStruct((m, d), F32),
        grid=(m // tm,),
        in_specs=[pl.BlockSpec((tm, d), lambda i: (i, 0)), _resident((1, d)), _resident((d, f)),
                  _resident((d, f)), _resident((f, d)), _resident((1, d))],
        out_specs=pl.BlockSpec((tm, d), lambda i: (i, 0)),
        compiler_params=_cparams(("arbitrary",), 52),
        name="ffn_block",
    )(x, pre_g, wg, wu, wd, post_g)


def _inproj_kernel(x_ref, g_ref, wbig_ref, wsmall_ref, proj_ref, small_ref, *, nslab):
    xn = _rms(x_ref[...], g_ref[...]).astype(BF16)
    for s in range(nslab):
        proj_ref[s] = _dot(xn, wbig_ref[:, s * SLAB:(s + 1) * SLAB]).astype(proj_ref.dtype)
    small_ref[...] = _dot(xn, wsmall_ref[...])


def _inproj_stream_kernel(x_ref, g_ref, wbig_ref, wsmall_ref, proj_ref, small_ref, xn_ref):
    @pl.when(pl.program_id(0) == 0)
    def _():
        xn_ref[...] = _rms(x_ref[...], g_ref[...]).astype(BF16)
        small_ref[...] = _dot(xn_ref[...], wsmall_ref[...])

    proj_ref[0] = _dot(xn_ref[...], wbig_ref[...]).astype(proj_ref.dtype)


def _inproj(h, g, wbig, wsmall, proj_dtype):
    m, d = h.shape
    nslab = wbig.shape[1] // SLAB
    ns = wsmall.shape[1]
    tm = _row_tile(m, 512)
    if tm == m:
        return pl.pallas_call(
            _inproj_stream_kernel,
            out_shape=(jax.ShapeDtypeStruct((nslab, m, SLAB), proj_dtype), jax.ShapeDtypeStruct((m, ns), F32)),
            grid=(nslab,),
            in_specs=[pl.BlockSpec((m, d), lambda s: (0, 0)), pl.BlockSpec((1, d), lambda s: (0, 0)),
                      pl.BlockSpec((d, SLAB), lambda s: (0, s)), pl.BlockSpec(wsmall.shape, lambda s: (0, 0))],
            out_specs=(pl.BlockSpec((1, m, SLAB), lambda s: (s, 0, 0)), pl.BlockSpec((m, ns), lambda s: (0, 0))),
            scratch_shapes=[pltpu.VMEM((m, d), BF16)],
            compiler_params=_cparams(("arbitrary",), 32),
            name="in_proj_stream",
        )(h, g, wbig, wsmall)
    return pl.pallas_call(
        functools.partial(_inproj_kernel, nslab=nslab),
        out_shape=(jax.ShapeDtypeStruct((nslab, m, SLAB), proj_dtype), jax.ShapeDtypeStruct((m, ns), F32)),
        grid=(m // tm,),
        in_specs=[pl.BlockSpec((tm, d), lambda i: (i, 0)), _resident((1, d)), _resident(wbig.shape),
                  _resident(wsmall.shape)],
        out_specs=(pl.BlockSpec((nslab, tm, SLAB), lambda i: (0, i, 0)), pl.BlockSpec((tm, ns), lambda i: (i, 0))),
        compiler_params=_cparams(("arbitrary",), 52),
        name="in_proj",
    )(h, g, wbig, wsmall)


def _qkv_kernel(s7_ref, small_ref, pos_ref, qg_ref, kvg_ref, wq_ref, wuk_ref, inv_ref, sign_ref,
                ckv_ref, kpe_ref, kcat_ref, q_ref, *, tqb):
    s7 = s7_ref[0].astype(F32)
    tm = s7.shape[0]
    q_lora = qg_ref.shape[1]
    kv_lora = kvg_ref.shape[1]
    nh = MLA_HEADS
    c_kv = _rms(s7[:, q_lora:q_lora + kv_lora], kvg_ref[...])
    ckv_ref[...] = c_kv
    ang = pos_ref[...] * inv_ref[...]
    cos = jnp.cos(ang)
    sin = jnp.sin(ang) * sign_ref[...]
    small = small_ref[...]
    kpe = small[:, 128:256] * cos + small[:, 256:384] * sin
    kpe_ref[...] = kpe
    kcat_ref[:, 0:kv_lora] = c_kv.astype(kcat_ref.dtype)
    kcat_ref[:, kv_lora:kv_lora + 128] = kpe.astype(kcat_ref.dtype)

    qn = _rms(s7[:, 0:q_lora], qg_ref[...] * Q_SCALE).astype(BF16)
    nope_w = nh * QK_NOPE
    pe_w = nh * QK_ROPE
    q_nope = _dot(qn, wq_ref[:, 0:nope_w])
    q_pe = _dot(qn, wq_ref[:, nope_w:nope_w + pe_w])
    q_rot = _dot(qn, wq_ref[:, nope_w + pe_w:nope_w + 2 * pe_w])
    cos_w = jnp.concatenate([cos] * (pe_w // 128), axis=1)
    sin_w = jnp.concatenate([sin] * (pe_w // 128), axis=1)
    q_pe = q_pe * cos_w + q_rot * sin_w
    lane = lax.broadcasted_iota(jnp.int32, (1, 128), 1)
    nblk = tm // tqb
    for p in range(nh // 2):
        qn_pair = q_nope[:, p * 128:(p + 1) * 128].astype(BF16)
        q_lat = _dot(qn_pair, wuk_ref[p])
        for e in range(2):
            hd = 2 * p + e
            pe_blk = q_pe[:, (hd // 4) * 128:(hd // 4 + 1) * 128]
            sel = (lane >= (hd % 4) * QK_ROPE) & (lane < (hd % 4 + 1) * QK_ROPE)
            pe_blk = jnp.where(sel, pe_blk, 0.0)
            lat = q_lat[:, e * kv_lora:(e + 1) * kv_lora]
            for g in range(nblk):
                q_ref[g, hd, :, 0:kv_lora] = lat[g * tqb:(g + 1) * tqb].astype(q_ref.dtype)
                q_ref[g, hd, :, kv_lora:kv_lora + 128] = pe_blk[g * tqb:(g + 1) * tqb].astype(q_ref.dtype)


def _qkv_prep(proj3, small, pos, qg, kvg, wq, wuk, inv128, sign128, tqb, dtype):
    nslab, m, _ = proj3.shape
    kv_lora = kvg.shape[1]
    tm = _row_tile(m, 512, max(16, tqb))
    nblk = tm // tqb
    return pl.pallas_call(
        functools.partial(_qkv_kernel, tqb=tqb),
        out_shape=(jax.ShapeDtypeStruct((m, kv_lora), F32), jax.ShapeDtypeStruct((m, 128), F32),
                   jax.ShapeDtypeStruct((m, QK_DIM), dtype),
                   jax.ShapeDtypeStruct((m // tqb, MLA_HEADS, tqb, QK_DIM), dtype)),
        grid=(m // tm,),
        in_specs=[pl.BlockSpec((1, tm, SLAB), lambda i: (nslab - 1, i, 0)),
                  pl.BlockSpec((tm, small.shape[1]), lambda i: (i, 0)),
                  pl.BlockSpec((tm, 1), lambda i: (i, 0)),
                  _resident(qg.shape), _resident(kvg.shape), _resident(wq.shape), _resident(wuk.shape),
                  _resident((1, 128)), _resident((1, 128))],
        out_specs=(pl.BlockSpec((tm, kv_lora), lambda i: (i, 0)), pl.BlockSpec((tm, 128), lambda i: (i, 0)),
                   pl.BlockSpec((tm, QK_DIM), lambda i: (i, 0)),
                   pl.BlockSpec((nblk, MLA_HEADS, tqb, QK_DIM), lambda i: (i, 0, 0, 0))),
        compiler_params=_cparams(("arbitrary",), 48),
        name="qkv_prep",
    )(proj3, small, pos, qg, kvg, wq, wuk, inv128, sign128)


def _ssd_chunk(xh_ref, dt_raw, state_ref, apad_ref, y_ref, lo, hi, p, bf16_input, rows_out=CHUNK):
    d_inner = state_ref.shape[1]
    nheads = d_inner // SSM_HEAD_DIM
    gw = d_inner // SSM_GROUPS
    rows = lax.broadcasted_iota(jnp.int32, (CHUNK, 1), 0)
    valid = (rows >= lo) & (rows < hi)
    ntap = CONV_K - 1

    def conv_rows(nrows):
        out = p["conv_b"][...]
        for k in range(CONV_K):
            out = out + p["conv_w"][k:k + 1, :] * xh_ref[pl.ds(8 - ntap + k, nrows), :]
        return out

    if bf16_input:
        cur = xh_ref[8:8 + CHUNK, :]
        shifted = _dot(p["shift"][...], cur.astype(BF16))
        conv = p["conv_b"][...]
        for k in range(ntap):
            conv = conv + p["conv_w"][k:k + 1, :] * shifted[k * CHUNK:(k + 1) * CHUNK]
        conv = conv + p["conv_w"][ntap:CONV_K, :] * cur
        conv = jnp.concatenate([conv_rows(8), conv[8:]], axis=0)
    else:
        conv = conv_rows(CHUNK)
    xc = jnp.where(valid, jax.nn.silu(conv), 0.0)
    xs = xc[:, 0:d_inner]
    bm = xc[:, d_inner:d_inner + SSM_GROUPS * D_STATE]
    cm = xc[:, d_inner + SSM_GROUPS * D_STATE:]

    dt = jnp.where(valid, _softplus(dt_raw + p["dt_bias"][...]), 0.0)
    da = dt * (-jnp.exp(p["a_log"][...]))
    a_cs = _dot_exact(p["tril"][...], da) * math.log2(math.e)
    dt_x = _expand_exact(dt, p["expand"][...])
    acs_x = _expand_exact(a_cs, p["expand"][...])
    alast_x = acs_x[CHUNK - 1:CHUNK, :]
    apad_ref[...] = a_cs.T
    xdt = xs * dt_x
    xds = (xdt * jnp.exp2(alast_x - acs_x)).astype(BF16)
    exp_acs = jnp.exp2(acs_x)
    causal = rows >= lax.broadcasted_iota(jnp.int32, (1, CHUNK), 1)
    lane = lax.broadcasted_iota(jnp.int32, (1, 128), 1)
    hpg = nheads // SSM_GROUPS

    ro = rows_out
    for g in range(SSM_GROUPS):
        gs = slice(g * gw, (g + 1) * gw)
        bg = bm[:, g * D_STATE:(g + 1) * D_STATE]
        cg = cm[0:ro, g * D_STATE:(g + 1) * D_STATE].astype(BF16)
        cb = _dot_nt(cg, bg.astype(BF16))
        st = state_ref[:, gs]
        y_off = _dot(cg, st.astype(BF16)) * exp_acs[0:ro, gs]
        state_ref[:, gs] = st * jnp.exp2(alast_x[:, gs]) + _dot(bg.T.astype(BF16), xds[:, gs])
        y_ref[0:ro, gs] = y_off + xs[0:ro, gs] * p["dskip_x"][:, gs]
        for pr in range(hpg // 2):
            w2 = []
            for e in range(2):
                hd = g * hpg + 2 * pr + e
                seg = jnp.broadcast_to(a_cs[0:ro, hd:hd + 1], (ro, 128)) - apad_ref[pl.ds(hd, 1), :]
                decay = jnp.exp2(jnp.where(causal[0:ro], seg, NEG))
                w2.append((cb * decay).astype(BF16))
            ls = slice(g * gw + pr * 128, g * gw + (pr + 1) * 128)
            xp = xdt[:, ls]
            x2 = jnp.concatenate([jnp.where(lane < SSM_HEAD_DIM, xp, 0.0),
                                  jnp.where(lane >= SSM_HEAD_DIM, xp, 0.0)], axis=0).astype(BF16)
            y_ref[0:ro, ls] = y_ref[0:ro, ls] + _dot(jnp.concatenate(w2, axis=1), x2)


def _gate_norm(y, z, g, d_inner):
    yg = y * jax.nn.silu(z)
    gw = d_inner // SSM_GROUPS
    outs = []
    for k in range(SSM_GROUPS):
        v = yg[:, k * gw:(k + 1) * gw]
        outs.append(_rms(v, g[:, k * gw:(k + 1) * gw]))
    return jnp.concatenate(outs, axis=1)


_SSD_PARAM_NAMES = ("conv_w", "conv_b", "dt_bias", "a_log", "dskip_x", "norm_g", "tril", "expand", "shift")


def _ssd_prompt_kernel(x0_ref, x1_ref, bc_ref, z0_ref, z1_ref, small_ref, m0_ref, m1_ref, mbc_ref, msmall_ref,
                       *rest, nc):
    np_ = len(_SSD_PARAM_NAMES)
    p = dict(zip(_SSD_PARAM_NAMES, rest[:np_]))
    yn_ref, hfin_ref, convnew_ref, xh_ref, state_ref, apad_ref, y_ref, state0_ref, tail0_ref = rest[np_:]
    b = pl.program_id(0)
    c = pl.program_id(1)
    d_inner = state_ref.shape[1]
    assert x0_ref.dtype == BF16
    shared_first = (c == 0) & (b == 0)

    @pl.when(shared_first)
    def _():
        xh_ref[0:8, :] = jnp.zeros((8, xh_ref.shape[1]), F32)
        state_ref[...] = jnp.zeros(state_ref.shape, F32)
        valid = lax.broadcasted_iota(jnp.int32, (CHUNK, 1), 0) >= LEAD
        for j, ref in enumerate((m0_ref, m1_ref, mbc_ref)):
            xh_ref[8:8 + CHUNK, j * SLAB:(j + 1) * SLAB] = jnp.where(valid, ref[0].astype(BF16).astype(F32), 0.0)

    @pl.when((c == 0) & (b > 0))
    def _():
        state_ref[...] = state0_ref[...]
        xh_ref[0:8, :] = tail0_ref[...]

    @pl.when(c > 0)
    def _():
        for j, ref in enumerate((x0_ref, x1_ref, bc_ref)):
            xh_ref[8:8 + CHUNK, j * SLAB:(j + 1) * SLAB] = ref[0].astype(F32)

    @pl.when((c > 0) | (b == 0))
    def _():
        lo = jnp.where(c == 0, LEAD, 0)
        dt_raw = jnp.where(c == 0, msmall_ref[:, 0:128], small_ref[:, 0:128])
        _ssd_chunk(xh_ref, dt_raw, state_ref, apad_ref, y_ref, lo, CHUNK, p, True)
        z = jnp.concatenate([z0_ref[0], z1_ref[0]], axis=1).astype(F32)
        yn_ref[...] = _gate_norm(y_ref[...], z, p["norm_g"][...], d_inner).astype(yn_ref.dtype)
        xh_ref[0:8, :] = xh_ref[CHUNK:CHUNK + 8, :]

    @pl.when(shared_first)
    def _():
        state0_ref[...] = state_ref[...]
        tail0_ref[...] = xh_ref[0:8, :]

    @pl.when(c == nc - 1)
    def _():
        convnew_ref[0] = xh_ref[pl.ds(8 + CHUNK - (CONV_K - 1), CONV_K - 1), :]
        for j in range(d_inner // 128):
            hfin_ref[0, j * 128:(j + 1) * 128, :] = state_ref[:, j * 128:(j + 1) * 128].T


def _ssd_prompt(proj3, small, proj3m, smallm, meta_tile, params, bp, nc):
    d_inner = params["expand"].shape[1]
    conv_dim = params["conv_b"].shape[1]
    m = proj3.shape[1]
    ns = small.shape[1]
    ncx = nc - 1
    plist = [params[k] for k in _SSD_PARAM_NAMES]

    def xblk(b, c):
        return b * ncx + jnp.maximum(c - 1, 0)

    def slab(s):
        return pl.BlockSpec((1, CHUNK, SLAB), lambda b, c, s=s: (s, xblk(b, c), 0))

    def mslab(s):
        return pl.BlockSpec((1, CHUNK, SLAB), lambda b, c, s=s: (s, meta_tile, 0))

    return pl.pallas_call(
        functools.partial(_ssd_prompt_kernel, nc=nc),
        out_shape=(jax.ShapeDtypeStruct((m, d_inner), BF16),
                   jax.ShapeDtypeStruct((bp, d_inner, D_STATE), F32),
                   jax.ShapeDtypeStruct((bp, CONV_K - 1, conv_dim), F32)),
        grid=(bp, nc),
        in_specs=[slab(2), slab(3), slab(4), slab(0), slab(1),
                  pl.BlockSpec((CHUNK, ns), lambda b, c: (xblk(b, c), 0)),
                  mslab(2), mslab(3), mslab(4), pl.BlockSpec((CHUNK, ns), lambda b, c: (meta_tile, 0))]
                 + [_resident(a.shape) for a in plist],
        out_specs=(pl.BlockSpec((CHUNK, d_inner), lambda b, c: (xblk(b, c), 0)),
                   pl.BlockSpec((1, d_inner, D_STATE), lambda b, c: (b, 0, 0)),
                   pl.BlockSpec((1, CONV_K - 1, conv_dim), lambda b, c: (b, 0, 0))),
        scratch_shapes=[pltpu.VMEM((8 + CHUNK, conv_dim), F32), pltpu.VMEM((D_STATE, d_inner), F32),
                        pltpu.VMEM((CHUNK, 128), F32), pltpu.VMEM((CHUNK, d_inner), F32),
                        pltpu.VMEM((D_STATE, d_inner), F32), pltpu.VMEM((8, conv_dim), F32)],
        compiler_params=_cparams(("arbitrary", "arbitrary"), 40),
        name="ssd_prompt",
    )(proj3, proj3, proj3, proj3, proj3, small, proj3m, proj3m, proj3m, smallm, *plist)


def _ssd_sample_kernel(x0_ref, x1_ref, bc_ref, z0_ref, z1_ref, small_ref, conv0_ref, h0_ref, *rest, tdec):
    np_ = len(_SSD_PARAM_NAMES)
    p = dict(zip(_SSD_PARAM_NAMES, rest[:np_]))
    yn_ref, hnew_ref, convnew_ref, xh_ref, state_ref, apad_ref, y_ref, dt_ref = rest[np_:]
    d_inner = state_ref.shape[1]
    nk = CONV_K - 1
    xh_ref[...] = jnp.zeros(xh_ref.shape, F32)
    xh_ref[8 - nk:8, :] = conv0_ref[0]
    xh_ref[8:8 + tdec, 0:SLAB] = x0_ref[0]
    xh_ref[8:8 + tdec, SLAB:2 * SLAB] = x1_ref[0]
    xh_ref[8:8 + tdec, 2 * SLAB:3 * SLAB] = bc_ref[0]
    dt_ref[...] = jnp.zeros(dt_ref.shape, F32)
    dt_ref[0:tdec, :] = small_ref[:, 0:128]
    for j in range(d_inner // 128):
        state_ref[:, j * 128:(j + 1) * 128] = h0_ref[0, j * 128:(j + 1) * 128, :].T
    _ssd_chunk(xh_ref, dt_ref[...], state_ref, apad_ref, y_ref, 0, tdec, p, False, rows_out=tdec)
    z = jnp.concatenate([z0_ref[0], z1_ref[0]], axis=1)
    yn_ref[...] = _gate_norm(y_ref[0:tdec, :], z, p["norm_g"][...], d_inner)
    convnew_ref[0] = xh_ref[pl.ds(8 + tdec - nk, nk), :]
    for j in range(d_inner // 128):
        hnew_ref[0, j * 128:(j + 1) * 128, :] = state_ref[:, j * 128:(j + 1) * 128].T


def _ssd_sample(proj3, small, conv0, h0, params, bs, tdec):
    d_inner = params["expand"].shape[1]
    conv_dim = params["conv_b"].shape[1]
    ns = small.shape[1]
    plist = [params[k] for k in _SSD_PARAM_NAMES]

    def slab(s):
        return pl.BlockSpec((1, tdec, SLAB), lambda b, s=s: (s, b, 0))

    return pl.pallas_call(
        functools.partial(_ssd_sample_kernel, tdec=tdec),
        out_shape=(jax.ShapeDtypeStruct((bs * tdec, d_inner), F32),
                   jax.ShapeDtypeStruct((bs, d_inner, D_STATE), F32),
                   jax.ShapeDtypeStruct((bs, CONV_K - 1, conv_dim), F32)),
        grid=(bs,),
        in_specs=[slab(2), slab(3), slab(4), slab(0), slab(1), pl.BlockSpec((tdec, ns), lambda b: (b, 0)),
                  pl.BlockSpec((1, CONV_K - 1, conv_dim), lambda b: (b, 0, 0)),
                  pl.BlockSpec((1, d_inner, D_STATE), lambda b: (b, 0, 0))] + [_resident(a.shape) for a in plist],
        out_specs=(pl.BlockSpec((tdec, d_inner), lambda b: (b, 0)),
                   pl.BlockSpec((1, d_inner, D_STATE), lambda b: (b, 0, 0)),
                   pl.BlockSpec((1, CONV_K - 1, conv_dim), lambda b: (b, 0, 0))),
        scratch_shapes=[pltpu.VMEM((8 + CHUNK, conv_dim), F32), pltpu.VMEM((D_STATE, d_inner), F32),
                        pltpu.VMEM((CHUNK, 128), F32), pltpu.VMEM((CHUNK, d_inner), F32),
                        pltpu.VMEM((CHUNK, 128), F32)],
        compiler_params=_cparams(("arbitrary",), 40),
        name="ssd_sample",
    )(proj3, proj3, proj3, proj3, proj3, small, conv0, h0, *plist)


def _attn_out(acc, l, wuv_ref, nh, tq):
    o_lat = acc / l
    outs = []
    for p in range(nh // 2):
        pair = jnp.concatenate([o_lat[(2 * p) * tq:(2 * p + 1) * tq], o_lat[(2 * p + 1) * tq:(2 * p + 2) * tq]],
                               axis=1).astype(BF16)
        outs.append(_dot(pair, wuv_ref[p]))
    return jnp.concatenate(outs, axis=1)


def _attn_prompt_kernel(q_ref, k_ref, vt_ref, wuvt_ref, o_ref, m_ref, l_ref, acc_ref, p_ref, *, tk, lp, q_off):
    qi = pl.program_id(1) + q_off
    nh, tq = q_ref.shape[1], q_ref.shape[2]
    hpg = q_ref.shape[1] // m_ref.shape[0]
    ngrp = nh // hpg
    gq = hpg * tq
    m_ref[...] = jnp.full(m_ref.shape, NEG, F32)
    l_ref[...] = jnp.zeros(l_ref.shape, F32)
    qidx = qi * tq + (lax.broadcasted_iota(jnp.int32, (1, gq), 1) & (tq - 1))
    nkc = ((qi + 1) * tq + tk - 1) // tk

    def chunk_start(kc):
        return pl.multiple_of(jnp.minimum(kc * tk, lp - tk), 128)

    def stage(kc, scores, values, diagonal):
        if scores:
            k = k_ref[pl.ds(chunk_start(kc), tk), :]
            kidx = chunk_start(kc) + lax.broadcasted_iota(jnp.int32, (tk, 1), 0)
        if values:
            vt = vt_ref[0, :, pl.ds(chunk_start(kc - 1), tk)]
        for r in range(ngrp):
            if values:
                acc = acc_ref[r] + _dot(vt, p_ref[r])
            else:
                acc = jnp.zeros(acc_ref.shape[1:], F32)
            if scores:
                q = q_ref[0, hpg * r:hpg * (r + 1)].reshape(gq, q_ref.shape[3])
                s = _dot_nt(k, q)
                if diagonal:
                    s = jnp.where((kidx <= qidx) & (kidx >= jnp.maximum(kc * tk, LEAD)), s, NEG)
                else:
                    s = jnp.concatenate([jnp.where(kidx[0:CHUNK] >= LEAD, s[0:CHUNK], NEG), s[CHUNK:]], axis=0)
                m_old = m_ref[r:r + 1, :]
                m_new = jnp.maximum(m_old, jnp.max(s, axis=0, keepdims=True))
                pexp = jnp.exp2(s - m_new)
                alpha = jnp.exp2(m_old - m_new)
                l_ref[r:r + 1, :] = alpha * l_ref[r:r + 1, :] + jnp.sum(pexp, axis=0, keepdims=True)
                p_ref[r] = pexp.astype(BF16)
                m_ref[r:r + 1, :] = m_new
                if values:
                    acc = alpha * acc
            acc_ref[r] = acc

    def body(kc, carry):
        stage(kc, True, True, False)
        return carry

    @pl.when(nkc == 1)
    def _():
        stage(0, True, False, True)

    @pl.when(nkc > 1)
    def _():
        stage(0, True, False, False)
        lax.fori_loop(1, nkc - 1, body, 0)
        stage(nkc - 1, True, True, True)

    stage(nkc, False, True, False)
    for r in range(ngrp):
        o_lat_t = (acc_ref[r] * (1.0 / l_ref[r:r + 1, :])).astype(BF16)
        for j in range(hpg // 2):
            h0 = hpg * r + 2 * j
            pair = jnp.concatenate([_dot(wuvt_ref[h0 + e], o_lat_t[:, (2 * j + e) * tq:(2 * j + e + 1) * tq])
                                    for e in range(2)], axis=0)
            o_ref[:, h0 * V_HEAD:(h0 + 2) * V_HEAD] = pair.T.astype(o_ref.dtype)


def _attn_prompt(q4, kcat, vt, wuvt, bp, lp):
    q_off = 1
    nq = lp // CHUNK - q_off
    nh = MLA_HEADS
    kv_lora = vt.shape[1]
    tk = 512
    hpg = 4
    assert lp % 128 == 0 and lp >= tk and 2 * V_HEAD == 128 and nh % hpg == 0
    return pl.pallas_call(
        functools.partial(_attn_prompt_kernel, tk=tk, lp=lp, q_off=q_off),
        out_shape=jax.ShapeDtypeStruct((bp * nq * CHUNK, nh * V_HEAD), BF16),
        grid=(bp, nq),
        in_specs=[pl.BlockSpec((1, nh, CHUNK, QK_DIM), lambda b, i: (b * nq + i, 0, 0, 0)),
                  pl.BlockSpec((lp, QK_DIM), lambda b, i: (b, 0)),
                  pl.BlockSpec((1, kv_lora, lp), lambda b, i: (b, 0, 0)),
                  _resident(wuvt.shape)],
        out_specs=pl.BlockSpec((CHUNK, nh * V_HEAD), lambda b, i: (b * nq + i, 0)),
        scratch_shapes=[pltpu.VMEM((nh // hpg, hpg * CHUNK), F32), pltpu.VMEM((nh // hpg, hpg * CHUNK), F32),
                        pltpu.VMEM((nh // hpg, kv_lora, hpg * CHUNK), F32),
                        pltpu.VMEM((nh // hpg, tk, hpg * CHUNK), BF16)],
        compiler_params=_cparams(("arbitrary", "arbitrary"), 40),
        name="attn_prompt",
    )(q4, kcat, vt, wuvt)


def _attn_sample_kernel(pt_ref, q_ref, knew_ref, vnew_ref, foldt_ref, wuv_ref, ckv_hbm, krt_hbm, o_ref,
                        cbuf, rbuf, sem, m_ref, l_ref, acc_ref, q_scr, qlt_scr, qpt_scr, *, npg, nchunk, tdec):
    b = pl.program_id(0)
    c = pl.program_id(1)
    nb = pl.num_programs(0)
    step = b * nchunk + c
    slot = step % 2
    nh = q_ref.shape[1]
    rows = nh * tdec
    kv_lora = cbuf.shape[-1]
    page = cbuf.shape[2]

    def copies(bb, cc, sl):
        out = []
        for j in range(npg):
            pg = pt_ref[bb, cc * npg + j]
            out.append(pltpu.make_async_copy(ckv_hbm.at[pg], cbuf.at[sl, j], sem.at[0, sl]))
            out.append(pltpu.make_async_copy(krt_hbm.at[pg], rbuf.at[sl, :, pl.ds(j * page, page)], sem.at[1, sl]))
        return out

    @pl.when(step == 0)
    def _():
        for i, cp in enumerate(copies(b, c, slot)):
            cp.start(priority=(i // 2) % 2)

    @pl.when(step + 1 < nb * nchunk)
    def _():
        nxt = step + 1
        for i, cp in enumerate(copies(nxt // nchunk, nxt % nchunk, 1 - slot)):
            cp.start(priority=(i // 2) % 2)

    @pl.when(c == 0)
    def _():
        m_ref[...] = jnp.full(m_ref.shape, NEG, F32)
        l_ref[...] = jnp.zeros(l_ref.shape, F32)
        acc_ref[...] = jnp.zeros(acc_ref.shape, F32)
        q = q_ref[0].reshape(rows, q_ref.shape[3])
        q_scr[...] = q.astype(BF16)
        for j in range(kv_lora // 128):
            qlt_scr[j * 128:(j + 1) * 128, :] = q[:, j * 128:(j + 1) * 128].T.astype(BF16)
        qpt_scr[...] = lax.dot_general(foldt_ref[...], q[:, kv_lora:kv_lora + 128], (((1,), (1,)), ((), ())),
                                       preferred_element_type=F32, precision=HIGHEST).astype(BF16)

    for cp in copies(b, c, slot):
        cp.wait()

    def lanes_to_rows(x):
        return jnp.broadcast_to(x, (rows, rows)).T

    def widen(x):
        return jnp.concatenate([lanes_to_rows(x)] * (kv_lora // 128), axis=1)

    def update(s_t, v):
        m_old = m_ref[...]
        m_new = jnp.maximum(m_old, jnp.max(s_t, axis=0, keepdims=True))
        p_t = jnp.exp2(s_t - m_new)
        alpha = jnp.exp2(m_old - m_new)
        l_ref[...] = alpha * l_ref[...] + jnp.sum(p_t, axis=0, keepdims=True)
        m_ref[...] = m_new
        nk = s_t.shape[0]
        if nk % 128:
            pmat = p_t.T
        else:
            pmat = jnp.concatenate([p_t[j * 128:(j + 1) * 128, :].T for j in range(nk // 128)], axis=1)
        acc_ref[...] = acc_ref[...] * widen(alpha) + _dot(pmat.astype(BF16), v)

    kc = cbuf[slot].reshape(npg * page, kv_lora).astype(BF16)
    kr = jnp.concatenate([rbuf[slot, :, j * page:(j + 1) * page].T for j in range(npg)],
                         axis=0).astype(BF16)
    update(_dot(kc, qlt_scr[...]) + _dot(kr, qpt_scr[...]), kc)

    @pl.when(c == nchunk - 1)
    def _():
        knew = knew_ref[...].astype(BF16)
        s_t = _dot_nt(knew, q_scr[...])
        nk = knew.shape[0]
        kid = lax.broadcasted_iota(jnp.int32, (nk, 1), 0)
        t = lax.broadcasted_iota(jnp.int32, (1, rows), 1) & (tdec - 1)
        ok = (kid >= b * tdec) & (kid <= b * tdec + t)
        update(jnp.where(ok, s_t, NEG), vnew_ref[...].astype(BF16))
        o_ref[...] = _attn_out(acc_ref[...], widen(l_ref[...]), wuv_ref, nh, tdec)


def _attn_sample(page_table, q4, kcat_new, v_new, fold, wuv, cache_ckv, cache_krt, bs, tdec):
    nh = MLA_HEADS
    n_pages = page_table.shape[1]
    page, kv_lora = cache_ckv.shape[1], cache_ckv.shape[2]
    rope = cache_krt.shape[1]
    npg = 64
    while n_pages % npg:
        npg //= 2
    nchunk = n_pages // npg
    rows = nh * tdec
    assert tdec & (tdec - 1) == 0 and rows == 128 and page == 128
    grid_spec = pltpu.PrefetchScalarGridSpec(
        num_scalar_prefetch=1,
        grid=(bs, nchunk),
        in_specs=[pl.BlockSpec((1, nh, tdec, QK_DIM), lambda b, c, pt: (b, 0, 0, 0)),
                  pl.BlockSpec(kcat_new.shape, lambda b, c, pt: (0, 0), pipeline_mode=pl.Buffered(1)),
                  pl.BlockSpec(v_new.shape, lambda b, c, pt: (0, 0), pipeline_mode=pl.Buffered(1)),
                  pl.BlockSpec(fold.shape, lambda b, c, pt: (0, 0), pipeline_mode=pl.Buffered(1)),
                  pl.BlockSpec(wuv.shape, lambda b, c, pt: (0, 0, 0), pipeline_mode=pl.Buffered(1)),
                  pl.BlockSpec(memory_space=pl.ANY), pl.BlockSpec(memory_space=pl.ANY)],
        out_specs=pl.BlockSpec((tdec, nh * V_HEAD), lambda b, c, pt: (b, 0)),
        scratch_shapes=[pltpu.VMEM((2, npg, page, kv_lora), F32), pltpu.VMEM((2, rope, npg * page), F32),
                        pltpu.SemaphoreType.DMA((2, 2)),
                        pltpu.VMEM((1, rows), F32), pltpu.VMEM((1, rows), F32), pltpu.VMEM((rows, kv_lora), F32),
                        pltpu.VMEM((rows, QK_DIM), BF16), pltpu.VMEM((kv_lora, rows), BF16),
                        pltpu.VMEM((rope, rows), BF16)],
    )
    return pl.pallas_call(
        functools.partial(_attn_sample_kernel, npg=npg, nchunk=nchunk, tdec=tdec),
        out_shape=jax.ShapeDtypeStruct((bs * tdec, nh * V_HEAD), F32),
        grid_spec=grid_spec,
        compiler_params=_cparams(("arbitrary", "arbitrary"), 40),
        name="attn_sample",
    )(page_table, q4, kcat_new, v_new, fold, wuv, cache_ckv, cache_krt)


def _merge_kernel(yn_ref, o_ref, ga_ref, gb_ref, h_ref, wa_ref, wb_ref, wo_ref, g_ref, out_ref):
    a = _dot(yn_ref[...].astype(BF16), wa_ref[...])
    bb = _dot(o_ref[...].astype(BF16), wb_ref[...])
    merged = jax.nn.sigmoid(ga_ref[0].astype(F32)) * a + jax.nn.sigmoid(gb_ref[0].astype(F32)) * bb
    mix = _dot(merged.astype(BF16), wo_ref[...])
    out_ref[...] = h_ref[...] + _rms(mix, g_ref[...])


def _merge(yn, o, proj3, h, wa, wb, wo, g):
    m, d = yn.shape[0], h.shape[1]
    tm = _row_tile(m, 512)
    return pl.pallas_call(
        _merge_kernel,
        out_shape=jax.ShapeDtypeStruct((m, d), F32),
        grid=(m // tm,),
        in_specs=[pl.BlockSpec((tm, yn.shape[1]), lambda i: (i, 0)), pl.BlockSpec((tm, o.shape[1]), lambda i: (i, 0)),
                  pl.BlockSpec((1, tm, SLAB), lambda i: (5, i, 0)), pl.BlockSpec((1, tm, SLAB), lambda i: (6, i, 0)),
                  pl.BlockSpec((tm, d), lambda i: (i, 0)),
                  _resident(wa.shape), _resident(wb.shape), _resident(wo.shape), _resident((1, d))],
        out_specs=pl.BlockSpec((tm, d), lambda i: (i, 0)),
        compiler_params=_cparams(("arbitrary",), 40),
        name="gated_merge",
    )(yn, o, proj3, proj3, h, wa, wb, wo, g)


def _prep_weights(w, d_inner, conv_dim, nheads, q_lora, kv_lora):
    d = w["w_in"].shape[0]
    offs = np.cumsum([0, d_inner, conv_dim, nheads, q_lora, kv_lora, QK_ROPE, d, d])
    seg = {k: w["w_in"][:, offs[i]:offs[i + 1]]
           for i, k in enumerate(("z", "xbc", "dt", "q_a", "kv_a", "k_pe", "ga", "gb"))}
    pad = SLAB - q_lora - kv_lora
    wbig = jnp.concatenate([seg["z"], seg["xbc"], seg["ga"], seg["gb"], seg["q_a"], seg["kv_a"],
                            jnp.zeros((d, pad), F32)], axis=1).astype(BF16)
    swap = np.concatenate([np.arange(QK_ROPE // 2, QK_ROPE), np.arange(QK_ROPE // 2)])
    wsmall = jnp.concatenate([seg["dt"], jnp.zeros((d, 128 - nheads), F32), jnp.tile(seg["k_pe"], (1, 4)),
                              jnp.tile(seg["k_pe"][:, swap], (1, 4))], axis=1).astype(BF16)
    nh = MLA_HEADS
    wq3 = w["w_q_b"].reshape(q_lora, nh, QK_NOPE + QK_ROPE)
    wq = jnp.concatenate([wq3[:, :, :QK_NOPE].reshape(q_lora, nh * QK_NOPE),
                          wq3[:, :, QK_NOPE:].reshape(q_lora, nh * QK_ROPE),
                          wq3[:, :, QK_NOPE:][:, :, swap].reshape(q_lora, nh * QK_ROPE)], axis=1).astype(BF16)
    wk = jnp.transpose(w["w_uk"], (1, 2, 0)).reshape(nh // 2, 2, QK_NOPE, kv_lora)
    zk = jnp.zeros((nh // 2, QK_NOPE, kv_lora), F32)
    wuk = jnp.concatenate([jnp.concatenate([wk[:, 0], zk], axis=2),
                           jnp.concatenate([zk, wk[:, 1]], axis=2)], axis=1).astype(BF16)
    wv = jnp.transpose(w["w_uv"], (1, 0, 2)).reshape(nh // 2, 2, kv_lora, V_HEAD)
    zv = jnp.zeros((nh // 2, kv_lora, V_HEAD), F32)
    wuvt = jnp.transpose(w["w_uv"], (1, 2, 0)).astype(BF16)
    wuv = jnp.concatenate([jnp.concatenate([wv[:, 0], zv], axis=2),
                           jnp.concatenate([zv, wv[:, 1]], axis=2)], axis=1).astype(BF16)
    def expand3(width):
        e = np.kron(np.eye(128, nheads, dtype=np.float32), np.ones((1, width), np.float32))
        return jnp.asarray(np.tile(e, (3, 1)), BF16)

    lane_pad = lambda a: jnp.pad(a, (0, 128 - nheads))[None]
    ssd = dict(conv_w=w["conv_w"], conv_b=w["conv_b"][None], dt_bias=lane_pad(w["dt_bias"]),
               a_log=lane_pad(w["a_log"]),
               dskip_x=jnp.repeat(w["d_skip"], SSM_HEAD_DIM)[None], norm_g=w["ssm_norm_g"][None],
               tril=jnp.asarray(np.tril(np.ones((CHUNK, CHUNK), np.float32))),
               expand=expand3(SSM_HEAD_DIM),
               shift=jnp.asarray(np.concatenate([np.eye(CHUNK, k=k - (CONV_K - 1), dtype=np.float32)
                                                 for k in range(CONV_K - 1)], axis=0), BF16))
    inv = ROPE_THETA ** (-jnp.arange(0, QK_ROPE, 2, dtype=F32) / QK_ROPE)
    inv128 = jnp.tile(jnp.concatenate([inv, inv]), 4)[None]
    sign128 = jnp.asarray(np.tile(np.concatenate([-np.ones(QK_ROPE // 2), np.ones(QK_ROPE // 2)]), 4)[None], F32)
    foldt = jnp.asarray(np.tile(np.eye(QK_ROPE, dtype=np.float32), (1, 4)))
    bf = lambda a: a.astype(BF16)
    row = lambda a: a[None]
    return dict(
        wbig=wbig, wsmall=wsmall, wq=wq, wuk=wuk, wuv=wuv, wuvt=wuvt, ssd=ssd, inv128=inv128, sign128=sign128, foldt=foldt,
        ffn1=(row(w["ffn1_pre_g"]), bf(w["ffn1_w_gate"]), bf(w["ffn1_w_up"]), bf(w["ffn1_w_down"]),
              row(w["ffn1_post_g"])),
        ffn2=(row(w["ffn2_pre_g"]), bf(w["ffn2_w_gate"]), bf(w["ffn2_w_up"]), bf(w["ffn2_w_down"]),
              row(w["ffn2_post_g"])),
        mix_pre_g=row(w["mix_pre_g"]), mix_post_g=row(w["mix_post_g"]), q_g=row(w["q_a_norm_g"]),
        kv_g=row(w["kv_a_norm_g"]), wa=bf(w["w_a_out"]), wb=bf(w["w_b_out"]), wo=bf(w["w_o"]))


def kernel(x_prompt, x_sample, cache_kv_latent, cache_k_rope, state_ssm, state_conv, page_table, meta_tokens,
           ffn1_pre_g, ffn1_w_gate, ffn1_w_up, ffn1_w_down, ffn1_post_g,
           mix_pre_g, w_in, conv_w, conv_b, dt_bias, a_log, d_skip, ssm_norm_g,
           q_a_norm_g, w_q_b, kv_a_norm_g, w_uk, w_uv, w_a_out, w_b_out, w_o, mix_post_g,
           ffn2_pre_g, ffn2_w_gate, ffn2_w_up, ffn2_w_down, ffn2_post_g):
    names = ("ffn1_pre_g", "ffn1_w_gate", "ffn1_w_up", "ffn1_w_down", "ffn1_post_g", "mix_pre_g", "w_in", "conv_w",
             "conv_b", "dt_bias", "a_log", "d_skip", "ssm_norm_g", "q_a_norm_g", "w_q_b", "kv_a_norm_g", "w_uk",
             "w_uv", "w_a_out", "w_b_out", "w_o", "mix_post_g", "ffn2_pre_g", "ffn2_w_gate", "ffn2_w_up",
             "ffn2_w_down", "ffn2_post_g")
    stacked = dict(zip(names, (ffn1_pre_g, ffn1_w_gate, ffn1_w_up, ffn1_w_down, ffn1_post_g, mix_pre_g, w_in,
                               conv_w, conv_b, dt_bias, a_log, d_skip, ssm_norm_g, q_a_norm_g, w_q_b, kv_a_norm_g,
                               w_uk, w_uv, w_a_out, w_b_out, w_o, mix_post_g, ffn2_pre_g, ffn2_w_gate, ffn2_w_up,
                               ffn2_w_down, ffn2_post_g)))
    depth = w_in.shape[0]
    bp, seq, d = x_prompt.shape
    bs, tdec, _ = x_sample.shape
    nheads = dt_bias.shape[1]
    d_inner = nheads * SSM_HEAD_DIM
    conv_dim = conv_b.shape[1]
    q_lora, kv_lora = q_a_norm_g.shape[1], kv_a_norm_g.shape[1]
    assert seq % CHUNK == 0 and conv_dim == 3 * SLAB and d_inner == 2 * SLAB and d == SLAB
    assert q_lora + kv_lora <= SLAB and CHUNK % tdec == 0
    n_pages, page = page_table.shape[1], cache_kv_latent.shape[2]
    past_len = n_pages * page
    lp = LEAD + N_META + seq
    nc = lp // CHUNK
    ns_rows = bs * tdec
    assert depth == 1 and ns_rows % CHUNK == 0
    meta_tile = ns_rows // CHUNK
    meta0 = ns_rows + LEAD

    hx = x_prompt.reshape(bp * seq, d)
    hs = jnp.concatenate([x_sample.reshape(ns_rows, d), jnp.zeros((LEAD, d), F32), meta_tokens], axis=0)
    pos_x = jnp.tile(N_META + jnp.arange(seq, dtype=jnp.int32), bp).astype(F32)[:, None]
    pos_s = jnp.concatenate([jnp.tile(past_len + jnp.arange(tdec, dtype=jnp.int32), bs),
                             jnp.arange(CHUNK, dtype=jnp.int32) - LEAD]).astype(F32)[:, None]

    outs = [[] for _ in range(8)]
    for l in range(depth):
        w = _prep_weights({k: v[l] for k, v in stacked.items()}, d_inner, conv_dim, nheads, q_lora, kv_lora)

        g1 = _ffn(hs, *w["ffn1"])
        proj3s, smalls = _inproj(g1, w["mix_pre_g"], w["wbig"], w["wsmall"], F32)
        c_kv_s, kpe_s, kcat_s, q4s = _qkv_prep(proj3s, smalls, pos_s, w["q_g"], w["kv_g"], w["wq"], w["wuk"],
                                               w["inv128"], w["sign128"], tdec, F32)

        h1 = _ffn(hx, *w["ffn1"])
        proj3, small = _inproj(h1, w["mix_pre_g"], w["wbig"], w["wsmall"], BF16)
        yn, h_fin, conv_new = _ssd_prompt(proj3, small, proj3s, smalls, meta_tile, w["ssd"], bp, nc)
        c_kv, kpe, kcat, q4 = _qkv_prep(proj3, small, pos_x, w["q_g"], w["kv_g"], w["wq"], w["wuk"], w["inv128"],
                                        w["sign128"], CHUNK, BF16)

        def with_meta(tile, rows):
            f = rows.shape[1]
            return jnp.concatenate([jnp.broadcast_to(tile.astype(rows.dtype)[None], (bp,) + tile.shape),
                                    rows.reshape(bp, seq, f)], axis=1)

        k_all = with_meta(kcat_s[ns_rows:], kcat).reshape(bp * lp, QK_DIM)
        vt = jnp.transpose(k_all.reshape(bp, lp, QK_DIM)[:, :, :kv_lora], (0, 2, 1))
        o = _attn_prompt(q4, k_all, vt, w["wuvt"], bp, lp)
        h2 = _merge(yn, o, proj3, h1, w["wa"], w["wb"], w["wo"], w["mix_post_g"])
        hx = _ffn(h2, *w["ffn2"])
        outs[0].append(with_meta(c_kv_s[meta0:], c_kv))
        outs[1].append(with_meta(kpe_s[meta0:, :QK_ROPE], kpe[:, :QK_ROPE]))
        outs[2].append(h_fin.reshape(bp, nheads, SSM_HEAD_DIM, D_STATE))
        outs[3].append(conv_new)

        yns, h_new, conv_new_s = _ssd_sample(proj3s, smalls, state_conv[l],
                                             state_ssm[l].reshape(bs, d_inner, D_STATE), w["ssd"], bs, tdec)
        os_ = _attn_sample(page_table, q4s, kcat_s[:ns_rows], c_kv_s[:ns_rows], w["foldt"], w["wuv"],
                           cache_kv_latent[l], jnp.transpose(cache_k_rope[l], (0, 2, 1)), bs, tdec)
        g2 = _merge(yns, os_, proj3s, g1, w["wa"], w["wb"], w["wo"], w["mix_post_g"])
        hs = _ffn(g2, *w["ffn2"])
        outs[4].append(c_kv_s[:ns_rows].reshape(bs, tdec, kv_lora))
        outs[5].append(kpe_s[:ns_rows, :QK_ROPE].reshape(bs, tdec, QK_ROPE))
        outs[6].append(h_new.reshape(bs, nheads, SSM_HEAD_DIM, D_STATE))
        outs[7].append(conv_new_s)

    y_prompt = hx.reshape(bp, seq, d)
    y_sample = hs.reshape(bs, tdec, d)
    return (y_prompt, y_sample) + tuple(jnp.stack(o) for o in outs)
```
